```python
import math
import jax, jax.numpy as jnp
from jax import lax
import numpy as np

D_MODEL = 1024
BATCH = 4
SEQ = 4096
DEPTH = 1
DEC_BATCH = 32
DEC_SEQ = 1
PAST_LEN = 16384
PAGE_SIZE = 128

D_MIX = D_MODEL
D_ATT = D_MIX // 2
D_SSM = D_MIX - D_ATT
HEAD_DIM = 64
N_ATT_HEADS = D_ATT // HEAD_DIM
SSM_HEAD_DIM = 64
N_SSM_HEADS = D_SSM // SSM_HEAD_DIM
N_SSM_GROUPS = 2
HEADS_PER_GROUP = N_SSM_HEADS // N_SSM_GROUPS
D_STATE = 128
CONV_W = 4
CONV_DIM = D_SSM + 2 * N_SSM_GROUPS * D_STATE
SSD_CHUNK = 128
Q_BLOCK = 128
D_FF = ((8 * D_MODEL // 3 + 127) // 128) * 128
FFN_RESIDUAL = 0.5
EPS = 1e-6
SPLITS = (D_ATT, 2 * D_ATT, 3 * D_ATT, 3 * D_ATT + N_ATT_HEADS,
          3 * D_ATT + N_ATT_HEADS + D_SSM, 3 * D_ATT + N_ATT_HEADS + D_SSM + CONV_DIM)
N_IN = SPLITS[-1] + N_SSM_HEADS

kernel_name = "hymba_fox_ssd_macaron_step"

F32 = jnp.float32


def rmsnorm(x, g):
    xf = x.astype(F32)
    y = xf * lax.rsqrt(jnp.mean(xf * xf, axis=-1, keepdims=True) + EPS) * g.astype(F32)
    return y.astype(x.dtype)


def group_rmsnorm(x, g, groups):
    shp = x.shape
    xf = x.astype(F32).reshape(*shp[:-1], groups, shp[-1] // groups)
    xf = xf * lax.rsqrt(jnp.mean(xf * xf, axis=-1, keepdims=True) + EPS)
    return xf.reshape(shp) * g.astype(F32)


def half_ffn(x, norm_g, w_in, w_out):
    h = rmsnorm(x, norm_g)
    gate, up = jnp.split(h @ w_in, 2, axis=-1)
    return x + FFN_RESIDUAL * ((jax.nn.silu(gate) * up) @ w_out)


def causal_conv(u, prefix, w, b):
    L = u.shape[1]
    cat = jnp.concatenate([prefix.astype(u.dtype), u], axis=1)
    out = b + sum(w[j] * cat[:, j:j + L] for j in range(CONV_W))
    return out, cat[:, -(CONV_W - 1):]


def mixer_inputs(h, conv_prefix, w_in, b_f, conv_w, conv_b, dt_bias, a_log):
    b, L, _ = h.shape
    q, k, v, f_logit, z, xbc, dt_raw = jnp.split(h @ w_in, list(SPLITS), axis=-1)
    q = q.reshape(b, L, N_ATT_HEADS, HEAD_DIM)
    k = k.reshape(b, L, N_ATT_HEADS, HEAD_DIM)
    v = v.reshape(b, L, N_ATT_HEADS, HEAD_DIM)
    logf = jax.nn.log_sigmoid((f_logit + b_f).astype(F32))
    xbc_c, conv_state = causal_conv(xbc, conv_prefix, conv_w, conv_b)
    xbc_c = jax.nn.silu(xbc_c)
    xs, Bm, Cm = jnp.split(xbc_c, [D_SSM, D_SSM + N_SSM_GROUPS * D_STATE], axis=-1)
    xs = xs.reshape(b, L, N_SSM_HEADS, SSM_HEAD_DIM).astype(F32)
    Bh = jnp.repeat(Bm.reshape(b, L, N_SSM_GROUPS, D_STATE), HEADS_PER_GROUP, axis=2).astype(F32)
    Ch = jnp.repeat(Cm.reshape(b, L, N_SSM_GROUPS, D_STATE), HEADS_PER_GROUP, axis=2).astype(F32)
    dt = jax.nn.softplus((dt_raw + dt_bias).astype(F32))
    A = -jnp.exp(a_log.astype(F32))
    return q, k, v, logf, z, xs, Bh, Ch, dt, A, conv_state


def fox_prompt(q, k, v, logf):
    b, L, H, hd = q.shape
    nb = L // Q_BLOCK
    scale = HEAD_DIM ** -0.5
    c = jnp.cumsum(logf, axis=1)
    cT = c.transpose(0, 2, 1)
    key_pos = jnp.arange(L)
    qb = q.reshape(b, nb, Q_BLOCK, H, hd).transpose(1, 0, 2, 3, 4)
    cb = c.reshape(b, nb, Q_BLOCK, H).transpose(1, 0, 3, 2)

    def block(args):
        qi, ci, i = args
        s = jnp.einsum('bqhd,bkhd->bhqk', qi, k).astype(F32) * scale
        s = s + ci[..., None] - cT[:, :, None, :]
        qpos = i * Q_BLOCK + jnp.arange(Q_BLOCK)
        s = jnp.where(key_pos[None, :] <= qpos[:, None], s, -jnp.inf)
        p = jax.nn.softmax(s, axis=-1)
        return jnp.einsum('bhqk,bkhd->bqhd', p.astype(v.dtype), v)

    o = lax.map(block, (qb, cb, jnp.arange(nb)))
    return o.transpose(1, 0, 2, 3, 4).reshape(b, L, H, hd)


def fox_sample(q, k_new, v_new, logf_new, k_past, v_past, logf_past):
    T = q.shape[1]
    P = k_past.shape[1]
    scale = HEAD_DIM ** -0.5
    rev = lax.cumsum(logf_past, axis=1, reverse=True) - logf_past
    cn = jnp.cumsum(logf_new, axis=1)
    cnT = cn.transpose(0, 2, 1)
    s_past = jnp.einsum('bqhd,bkhd->bhqk', q, k_past).astype(F32) * scale
    s_past = s_past + cnT[..., None] + rev.transpose(0, 2, 1)[:, :, None, :]
    s_new = jnp.einsum('bqhd,bkhd->bhqk', q, k_new).astype(F32) * scale
    s_new = s_new + cnT[..., None] - cnT[:, :, None, :]
    causal = jnp.arange(T)[None, :] <= jnp.arange(T)[:, None]
    s_new = jnp.where(causal, s_new, -jnp.inf)
    p = jax.nn.softmax(jnp.concatenate([s_past, s_new], axis=-1), axis=-1)
    o = jnp.einsum('bhqk,bkhd->bqhd', p[..., :P].astype(v_past.dtype), v_past)
    return o + jnp.einsum('bhqk,bkhd->bqhd', p[..., P:].astype(v_new.dtype), v_new)


def ssd_prompt(x, dt, A, Bh, Ch, d_skip):
    b, L, h, p = x.shape
    n = Bh.shape[-1]
    nc = L // SSD_CHUNK
    r = lambda t: t.reshape(b, nc, SSD_CHUNK, *t.shape[2:])
    xdt = r(x * dt[..., None])
    a_c = jnp.cumsum(r(dt * A), axis=2)
    Bc, Cc = r(Bh), r(Ch)
    seg = a_c[:, :, :, None, :] - a_c[:, :, None, :, :]
    tri = jnp.tril(jnp.ones((SSD_CHUNK, SSD_CHUNK), dtype=bool))
    Lm = jnp.exp(jnp.where(tri[None, None, :, :, None], seg, -jnp.inf))
    scores = jnp.einsum('bcthn,bcshn->bctsh', Cc, Bc) * Lm
    y_diag = jnp.einsum('bctsh,bcshp->bcthp', scores, xdt)
    decay_end = jnp.exp(a_c[:, :, -1:, :] - a_c)
    states = jnp.einsum('bcshn,bcsh,bcshp->bchpn', Bc, decay_end, xdt)
    chunk_decay = jnp.exp(a_c[:, :, -1, :])

    def step(hprev, inp):
        st, dec = inp
        return dec[:, :, None, None] * hprev + st, hprev

    h0 = jnp.zeros((b, h, p, n), F32)
    h_final, h_starts = lax.scan(step, h0, (states.transpose(1, 0, 2, 3, 4),
                                            chunk_decay.transpose(1, 0, 2)))
    h_starts = h_starts.transpose(1, 0, 2, 3, 4)
    y_off = jnp.einsum('bcthn,bchpn,bcth->bcthp', Cc, h_starts, jnp.exp(a_c))
    y = (y_diag + y_off).reshape(b, L, h, p) + d_skip.astype(F32)[:, None] * x
    return y, h_final


def ssd_sample(x, dt, A, Bh, Ch, d_skip, h0):
    def step(hs, inp):
        xt, dtt, Bt, Ct = inp
        hs = jnp.exp(dtt * A)[..., None, None] * hs + jnp.einsum('bhp,bhn->bhpn', xt * dtt[..., None], Bt)
        return hs, jnp.einsum('bhn,bhpn->bhp', Ct, hs)

    h_final, ys = lax.scan(step, h0.astype(F32), (x.swapaxes(0, 1), dt.swapaxes(0, 1),
                                                  Bh.swapaxes(0, 1), Ch.swapaxes(0, 1)))
    y = ys.swapaxes(0, 1) + d_skip.astype(F32)[:, None] * x
    return y, h_final


def mixer_output(att_o, y_ssm, z, att_norm, ssm_norm, w_out):
    b, L = z.shape[:2]
    att = rmsnorm(att_o.reshape(b, L, D_ATT), att_norm)
    g = y_ssm.reshape(b, L, D_SSM) * jax.nn.silu(z.astype(F32))
    g = group_rmsnorm(g, ssm_norm, N_SSM_GROUPS)
    mixed = jnp.concatenate([att.astype(z.dtype), g.astype(z.dtype)], axis=-1)
    return mixed @ w_out


def setup_inputs(seed: int = 0) -> dict:
    key = jax.random.key(seed)
    ks = jax.random.split(key, 32)
    n_pages = PAST_LEN // PAGE_SIZE
    n_used = DEC_BATCH * n_pages
    n_pool = n_used + max(1, n_used // 4)
    nrm = lambda k, shape, s=1.0: jax.random.normal(k, shape, F32) * s
    gain = lambda k, dim: 1.0 + 0.02 * jax.random.normal(k, (DEPTH, dim), F32)
    dt0 = jnp.exp(jax.random.uniform(ks[10], (DEPTH, N_SSM_HEADS), F32,
                                     math.log(1e-3), math.log(1e-1)))
    page_table = jax.random.permutation(ks[7], n_pool)[:n_used].reshape(DEC_BATCH, n_pages).astype(jnp.int32)
    b_f = jnp.linspace(1.0, 5.0, N_ATT_HEADS, dtype=F32)[None, :] + 0.1 * nrm(ks[9], (DEPTH, N_ATT_HEADS))
    logf_bias = jnp.linspace(6.0, 14.0, N_ATT_HEADS, dtype=F32)
    cache_logf = jax.nn.log_sigmoid(logf_bias + 0.5 * nrm(ks[4], (DEPTH, n_pool, PAGE_SIZE, N_ATT_HEADS)))
    return {
        "x_prompt": nrm(ks[0], (BATCH, SEQ, D_MODEL)),
        "x_sample": nrm(ks[1], (DEC_BATCH, DEC_SEQ, D_MODEL)),
        "cache_k": nrm(ks[2], (DEPTH, n_pool, PAGE_SIZE, N_ATT_HEADS, HEAD_DIM)),
        "cache_v": nrm(ks[3], (DEPTH, n_pool, PAGE_SIZE, N_ATT_HEADS, HEAD_DIM)),
        "cache_logf": cache_logf,
        "page_table": page_table,
        "state_conv": nrm(ks[5], (DEPTH, DEC_BATCH, CONV_W - 1, CONV_DIM)),
        "state_ssm": nrm(ks[6], (DEPTH, DEC_BATCH, N_SSM_HEADS, SSM_HEAD_DIM, D_STATE), 0.1),
        "ffn1_norm": gain(ks[8], D_MODEL),
        "w_ffn1_in": nrm(ks[11], (DEPTH, D_MODEL, 2 * D_FF), D_MODEL ** -0.5),
        "w_ffn1_out": nrm(ks[12], (DEPTH, D_FF, D_MODEL), D_FF ** -0.5),
        "mix_norm": gain(ks[13], D_MODEL),
        "w_in": nrm(ks[14], (DEPTH, D_MODEL, N_IN), D_MODEL ** -0.5),
        "b_f": b_f,
        "conv_w": nrm(ks[15], (DEPTH, CONV_W, CONV_DIM), CONV_W ** -0.5),
        "conv_b": nrm(ks[16], (DEPTH, CONV_DIM), 0.02),
        "dt_bias": dt0 + jnp.log(-jnp.expm1(-dt0)),
        "a_log": jnp.log(jax.random.uniform(ks[17], (DEPTH, N_SSM_HEADS), F32, 1.0, 16.0)),
        "d_skip": 1.0 + 0.02 * nrm(ks[18], (DEPTH, N_SSM_HEADS)),
        "att_out_norm": gain(ks[19], D_ATT),
        "ssm_out_norm": gain(ks[20], D_SSM),
        "w_out": nrm(ks[21], (DEPTH, D_MIX, D_MODEL), D_MIX ** -0.5),
        "ffn2_norm": gain(ks[22], D_MODEL),
        "w_ffn2_in": nrm(ks[23], (DEPTH, D_MODEL, 2 * D_FF), D_MODEL ** -0.5),
        "w_ffn2_out": nrm(ks[24], (DEPTH, D_FF, D_MODEL), D_FF ** -0.5),
        "final_norm": 1.0 + 0.02 * nrm(ks[25], (D_MODEL,)),
    }


def reference(x_prompt, x_sample, cache_k, cache_v, cache_logf, page_table, state_conv, state_ssm,
              ffn1_norm, w_ffn1_in, w_ffn1_out, mix_norm, w_in, b_f, conv_w, conv_b, dt_bias, a_log,
              d_skip, att_out_norm, ssm_out_norm, w_out, ffn2_norm, w_ffn2_in, w_ffn2_out, final_norm):
    n_pages = page_table.shape[1]
    past_len = n_pages * PAGE_SIZE
    dec_b = x_sample.shape[0]
    xp, xs = x_prompt, x_sample
    kp_l, vp_l, fp_l, cp_l, sp_l = [], [], [], [], []
    ks_l, vs_l, fs_l, cs_l, ss_l = [], [], [], [], []
    for l in range(DEPTH):
        xp = half_ffn(xp, ffn1_norm[l], w_ffn1_in[l], w_ffn1_out[l])
        xs = half_ffn(xs, ffn1_norm[l], w_ffn1_in[l], w_ffn1_out[l])
        mix_w = (w_in[l], b_f[l], conv_w[l], conv_b[l], dt_bias[l], a_log[l])
        hp = rmsnorm(xp, mix_norm[l])
        zero_prefix = jnp.zeros((xp.shape[0], CONV_W - 1, CONV_DIM), xp.dtype)
        q, k, v, logf, z, xss, Bh, Ch, dt, A, conv_p = mixer_inputs(hp, zero_prefix, *mix_w)
        att = fox_prompt(q, k, v, logf)
        y_ssm, ssm_p = ssd_prompt(xss, dt, A, Bh, Ch, d_skip[l])
        xp = xp + mixer_output(att, y_ssm, z, att_out_norm[l], ssm_out_norm[l], w_out[l])
        kp_l.append(k); vp_l.append(v); fp_l.append(logf); cp_l.append(conv_p); sp_l.append(ssm_p)
        hs = rmsnorm(xs, mix_norm[l])
        q2, k2, v2, logf2, z2, xss2, Bh2, Ch2, dt2, A2, conv_s = mixer_inputs(hs, state_conv[l], *mix_w)
        k_past = cache_k[l][page_table].reshape(dec_b, past_len, N_ATT_HEADS, HEAD_DIM)
        v_past = cache_v[l][page_table].reshape(dec_b, past_len, N_ATT_HEADS, HEAD_DIM)
        f_past = cache_logf[l][page_table].reshape(dec_b, past_len, N_ATT_HEADS).astype(F32)
        att2 = fox_sample(q2, k2, v2, logf2, k_past, v_past, f_past)
        y_ssm2, ssm_s = ssd_sample(xss2, dt2, A2, Bh2, Ch2, d_skip[l], state_ssm[l])
        xs = xs + mixer_output(att2, y_ssm2, z2, att_out_norm[l], ssm_out_norm[l], w_out[l])
        ks_l.append(k2); vs_l.append(v2); fs_l.append(logf2); cs_l.append(conv_s); ss_l.append(ssm_s)
        xp = half_ffn(xp, ffn2_norm[l], w_ffn2_in[l], w_ffn2_out[l])
        xs = half_ffn(xs, ffn2_norm[l], w_ffn2_in[l], w_ffn2_out[l])
    y_prompt = rmsnorm(xp, final_norm)
    y_sample = rmsnorm(xs, final_norm)
    k_prompt = jnp.stack(kp_l); v_prompt = jnp.stack(vp_l); logf_prompt = jnp.stack(fp_l)
    conv_prompt = jnp.stack(cp_l); ssm_prompt = jnp.stack(sp_l)
    k_sample = jnp.stack(ks_l); v_sample = jnp.stack(vs_l); logf_sample = jnp.stack(fs_l)
    conv_sample = jnp.stack(cs_l); ssm_sample = jnp.stack(ss_l)
    return (y_prompt, y_sample, k_prompt, v_prompt, logf_prompt, conv_prompt, ssm_prompt,
            k_sample, v_sample, logf_sample, conv_sample, ssm_sample)
```

```python
import functools

import numpy as np
import jax
import jax.numpy as jnp
from jax import lax
from jax.experimental import pallas as pl
from jax.experimental.pallas import tpu as pltpu

F32 = jnp.float32
BF16 = jnp.bfloat16

D_MODEL = 1024
D_ATT = 512
D_SSM = 512
HEAD_DIM = 64
N_ATT_HEADS = 8
N_SSM_HEADS = 8
SSM_HEAD_DIM = 64
N_SSM_GROUPS = 2
D_STATE = 128
CONV_W = 4
CONV_DIM = D_SSM + 2 * N_SSM_GROUPS * D_STATE
SSD_CHUNK = 128
PAGE_SIZE = 128
D_FF = 2816
FFN_RESIDUAL = 0.5
EPS = 1e-6
ATT_SCALE = HEAD_DIM ** -0.5
LANES = 128
DT_LANE0 = 8
AUG0 = HEAD_DIM
NEG_BIG = -1e30

VMEM_LIMIT = 56 * 1024 * 1024


def _cparams(sem):
    return pltpu.CompilerParams(dimension_semantics=sem, vmem_limit_bytes=VMEM_LIMIT)


def _dot(a, b):
    return jnp.dot(a, b, preferred_element_type=F32)


def _dot_nt(a, b):
    return lax.dot_general(a, b, (((1,), (1,)), ((), ())), preferred_element_type=F32)


def _split3(x):
    hi = x.astype(BF16)
    r1 = x - hi.astype(F32)
    mid = r1.astype(BF16)
    lo = (r1 - mid.astype(F32)).astype(BF16)
    return hi, mid, lo


def _dot3(x, m):
    hi, mid, lo = _split3(x)
    return _dot(hi, m) + _dot(mid, m) + _dot(lo, m)


def _dot3r(m, x):
    hi, mid, lo = _split3(x)
    return _dot(m, hi) + _dot(m, mid) + _dot(m, lo)


def _rms(x, g):
    return x * lax.rsqrt(jnp.mean(x * x, axis=-1, keepdims=True) + EPS) * g


def _silu(x):
    return x / (1.0 + jnp.exp(-x))


def _tril(n, dtype=BF16):
    r = lax.broadcasted_iota(jnp.int32, (n, n), 0)
    c = lax.broadcasted_iota(jnp.int32, (n, n), 1)
    return (c <= r).astype(dtype)


def _row_to_col(row):
    n = row.shape[1]
    r = lax.broadcasted_iota(jnp.int32, (n, n), 0)
    c = lax.broadcasted_iota(jnp.int32, (n, n), 1)
    return jnp.sum(jnp.where(r == c, jnp.broadcast_to(row, (n, n)), 0.0), axis=1, keepdims=True)


def _col_to_row(col):
    n = col.shape[0]
    r = lax.broadcasted_iota(jnp.int32, (n, n), 0)
    c = lax.broadcasted_iota(jnp.int32, (n, n), 1)
    return jnp.sum(jnp.where(r == c, jnp.broadcast_to(col, (n, n)), 0.0), axis=0, keepdims=True)


def _ffn_body(x_ref, g_ref, wg_ref, wu_ref, wo_ref, o_ref, h_scr, acc_scr, *, nk):
    k = pl.program_id(1)

    @pl.when(k == 0)
    def _():
        h_scr[...] = _rms(x_ref[...], g_ref[...]).astype(BF16)
        acc_scr[...] = jnp.zeros_like(acc_scr)

    h = h_scr[...]
    gate = _dot(h, wg_ref[...])
    up = _dot(h, wu_ref[...])
    a = (_silu(gate) * up).astype(BF16)
    acc_scr[...] += _dot(a, wo_ref[...])

    @pl.when(k == nk - 1)
    def _():
        o_ref[...] = x_ref[...] + FFN_RESIDUAL * acc_scr[...]


def _ffn(x, norm_g, w_in_b, w_out_b, *, tm, tf):
    n = x.shape[0]
    nk = D_FF // tf
    return pl.pallas_call(
        functools.partial(_ffn_body, nk=nk),
        grid=(n // tm, nk),
        in_specs=[
            pl.BlockSpec((tm, D_MODEL), lambda i, k: (i, 0)),
            pl.BlockSpec((1, D_MODEL), lambda i, k: (0, 0)),
            pl.BlockSpec((D_MODEL, tf), lambda i, k: (0, k)),
            pl.BlockSpec((D_MODEL, tf), lambda i, k: (0, nk + k)),
            pl.BlockSpec((tf, D_MODEL), lambda i, k: (k, 0)),
        ],
        out_specs=pl.BlockSpec((tm, D_MODEL), lambda i, k: (i, 0)),
        out_shape=jax.ShapeDtypeStruct((n, D_MODEL), F32),
        scratch_shapes=[pltpu.VMEM((tm, D_MODEL), BF16), pltpu.VMEM((tm, D_MODEL), F32)],
        compiler_params=_cparams(("arbitrary", "arbitrary")),
        name="ffn1",
    )(x, norm_g, w_in_b, w_in_b, w_out_b)


def _ffn2_body(x_ref, att_ref, gn_ref, ag_ref, wm_ref, g_ref, wg_ref, wu_ref, wo_ref, fg_ref,
               o_ref, h_scr, acc_scr, x2_scr, *, nk):
    k = pl.program_id(1)

    @pl.when(k == 0)
    def _():
        an = _rms(att_ref[...], ag_ref[...]).astype(BF16)
        x2 = (x_ref[...] + _dot(an, wm_ref[:D_ATT, :]) + _dot(gn_ref[...], wm_ref[D_ATT:, :]))
        x2_scr[...] = x2
        h_scr[...] = _rms(x2, g_ref[...]).astype(BF16)
        acc_scr[...] = jnp.zeros_like(acc_scr)

    h = h_scr[...]
    gate = _dot(h, wg_ref[...])
    up = _dot(h, wu_ref[...])
    a = (_silu(gate) * up).astype(BF16)
    acc_scr[...] += _dot(a, wo_ref[...])

    @pl.when(k == nk - 1)
    def _():
        y = x2_scr[...] + FFN_RESIDUAL * acc_scr[...]
        o_ref[...] = _rms(y, fg_ref[...])


def _ffn2(x1, att, gn, att_g, w_mix_b, norm_g, w_in_b, w_out_b, final_g, *, tm, tf):
    n = x1.shape[0]
    nk = D_FF // tf
    return pl.pallas_call(
        functools.partial(_ffn2_body, nk=nk),
        grid=(n // tm, nk),
        in_specs=[
            pl.BlockSpec((tm, D_MODEL), lambda i, k: (i, 0)),
            pl.BlockSpec((tm, D_ATT), lambda i, k: (i, 0)),
            pl.BlockSpec((tm, D_SSM), lambda i, k: (i, 0)),
            pl.BlockSpec((1, D_ATT), lambda i, k: (0, 0)),
            pl.BlockSpec((D_MODEL, D_MODEL), lambda i, k: (0, 0)),
            pl.BlockSpec((1, D_MODEL), lambda i, k: (0, 0)),
            pl.BlockSpec((D_MODEL, tf), lambda i, k: (0, k)),
            pl.BlockSpec((D_MODEL, tf), lambda i, k: (0, nk + k)),
            pl.BlockSpec((tf, D_MODEL), lambda i, k: (k, 0)),
            pl.BlockSpec((1, D_MODEL), lambda i, k: (0, 0)),
        ],
        out_specs=pl.BlockSpec((tm, D_MODEL), lambda i, k: (i, 0)),
        out_shape=jax.ShapeDtypeStruct((n, D_MODEL), F32),
        scratch_shapes=[pltpu.VMEM((tm, D_MODEL), BF16), pltpu.VMEM((tm, D_MODEL), F32),
                        pltpu.VMEM((tm, D_MODEL), F32)],
        compiler_params=_cparams(("arbitrary", "arbitrary")),
        name="ffn2",
    )(x1, att, gn, att_g, w_mix_b, norm_g, w_in_b, w_in_b, w_out_b, final_g)


def _small_slab(raw, bias):
    v = raw + bias
    t = jnp.log1p(jnp.exp(-jnp.abs(v)))
    lane = lax.broadcasted_iota(jnp.int32, v.shape, 1)
    logf = -(jnp.maximum(-v, 0.0) + t)
    dt = jnp.maximum(v, 0.0) + t
    return jnp.where(lane < DT_LANE0, logf, jnp.where(lane < 2 * DT_LANE0, dt, 0.0))


def _inproj_aug_body(x_ref, g_ref, wq_ref, wk_ref, wv_ref, wz_ref, wx_ref, ws_ref, bs_ref,
                     eq_ref, ek_ref,
                     k_ref, v_ref, qa_ref, ka_ref, vb_ref, z_ref, xbc_ref, slab_ref,
                     carry_scr, *, tiles_per_seq, tm):
    i = pl.program_id(0)

    @pl.when(i % tiles_per_seq == 0)
    def _():
        carry_scr[...] = jnp.zeros_like(carry_scr)

    h = _rms(x_ref[...], g_ref[...]).astype(BF16)
    q = _dot(h, wq_ref[...])
    k = _dot(h, wk_ref[...])
    v = _dot(h, wv_ref[...])
    k_ref[...] = k
    v_ref[...] = v
    vb_ref[...] = v.astype(BF16)
    z_ref[...] = _dot(h, wz_ref[...])
    xbc_ref[...] = _dot(h, wx_ref[...])
    slab = _small_slab(_dot(h, ws_ref[...]), bs_ref[...])
    slab_ref[...] = slab

    lane = lax.broadcasted_iota(jnp.int32, slab.shape, 1)
    logf = jnp.where(lane < DT_LANE0, slab, 0.0)
    c = _dot3r(_tril(tm), logf) + carry_scr[0:1, :]
    carry_scr[0:1, :] = c[tm - 1:tm, :]
    c_hi = c.astype(BF16).astype(F32)
    r1 = c - c_hi
    c_mid = r1.astype(BF16).astype(F32)
    c_lo = (r1 - c_mid).astype(BF16).astype(F32)
    aug = (c_hi + pltpu.roll(c_mid, DT_LANE0, 1) + pltpu.roll(c_lo, 2 * DT_LANE0, 1)
           + jnp.where(lane == 3 * DT_LANE0, 1.0, 0.0)).astype(BF16)
    qs = (q * ATT_SCALE).astype(BF16)
    qa_ref[...] = (_dot(qs, eq_ref[:D_ATT, :]) + _dot(aug, eq_ref[D_ATT:, :])).astype(BF16)
    ka_ref[...] = (_dot(k.astype(BF16), ek_ref[:D_ATT, :]) + _dot(aug, ek_ref[D_ATT:, :])).astype(BF16)


def _aug_scatter_mats():
    eq = np.zeros((D_ATT + LANES, N_ATT_HEADS * LANES), np.float32)
    ek = np.zeros((D_ATT + LANES, N_ATT_HEADS * LANES), np.float32)
    one_lane = D_ATT + 3 * DT_LANE0
    for h in range(N_ATT_HEADS):
        for d in range(HEAD_DIM):
            eq[h * HEAD_DIM + d, h * LANES + d] = 1.0
            ek[h * HEAD_DIM + d, h * LANES + d] = 1.0
        for p in range(3):
            eq[D_ATT + p * DT_LANE0 + h, h * LANES + AUG0 + p] = 1.0
            eq[one_lane, h * LANES + AUG0 + 3 + p] = 1.0
            ek[one_lane, h * LANES + AUG0 + p] = 1.0
            ek[D_ATT + p * DT_LANE0 + h, h * LANES + AUG0 + 3 + p] = -1.0
    return jnp.asarray(eq, BF16), jnp.asarray(ek, BF16)


def _inproj_aug(x1, norm_g, ws, bias_s, *, tm, seq):
    n = x1.shape[0]
    eq, ek = _aug_scatter_mats()
    row = lambda w: pl.BlockSpec((tm, w), lambda i: (i, 0))
    full = lambda a: pl.BlockSpec(a.shape, lambda i: (0, 0))
    wq, wk, wv, wz, wx, wsm = ws
    return pl.pallas_call(
        functools.partial(_inproj_aug_body, tiles_per_seq=seq // tm, tm=tm),
        grid=(n // tm,),
        in_specs=[row(D_MODEL), full(norm_g), full(wq), full(wk), full(wv), full(wz), full(wx),
                  full(wsm), full(bias_s), full(eq), full(ek)],
        out_specs=[row(D_ATT), row(D_ATT), row(N_ATT_HEADS * LANES), row(N_ATT_HEADS * LANES),
                   row(D_ATT), row(D_SSM), row(CONV_DIM), row(LANES)],
        out_shape=[
            jax.ShapeDtypeStruct((n, D_ATT), F32), jax.ShapeDtypeStruct((n, D_ATT), F32),
            jax.ShapeDtypeStruct((n, N_ATT_HEADS * LANES), BF16),
            jax.ShapeDtypeStruct((n, N_ATT_HEADS * LANES), BF16),
            jax.ShapeDtypeStruct((n, D_ATT), BF16), jax.ShapeDtypeStruct((n, D_SSM), F32),
            jax.ShapeDtypeStruct((n, CONV_DIM), F32), jax.ShapeDtypeStruct((n, LANES), F32)],
        scratch_shapes=[pltpu.VMEM((8, LANES), F32)],
        compiler_params=_cparams(("arbitrary",)),
        name="inproj_prompt",
    )(x1, norm_g, wq, wk, wv, wz, wx, wsm, bias_s, eq, ek)


def _inproj_plain_body(x_ref, g_ref, wq_ref, wk_ref, wv_ref, wz_ref, wx_ref, ws_ref, bs_ref,
                       q_ref, k_ref, v_ref, z_ref, xbc_ref, slab_ref):
    h = _rms(x_ref[...], g_ref[...]).astype(BF16)
    q_ref[...] = _dot(h, wq_ref[...])
    k_ref[...] = _dot(h, wk_ref[...])
    v_ref[...] = _dot(h, wv_ref[...])
    z_ref[...] = _dot(h, wz_ref[...])
    xbc_ref[...] = _dot(h, wx_ref[...])
    slab_ref[...] = _small_slab(_dot(h, ws_ref[...]), bs_ref[...])


def _inproj_plain(x1, norm_g, ws, bias_s):
    n = x1.shape[0]
    wq, wk, wv, wz, wx, wsm = ws
    full = lambda a: pl.BlockSpec(a.shape, lambda i: (0, 0))
    out = lambda w: pl.BlockSpec((n, w), lambda i: (0, 0))
    return pl.pallas_call(
        _inproj_plain_body,
        grid=(1,),
        in_specs=[full(x1), full(norm_g), full(wq), full(wk), full(wv), full(wz), full(wx),
                  full(wsm), full(bias_s)],
        out_specs=[out(D_ATT), out(D_ATT), out(D_ATT), out(D_SSM), out(CONV_DIM), out(LANES)],
        out_shape=[jax.ShapeDtypeStruct((n, w), F32)
                   for w in (D_ATT, D_ATT, D_ATT, D_SSM, CONV_DIM, LANES)],
        compiler_params=_cparams(("arbitrary",)),
        name="inproj_sample",
    )(x1, norm_g, wq, wk, wv, wz, wx, wsm, bias_s)


def _attn_body(q_ref, k_ref, v_ref, o_ref, *, tq):
    qi = pl.program_id(2)
    outs = []
    for j in range(2):
        q = q_ref[0, :, j * LANES:(j + 1) * LANES]

        def tile(ki, carry, masked):
            m, l, acc = carry
            start = pl.multiple_of(ki * tq, tq)
            kt = k_ref[0, pl.ds(start, tq), j * LANES:(j + 1) * LANES]
            vt = v_ref[0, pl.ds(start, tq), :]
            s = _dot_nt(q, kt)
            if masked:
                r = lax.broadcasted_iota(jnp.int32, s.shape, 0)
                c = lax.broadcasted_iota(jnp.int32, s.shape, 1)
                s = jnp.where(c <= r, s, NEG_BIG)
            m_new = jnp.maximum(m, jnp.max(s, axis=1, keepdims=True))
            p = jnp.exp(s - m_new)
            alpha = jnp.exp(m - m_new)
            l = alpha * l + jnp.sum(p, axis=1, keepdims=True)
            acc = alpha * acc + _dot(p.astype(BF16), vt)
            return m_new, l, acc

        init = (jnp.full((tq, 1), NEG_BIG, F32), jnp.zeros((tq, 1), F32), jnp.zeros((tq, LANES), F32))
        carry = lax.fori_loop(0, qi, lambda ki, c: tile(ki, c, False), init)
        m, l, acc = tile(qi, carry, True)
        outs.append(acc / l)
    lane = lax.broadcasted_iota(jnp.int32, (tq, LANES), 1)
    o_ref[0] = jnp.where(lane < HEAD_DIM, outs[0], outs[1])


def _attention(qa, ka, vb, *, tq):
    b, L, _ = qa.shape
    return pl.pallas_call(
        functools.partial(_attn_body, tq=tq),
        grid=(b, N_ATT_HEADS // 2, L // tq),
        in_specs=[
            pl.BlockSpec((1, tq, 2 * LANES), lambda bi, hp, qi: (bi, qi, hp)),
            pl.BlockSpec((1, L, 2 * LANES), lambda bi, hp, qi: (bi, 0, hp)),
            pl.BlockSpec((1, L, LANES), lambda bi, hp, qi: (bi, 0, hp)),
        ],
        out_specs=pl.BlockSpec((1, tq, LANES), lambda bi, hp, qi: (bi, qi, hp)),
        out_shape=jax.ShapeDtypeStruct((b, L, D_ATT), F32),
        compiler_params=_cparams(("arbitrary", "arbitrary", "arbitrary")),
        name="fox_prompt",
    )(qa, ka, vb)


def _head_expand_mat():
    e = np.zeros((LANES, D_SSM), np.float32)
    for h in range(N_SSM_HEADS):
        e[DT_LANE0 + h, h * SSM_HEAD_DIM:(h + 1) * SSM_HEAD_DIM] = 1.0
    return jnp.asarray(e, BF16)


def _gate_groupnorm(y, z, gain):
    g = y * _silu(z)
    half = D_SSM // N_SSM_GROUPS
    parts = []
    for gi in range(N_SSM_GROUPS):
        gg = g[:, gi * half:(gi + 1) * half]
        parts.append(gg * lax.rsqrt(jnp.mean(gg * gg, axis=-1, keepdims=True) + EPS))
    return jnp.concatenate(parts, axis=1) * gain


def _ssd_body(xbc_ref, slab_ref, z_ref, cw_ref, cb_ref, arow_ref, dsk_ref, sg_ref, e8_ref,
              gn_ref, st_ref, conv_scr, ht_scr, *, n_chunks):
    c = pl.program_id(1)
    Q = SSD_CHUNK

    @pl.when(c == 0)
    def _():
        conv_scr[0:8, :] = jnp.zeros((8, CONV_DIM), F32)
        ht_scr[...] = jnp.zeros_like(ht_scr)

    u = xbc_ref[0]
    conv_scr[8:8 + Q, :] = u
    conv = (cb_ref[...] + cw_ref[0:1, :] * conv_scr[5:5 + Q, :] + cw_ref[1:2, :] * conv_scr[6:6 + Q, :]
            + cw_ref[2:3, :] * conv_scr[7:7 + Q, :] + cw_ref[3:4, :] * u)
    conv_scr[0:8, :] = u[Q - 8:Q, :]
    xc = _silu(conv)
    xs = xc[:, :D_SSM]
    e8 = e8_ref[...]

    slab = slab_ref[0]
    lane = lax.broadcasted_iota(jnp.int32, slab.shape, 1)
    dtm = jnp.where((lane >= DT_LANE0) & (lane < 2 * DT_LANE0), slab, 0.0)
    a = dtm * arow_ref[...]
    a_c = _dot3r(_tril(Q), a)
    a_ct = a_c.T
    a_last = a_c[Q - 1:Q, :]
    ea_x = _dot3(jnp.exp(a_c), e8)
    dec_x = _dot3(jnp.exp(a_last - a_c), e8)
    xdt = xs * _dot3(dtm, e8)
    xdt_b = xdt.astype(BF16)
    xd_b = (xdt * dec_x).astype(BF16)
    cd_x = ea_x[Q - 1:Q, :]

    r_i = lax.broadcasted_iota(jnp.int32, (Q, Q), 0)
    c_i = lax.broadcasted_iota(jnp.int32, (Q, Q), 1)
    tri = c_i <= r_i
    lane_q = lax.broadcasted_iota(jnp.int32, (Q, LANES), 1)
    ht_old = ht_scr[...]
    gw = D_SSM // N_SSM_GROUPS
    hpg = N_SSM_HEADS // N_SSM_GROUPS
    y_parts, ht_parts = [], []
    for g in range(N_SSM_GROUPS):
        bm = xc[:, D_SSM + g * D_STATE:D_SSM + (g + 1) * D_STATE]
        cm = xc[:, D_SSM + (N_SSM_GROUPS + g) * D_STATE:D_SSM + (N_SSM_GROUPS + g + 1) * D_STATE]
        cb16 = cm.astype(BF16)
        cbm = _dot_nt(cb16, bm.astype(BF16))
        y_off = _dot(cb16, ht_old[:, g * gw:(g + 1) * gw].astype(BF16))
        y_diag = []
        for pr in range(hpg // 2):
            halves = []
            for hh in range(2):
                h = g * hpg + pr * 2 + hh
                col = a_c[:, DT_LANE0 + h:DT_LANE0 + h + 1]
                row = a_ct[DT_LANE0 + h:DT_LANE0 + h + 1, :]
                lm = jnp.where(tri, jnp.exp(col - row), 0.0)
                sc = (cbm * lm).astype(BF16)
                lo = g * gw + pr * LANES
                halves.append(_dot(sc, xdt_b[:, lo:lo + LANES]))
            y_diag.append(jnp.where(lane_q < SSM_HEAD_DIM, halves[0], halves[1]))
        y_parts.append(jnp.concatenate(y_diag, axis=1) + y_off * ea_x[:, g * gw:(g + 1) * gw])
        new = _dot(bm.T.astype(BF16), xd_b[:, g * gw:(g + 1) * gw])
        ht_parts.append(ht_old[:, g * gw:(g + 1) * gw] * cd_x[:, g * gw:(g + 1) * gw] + new)
    ht_new = jnp.concatenate(ht_parts, axis=1)
    ht_scr[...] = ht_new
    y = jnp.concatenate(y_parts, axis=1) + dsk_ref[...] * xs
    gn_ref[0] = _gate_groupnorm(y, z_ref[0], sg_ref[...]).astype(BF16)

    @pl.when(c == n_chunks - 1)
    def _():
        st_ref[0] = ht_new.T


def _ssd_prompt(xbc, slab, z, conv_w, conv_b, a_row, dsk_x, ssm_g):
    b, L, _ = xbc.shape
    nc = L // SSD_CHUNK
    e8 = _head_expand_mat()
    blk = lambda w: pl.BlockSpec((1, SSD_CHUNK, w), lambda bi, ci: (bi, ci, 0))
    full = lambda a: pl.BlockSpec(a.shape, lambda bi, ci: (0, 0))
    return pl.pallas_call(
        functools.partial(_ssd_body, n_chunks=nc),
        grid=(b, nc),
        in_specs=[blk(CONV_DIM), blk(LANES), blk(D_SSM), full(conv_w), full(conv_b), full(a_row),
                  full(dsk_x), full(ssm_g), full(e8)],
        out_specs=[blk(D_SSM), pl.BlockSpec((1, D_SSM, D_STATE), lambda bi, ci: (bi, 0, 0))],
        out_shape=[jax.ShapeDtypeStruct((b, L, D_SSM), BF16),
                   jax.ShapeDtypeStruct((b, D_SSM, D_STATE), F32)],
        scratch_shapes=[pltpu.VMEM((8 + SSD_CHUNK, CONV_DIM), F32), pltpu.VMEM((D_STATE, D_SSM), F32)],
        compiler_params=_cparams(("arbitrary", "arbitrary")),
        name="ssd_prompt",
    )(xbc, slab, z, conv_w, conv_b, a_row, dsk_x, ssm_g, e8)


def _ssd_step_body(xbc_ref, sc_ref, slab_ref, z_ref, st_ref, cw_ref, cb_ref, arow_ref, dsk_ref,
                   sg_ref, e8_ref, gn_ref, so_ref, xc_scr, dtx_scr, decx_scr):
    b = pl.program_id(0)

    @pl.when(b == 0)
    def _():
        conv = (cb_ref[...] + cw_ref[0:1, :] * sc_ref[0] + cw_ref[1:2, :] * sc_ref[1]
                + cw_ref[2:3, :] * sc_ref[2] + cw_ref[3:4, :] * xbc_ref[...])
        xc_scr[...] = _silu(conv)
        slab = slab_ref[...]
        lane = lax.broadcasted_iota(jnp.int32, slab.shape, 1)
        dtm = jnp.where((lane >= DT_LANE0) & (lane < 2 * DT_LANE0), slab, 0.0)
        e8 = e8_ref[...]
        dtx_scr[...] = _dot3(dtm, e8)
        decx_scr[...] = _dot3(jnp.exp(dtm * arow_ref[...]), e8)

    xrow = xc_scr[pl.ds(b, 1), :]
    xs = xrow[:, :D_SSM]
    xdt = xs * dtx_scr[pl.ds(b, 1), :]
    dec = decx_scr[pl.ds(b, 1), :]
    gw = D_SSM // N_SSM_GROUPS
    y_rows = []
    for j in range(D_SSM // LANES):
        g = (j * LANES) // gw
        bm = xrow[:, D_SSM + g * D_STATE:D_SSM + (g + 1) * D_STATE]
        cm = xrow[:, D_SSM + (N_SSM_GROUPS + g) * D_STATE:D_SSM + (N_SSM_GROUPS + g + 1) * D_STATE]
        xcol = _row_to_col(xdt[:, j * LANES:(j + 1) * LANES])
        dcol = _row_to_col(dec[:, j * LANES:(j + 1) * LANES])
        hs = dcol * st_ref[0, j * LANES:(j + 1) * LANES, :] + xcol * bm
        so_ref[0, j * LANES:(j + 1) * LANES, :] = hs
        y_rows.append(_col_to_row(jnp.sum(hs * cm, axis=1, keepdims=True)))
    y = jnp.concatenate(y_rows, axis=1) + dsk_ref[...] * xs
    gn_ref[0] = _gate_groupnorm(y, z_ref[pl.ds(b, 1), :], sg_ref[...]).astype(BF16)


def _ssd_step(xbc, sconv_t, slab, z, state, conv_w, conv_b, a_row, dsk_x, ssm_g):
    nb = xbc.shape[0]
    e8 = _head_expand_mat()
    full2 = lambda a: pl.BlockSpec(a.shape, lambda bi: (0,) * a.ndim)
    gn, st = pl.pallas_call(
        _ssd_step_body,
        grid=(nb,),
        in_specs=[full2(xbc), full2(sconv_t), full2(slab), full2(z),
                  pl.BlockSpec((1, D_SSM, D_STATE), lambda bi: (bi, 0, 0)),
                  full2(conv_w), full2(conv_b), full2(a_row), full2(dsk_x), full2(ssm_g), full2(e8)],
        out_specs=[pl.BlockSpec((1, 1, D_SSM), lambda bi: (bi, 0, 0)),
                   pl.BlockSpec((1, D_SSM, D_STATE), lambda bi: (bi, 0, 0))],
        out_shape=[jax.ShapeDtypeStruct((nb, 1, D_SSM), BF16),
                   jax.ShapeDtypeStruct((nb, D_SSM, D_STATE), F32)],
        scratch_shapes=[pltpu.VMEM((nb, CONV_DIM), F32), pltpu.VMEM((nb, D_SSM), F32),
                        pltpu.VMEM((nb, D_SSM), F32)],
        compiler_params=_cparams(("arbitrary",)),
        name="ssd_step",
    )(xbc, sconv_t, slab, z, state, conv_w, conv_b, a_row, dsk_x, ssm_g, e8)
    return gn.reshape(nb, D_SSM), st


def _decode_body(pt_ref, q_ref, k2_ref, v2_ref, lf2_ref, *refs, pps, n_groups):
    k_refs = refs[:pps]
    v_refs = refs[pps:2 * pps]
    lf_refs = refs[2 * pps:3 * pps]
    o_ref = refs[3 * pps]
    qblk_scr, m_scr, l_scr, acc_scr, carry_scr = refs[3 * pps + 1:]
    g = pl.program_id(1)
    H, HD, P = N_ATT_HEADS, HEAD_DIM, PAGE_SIZE
    r8 = lax.broadcasted_iota(jnp.int32, (H, D_ATT), 0)
    c8 = lax.broadcasted_iota(jnp.int32, (H, D_ATT), 1)
    own_head = c8 // HD == r8

    @pl.when(g == 0)
    def _():
        qs = (q_ref[0] * ATT_SCALE).astype(BF16).astype(F32)
        qblk = jnp.where(own_head, qs, 0.0)
        qblk_scr[...] = qblk.astype(BF16)
        k2 = k2_ref[0].astype(BF16).astype(F32)
        m_scr[...] = jnp.broadcast_to(jnp.sum(qblk * k2, axis=1, keepdims=True), m_scr.shape)
        l_scr[...] = jnp.ones_like(l_scr)
        acc_scr[...] = jnp.broadcast_to(v2_ref[0].astype(BF16).astype(F32), acc_scr.shape)
        carry_scr[...] = jnp.broadcast_to(_row_to_col(lf2_ref[0])[0:H, :], carry_scr.shape)

    kcat = jnp.concatenate([kr[0].astype(BF16) for kr in k_refs], axis=1)
    s = _dot(qblk_scr[...], kcat)
    x = jnp.concatenate([lr[0] for lr in lf_refs], axis=0)
    rr = lax.broadcasted_iota(jnp.int32, (P, P), 0)
    cc = lax.broadcasted_iota(jnp.int32, (P, P), 1)
    later = (rr > cc).astype(BF16)
    rev_local = _dot3(x, later)
    tot = jnp.sum(x, axis=1, keepdims=True)
    carry = carry_scr[:, 0:1]
    s_pages = [None] * pps
    for i in reversed(range(pps)):
        s_pages[i] = s[:, i * P:(i + 1) * P] + rev_local[i * H:(i + 1) * H, :] + carry
        carry = carry + tot[i * H:(i + 1) * H, :]
    carry_scr[...] = jnp.broadcast_to(carry, carry_scr.shape)
    st = jnp.concatenate(s_pages, axis=1)

    m_old = m_scr[:, 0:1]
    m_new = jnp.maximum(m_old, jnp.max(st, axis=1, keepdims=True))
    alpha = jnp.exp(m_old - m_new)
    p = jnp.exp(st - m_new)
    l_new = alpha * l_scr[:, 0:1] + jnp.sum(p, axis=1, keepdims=True)
    vcat = jnp.concatenate([vr[0].astype(BF16) for vr in v_refs], axis=1)
    acc = alpha * acc_scr[...] + _dot_nt(p.astype(BF16), vcat)
    m_scr[...] = jnp.broadcast_to(m_new, m_scr.shape)
    l_scr[...] = jnp.broadcast_to(l_new, l_scr.shape)
    acc_scr[...] = acc

    @pl.when(g == n_groups - 1)
    def _():
        o_ref[0] = jnp.sum(jnp.where(own_head, acc / l_new, 0.0), axis=0, keepdims=True)


def _decode_attention(q, k2, v2, slab2, cache_k, cache_v, cache_logf, page_table, *, pps):
    nb, n_pages = page_table.shape
    ng = n_pages // pps
    row3 = lambda a: a.reshape(nb, 1, a.shape[-1])
    rowspec = lambda w: pl.BlockSpec((1, 1, w), lambda b, g, pt: (b, 0, 0))

    def page_spec(rows, i):
        return pl.BlockSpec((1, rows, PAGE_SIZE),
                            lambda b, g, pt, i=i: (pt[b, (ng - 1 - g) * pps + i], 0, 0))

    in_specs = ([rowspec(D_ATT), rowspec(D_ATT), rowspec(D_ATT), rowspec(LANES)]
                + [page_spec(D_ATT, i) for i in range(pps)]
                + [page_spec(D_ATT, i) for i in range(pps)]
                + [page_spec(N_ATT_HEADS, i) for i in range(pps)])
    out = pl.pallas_call(
        functools.partial(_decode_body, pps=pps, n_groups=ng),
        grid_spec=pltpu.PrefetchScalarGridSpec(
            num_scalar_prefetch=1,
            grid=(nb, ng),
            in_specs=in_specs,
            out_specs=pl.BlockSpec((1, 1, D_ATT), lambda b, g, pt: (b, 0, 0)),
            scratch_shapes=[pltpu.VMEM((N_ATT_HEADS, D_ATT), BF16), pltpu.VMEM((8, LANES), F32),
                            pltpu.VMEM((8, LANES), F32), pltpu.VMEM((8, D_ATT), F32),
                            pltpu.VMEM((8, LANES), F32)]),
        out_shape=jax.ShapeDtypeStruct((nb, 1, D_ATT), F32),
        compiler_params=_cparams(("arbitrary", "arbitrary")),
        name="fox_decode",
    )(page_table, row3(q), row3(k2), row3(v2), row3(slab2),
      *([cache_k] * pps), *([cache_v] * pps), *([cache_logf] * pps))
    return out.reshape(nb, D_ATT)


def _lane_pad(vec, lane0):
    out = jnp.zeros((1, LANES), F32)
    return out.at[0, lane0:lane0 + vec.shape[0]].set(vec.astype(F32))


def kernel(x_prompt, x_sample, cache_k, cache_v, cache_logf, page_table, state_conv, state_ssm,
           ffn1_norm, w_ffn1_in, w_ffn1_out, mix_norm, w_in, b_f, conv_w, conv_b, dt_bias, a_log,
           d_skip, att_out_norm, ssm_out_norm, w_out, ffn2_norm, w_ffn2_in, w_ffn2_out, final_norm):
    depth = w_in.shape[0]
    assert depth == 1
    B, L, _ = x_prompt.shape
    nb, T, _ = x_sample.shape
    assert T == 1
    H, HD = N_ATT_HEADS, HEAD_DIM

    l0 = 0
    w1i, w1o = w_ffn1_in[l0].astype(BF16), w_ffn1_out[l0].astype(BF16)
    w2i, w2o = w_ffn2_in[l0].astype(BF16), w_ffn2_out[l0].astype(BF16)
    wi = w_in[l0]
    o_f = 3 * D_ATT
    o_z = o_f + H
    o_x = o_z + D_SSM
    o_dt = o_x + CONV_DIM
    w_small = jnp.zeros((D_MODEL, LANES), F32)
    w_small = w_small.at[:, 0:H].set(wi[:, o_f:o_f + H])
    w_small = w_small.at[:, DT_LANE0:DT_LANE0 + N_SSM_HEADS].set(wi[:, o_dt:o_dt + N_SSM_HEADS])
    ws = tuple(w.astype(BF16) for w in (wi[:, 0:D_ATT], wi[:, D_ATT:2 * D_ATT], wi[:, 2 * D_ATT:3 * D_ATT],
                                        wi[:, o_z:o_z + D_SSM], wi[:, o_x:o_x + CONV_DIM], w_small))
    bias_s = _lane_pad(b_f[l0], 0) + _lane_pad(dt_bias[l0], DT_LANE0)
    a_row = _lane_pad(-jnp.exp(a_log[l0].astype(F32)), DT_LANE0)
    dsk_x = jnp.repeat(d_skip[l0].astype(F32), SSM_HEAD_DIM)[None, :]
    row = lambda v: v.astype(F32)[None, :]
    g1, gm, g2, gf = row(ffn1_norm[l0]), row(mix_norm[l0]), row(ffn2_norm[l0]), row(final_norm)
    ga, gs = row(att_out_norm[l0]), row(ssm_out_norm[l0])
    cw, cb = conv_w[l0].astype(F32), row(conv_b[l0])
    wm = w_out[l0].astype(BF16)

    TM, TF = 512, 1408
    xp = x_prompt.reshape(B * L, D_MODEL)
    x1 = _ffn(xp, g1, w1i, w1o, tm=TM, tf=TF)
    k_p, v_p, qa, ka, vb, z_p, xbc_p, slab_p = _inproj_aug(x1, gm, ws, bias_s, tm=TM, seq=L)
    att = _attention(qa.reshape(B, L, -1), ka.reshape(B, L, -1), vb.reshape(B, L, -1), tq=256)
    gn_p, st_p = _ssd_prompt(xbc_p.reshape(B, L, -1), slab_p.reshape(B, L, -1), z_p.reshape(B, L, -1),
                             cw, cb, a_row, dsk_x, gs)
    y_p = _ffn2(x1, att.reshape(B * L, D_ATT), gn_p.reshape(B * L, D_SSM), ga, wm, g2, w2i, w2o, gf,
                tm=TM, tf=TF)

    xs = x_sample.reshape(nb, D_MODEL)
    x1s = _ffn(xs, g1, w1i, w1o, tm=nb, tf=TF)
    q_s, k_s, v_s, z_s, xbc_s, slab_s = _inproj_plain(x1s, gm, ws, bias_s)
    n_pool = cache_k.shape[1]
    ck = jnp.transpose(cache_k[l0], (0, 2, 3, 1)).reshape(n_pool, D_ATT, PAGE_SIZE)
    cv = jnp.transpose(cache_v[l0], (0, 2, 3, 1)).reshape(n_pool, D_ATT, PAGE_SIZE)
    clf = jnp.transpose(cache_logf[l0], (0, 2, 1))
    att_s = _decode_attention(q_s, k_s, v_s, slab_s, ck, cv, clf, page_table, pps=16)
    sconv_t = jnp.swapaxes(state_conv[l0], 0, 1)
    gn_s, st_s = _ssd_step(xbc_s, sconv_t, slab_s, z_s,
                           state_ssm[l0].reshape(nb, D_SSM, D_STATE), cw, cb, a_row, dsk_x, gs)
    y_s = _ffn2(x1s, att_s, gn_s, ga, wm, g2, w2i, w2o, gf, tm=nb, tf=TF)

    xbc_p3 = xbc_p.reshape(B, L, CONV_DIM)
    return (
        y_p.reshape(B, L, D_MODEL),
        y_s.reshape(nb, 1, D_MODEL),
        k_p.reshape(1, B, L, H, HD),
        v_p.reshape(1, B, L, H, HD),
        slab_p[:, :H].reshape(1, B, L, H),
        xbc_p3[:, L - (CONV_W - 1):, :][None],
        st_p.reshape(1, B, N_SSM_HEADS, SSM_HEAD_DIM, D_STATE),
        k_s.reshape(1, nb, 1, H, HD),
        v_s.reshape(1, nb, 1, H, HD),
        slab_s[:, :H].reshape(1, nb, 1, H),
        jnp.concatenate([state_conv[l0][:, 1:, :], xbc_s[:, None, :]], axis=1)[None],
        st_s.reshape(1, nb, N_SSM_HEADS, SSM_HEAD_DIM, D_STATE),
    )
```

```python
import functools

import numpy as np
import jax
import jax.numpy as jnp
from jax import lax
from jax.experimental import pallas as pl
from jax.experimental.pallas import tpu as pltpu

F32 = jnp.float32
BF16 = jnp.bfloat16

D_MODEL = 1024
D_ATT = 512
D_SSM = 512
HEAD_DIM = 64
N_ATT_HEADS = 8
N_SSM_HEADS = 8
SSM_HEAD_DIM = 64
N_SSM_GROUPS = 2
D_STATE = 128
CONV_W = 4
CONV_DIM = D_SSM + 2 * N_SSM_GROUPS * D_STATE
SSD_CHUNK = 128
PAGE_SIZE = 128
D_FF = 2816
FFN_RESIDUAL = 0.5
EPS = 1e-6
ATT_SCALE = HEAD_DIM ** -0.5
LANES = 128
DT_LANE0 = 8
AUG0 = HEAD_DIM
NEG_BIG = -1e30

VMEM_LIMIT = 56 * 1024 * 1024


def _cparams(sem):
    return pltpu.CompilerParams(dimension_semantics=sem, vmem_limit_bytes=VMEM_LIMIT)


def _dot(a, b):
    return jnp.dot(a, b, preferred_element_type=F32)


def _dot_nt(a, b):
    return lax.dot_general(a, b, (((1,), (1,)), ((), ())), preferred_element_type=F32)


def _split3(x):
    hi = x.astype(BF16)
    r1 = x - hi.astype(F32)
    mid = r1.astype(BF16)
    lo = (r1 - mid.astype(F32)).astype(BF16)
    return hi, mid, lo


def _dot3(x, m):
    hi, mid, lo = _split3(x)
    return _dot(hi, m) + _dot(mid, m) + _dot(lo, m)


def _dot3r(m, x):
    hi, mid, lo = _split3(x)
    return _dot(m, hi) + _dot(m, mid) + _dot(m, lo)


def _rms(x, g):
    return x * lax.rsqrt(jnp.mean(x * x, axis=-1, keepdims=True) + EPS) * g


def _silu(x):
    return x / (1.0 + jnp.exp(-x))


def _tril(n, dtype=BF16):
    r = lax.broadcasted_iota(jnp.int32, (n, n), 0)
    c = lax.broadcasted_iota(jnp.int32, (n, n), 1)
    return (c <= r).astype(dtype)


def _row_to_col(row):
    n = row.shape[1]
    r = lax.broadcasted_iota(jnp.int32, (n, n), 0)
    c = lax.broadcasted_iota(jnp.int32, (n, n), 1)
    return jnp.sum(jnp.where(r == c, jnp.broadcast_to(row, (n, n)), 0.0), axis=1, keepdims=True)


def _col_to_row(col):
    n = col.shape[0]
    r = lax.broadcasted_iota(jnp.int32, (n, n), 0)
    c = lax.broadcasted_iota(jnp.int32, (n, n), 1)
    return jnp.sum(jnp.where(r == c, jnp.broadcast_to(col, (n, n)), 0.0), axis=0, keepdims=True)


def _ffn_body(x_ref, g_ref, wg_ref, wu_ref, wo_ref, o_ref, h_scr, acc_scr, *, nk):
    k = pl.program_id(1)

    @pl.when(k == 0)
    def _():
        h_scr[...] = _rms(x_ref[...], g_ref[...]).astype(BF16)
        acc_scr[...] = jnp.zeros_like(acc_scr)

    h = h_scr[...]
    gate = _dot(h, wg_ref[...])
    up = _dot(h, wu_ref[...])
    a = (_silu(gate) * up).astype(BF16)
    acc_scr[...] += _dot(a, wo_ref[...])

    @pl.when(k == nk - 1)
    def _():
        o_ref[...] = x_ref[...] + FFN_RESIDUAL * acc_scr[...]


def _ffn(x, norm_g, w_in_b, w_out_b, *, tm, tf):
    n = x.shape[0]
    nk = D_FF // tf
    return pl.pallas_call(
        functools.partial(_ffn_body, nk=nk),
        grid=(n // tm, nk),
        in_specs=[
            pl.BlockSpec((tm, D_MODEL), lambda i, k: (i, 0)),
            pl.BlockSpec((1, D_MODEL), lambda i, k: (0, 0)),
            pl.BlockSpec((D_MODEL, tf), lambda i, k: (0, k)),
            pl.BlockSpec((D_MODEL, tf), lambda i, k: (0, nk + k)),
            pl.BlockSpec((tf, D_MODEL), lambda i, k: (k, 0)),
        ],
        out_specs=pl.BlockSpec((tm, D_MODEL), lambda i, k: (i, 0)),
        out_shape=jax.ShapeDtypeStruct((n, D_MODEL), F32),
        scratch_shapes=[pltpu.VMEM((tm, D_MODEL), BF16), pltpu.VMEM((tm, D_MODEL), F32)],
        compiler_params=_cparams(("arbitrary", "arbitrary")),
        name="ffn1",
    )(x, norm_g, w_in_b, w_in_b, w_out_b)


def _ffn2_body(x_ref, att_ref, gn_ref, ag_ref, wm_ref, g_ref, wg_ref, wu_ref, wo_ref, fg_ref,
               o_ref, h_scr, acc_scr, x2_scr, *, nk):
    k = pl.program_id(1)

    @pl.when(k == 0)
    def _():
        an = _rms(att_ref[...], ag_ref[...]).astype(BF16)
        x2 = (x_ref[...] + _dot(an, wm_ref[:D_ATT, :]) + _dot(gn_ref[...], wm_ref[D_ATT:, :]))
        x2_scr[...] = x2
        h_scr[...] = _rms(x2, g_ref[...]).astype(BF16)
        acc_scr[...] = jnp.zeros_like(acc_scr)

    h = h_scr[...]
    gate = _dot(h, wg_ref[...])
    up = _dot(h, wu_ref[...])
    a = (_silu(gate) * up).astype(BF16)
    acc_scr[...] += _dot(a, wo_ref[...])

    @pl.when(k == nk - 1)
    def _():
        y = x2_scr[...] + FFN_RESIDUAL * acc_scr[...]
        o_ref[...] = _rms(y, fg_ref[...])


def _ffn2(x1, att, gn, att_g, w_mix_b, norm_g, w_in_b, w_out_b, final_g, *, tm, tf):
    n = x1.shape[0]
    nk = D_FF // tf
    return pl.pallas_call(
        functools.partial(_ffn2_body, nk=nk),
        grid=(n // tm, nk),
        in_specs=[
            pl.BlockSpec((tm, D_MODEL), lambda i, k: (i, 0)),
            pl.BlockSpec((tm, D_ATT), lambda i, k: (i, 0)),
            pl.BlockSpec((tm, D_SSM), lambda i, k: (i, 0)),
            pl.BlockSpec((1, D_ATT), lambda i, k: (0, 0)),
            pl.BlockSpec((D_MODEL, D_MODEL), lambda i, k: (0, 0)),
            pl.BlockSpec((1, D_MODEL), lambda i, k: (0, 0)),
            pl.BlockSpec((D_MODEL, tf), lambda i, k: (0, k)),
            pl.BlockSpec((D_MODEL, tf), lambda i, k: (0, nk + k)),
            pl.BlockSpec((tf, D_MODEL), lambda i, k: (k, 0)),
            pl.BlockSpec((1, D_MODEL), lambda i, k: (0, 0)),
        ],
        out_specs=pl.BlockSpec((tm, D_MODEL), lambda i, k: (i, 0)),
        out_shape=jax.ShapeDtypeStruct((n, D_MODEL), F32),
        scratch_shapes=[pltpu.VMEM((tm, D_MODEL), BF16), pltpu.VMEM((tm, D_MODEL), F32),
                        pltpu.VMEM((tm, D_MODEL), F32)],
        compiler_params=_cparams(("arbitrary", "arbitrary")),
        name="ffn2",
    )(x1, att, gn, att_g, w_mix_b, norm_g, w_in_b, w_in_b, w_out_b, final_g)


def _small_slab(raw, bias):
    v = raw + bias
    t = jnp.log1p(jnp.exp(-jnp.abs(v)))
    lane = lax.broadcasted_iota(jnp.int32, v.shape, 1)
    logf = -(jnp.maximum(-v, 0.0) + t)
    dt = jnp.maximum(v, 0.0) + t
    return jnp.where(lane < DT_LANE0, logf, jnp.where(lane < 2 * DT_LANE0, dt, 0.0))


def _inproj_aug_body(x_ref, g_ref, wq_ref, wk_ref, wv_ref, wz_ref, wx_ref, ws_ref, bs_ref,
                     eq_ref, ek_ref, ev_ref,
                     k_ref, v_ref, qa_ref, ka_ref, va_ref, z_ref, xbc_ref, slab_ref,
                     carry_scr, *, tiles_per_seq, tm):
    i = pl.program_id(0)

    @pl.when(i % tiles_per_seq == 0)
    def _():
        carry_scr[...] = jnp.zeros_like(carry_scr)

    h = _rms(x_ref[...], g_ref[...]).astype(BF16)
    q = _dot(h, wq_ref[...])
    k = _dot(h, wk_ref[...])
    v = _dot(h, wv_ref[...])
    k_ref[0] = k.T
    v_ref[0] = v.T
    z_ref[...] = _dot(h, wz_ref[...])
    xbc_ref[...] = _dot(h, wx_ref[...])
    slab = _small_slab(_dot(h, ws_ref[...]), bs_ref[...])
    slab_ref[...] = slab

    lane = lax.broadcasted_iota(jnp.int32, slab.shape, 1)
    logf = jnp.where(lane < DT_LANE0, slab, 0.0)
    c = _dot3r(_tril(tm), logf) + carry_scr[0:1, :]
    carry_scr[0:1, :] = c[tm - 1:tm, :]
    c_hi = c.astype(BF16).astype(F32)
    r1 = c - c_hi
    c_mid = r1.astype(BF16).astype(F32)
    c_lo = (r1 - c_mid).astype(BF16).astype(F32)
    aug = (c_hi + pltpu.roll(c_mid, DT_LANE0, 1) + pltpu.roll(c_lo, 2 * DT_LANE0, 1)
           + jnp.where(lane == 3 * DT_LANE0, 1.0, 0.0)).astype(BF16)
    qs = (q * ATT_SCALE).astype(BF16)
    qa_ref[...] = (_dot(qs, eq_ref[:D_ATT, :]) + _dot(aug, eq_ref[D_ATT:, :])).astype(BF16)
    ka_ref[...] = (_dot(k.astype(BF16), ek_ref[:D_ATT, :]) + _dot(aug, ek_ref[D_ATT:, :])).astype(BF16)
    va_ref[...] = (_dot(v.astype(BF16), ev_ref[:D_ATT, :]) + _dot(aug, ev_ref[D_ATT:, :])).astype(BF16)


def _aug_scatter_mats():
    eq = np.zeros((D_ATT + LANES, N_ATT_HEADS * LANES), np.float32)
    ek = np.zeros((D_ATT + LANES, N_ATT_HEADS * LANES), np.float32)
    ev = np.zeros((D_ATT + LANES, N_ATT_HEADS * LANES), np.float32)
    one_lane = D_ATT + 3 * DT_LANE0
    for h in range(N_ATT_HEADS):
        v0, vsum = (0, HEAD_DIM) if h % 2 == 0 else (HEAD_DIM, 0)
        for d in range(HEAD_DIM):
            eq[h * HEAD_DIM + d, h * LANES + d] = 1.0
            ek[h * HEAD_DIM + d, h * LANES + d] = 1.0
            ev[h * HEAD_DIM + d, h * LANES + v0 + d] = 1.0
        ev[one_lane, h * LANES + vsum] = 1.0
        for p in range(3):
            eq[D_ATT + p * DT_LANE0 + h, h * LANES + AUG0 + p] = 1.0
            eq[one_lane, h * LANES + AUG0 + 3 + p] = 1.0
            ek[one_lane, h * LANES + AUG0 + p] = 1.0
            ek[D_ATT + p * DT_LANE0 + h, h * LANES + AUG0 + 3 + p] = -1.0
    return jnp.asarray(eq, BF16), jnp.asarray(ek, BF16), jnp.asarray(ev, BF16)


def _inproj_aug(x1, norm_g, ws, bias_s, *, tm, seq):
    n = x1.shape[0]
    eq, ek, ev = _aug_scatter_mats()
    row = lambda w: pl.BlockSpec((tm, w), lambda i: (i, 0))
    full = lambda a: pl.BlockSpec(a.shape, lambda i: (0, 0))
    wq, wk, wv, wz, wx, wsm = ws
    aug_w = N_ATT_HEADS * LANES
    tps = seq // tm
    kv_t = pl.BlockSpec((1, D_ATT, tm), lambda i: (i // tps, 0, i % tps))
    return pl.pallas_call(
        functools.partial(_inproj_aug_body, tiles_per_seq=tps, tm=tm),
        grid=(n // tm,),
        in_specs=[row(D_MODEL), full(norm_g), full(wq), full(wk), full(wv), full(wz), full(wx),
                  full(wsm), full(bias_s), full(eq), full(ek), full(ev)],
        out_specs=[kv_t, kv_t, row(aug_w), row(aug_w), row(aug_w),
                   row(D_SSM), row(CONV_DIM), row(LANES)],
        out_shape=[
            jax.ShapeDtypeStruct((n // seq, D_ATT, seq), F32), jax.ShapeDtypeStruct((n // seq, D_ATT, seq), F32),
            jax.ShapeDtypeStruct((n, aug_w), BF16), jax.ShapeDtypeStruct((n, aug_w), BF16),
            jax.ShapeDtypeStruct((n, aug_w), BF16), jax.ShapeDtypeStruct((n, D_SSM), F32),
            jax.ShapeDtypeStruct((n, CONV_DIM), F32), jax.ShapeDtypeStruct((n, LANES), F32)],
        scratch_shapes=[pltpu.VMEM((8, LANES), F32)],
        compiler_params=_cparams(("arbitrary",)),
        name="inproj_prompt",
    )(x1, norm_g, wq, wk, wv, wz, wx, wsm, bias_s, eq, ek, ev)


def _inproj_plain_body(x_ref, g_ref, wq_ref, wk_ref, wv_ref, wz_ref, wx_ref, ws_ref, bs_ref,
                       q_ref, k_ref, v_ref, z_ref, xbc_ref, slab_ref):
    h = _rms(x_ref[...], g_ref[...]).astype(BF16)
    q_ref[...] = _dot(h, wq_ref[...])
    k_ref[...] = _dot(h, wk_ref[...])
    v_ref[...] = _dot(h, wv_ref[...])
    z_ref[...] = _dot(h, wz_ref[...])
    xbc_ref[...] = _dot(h, wx_ref[...])
    slab_ref[...] = _small_slab(_dot(h, ws_ref[...]), bs_ref[...])


def _inproj_plain(x1, norm_g, ws, bias_s):
    n = x1.shape[0]
    wq, wk, wv, wz, wx, wsm = ws
    full = lambda a: pl.BlockSpec(a.shape, lambda i: (0, 0))
    out = lambda w: pl.BlockSpec((n, w), lambda i: (0, 0))
    return pl.pallas_call(
        _inproj_plain_body,
        grid=(1,),
        in_specs=[full(x1), full(norm_g), full(wq), full(wk), full(wv), full(wz), full(wx),
                  full(wsm), full(bias_s)],
        out_specs=[out(D_ATT), out(D_ATT), out(D_ATT), out(D_SSM), out(CONV_DIM), out(LANES)],
        out_shape=[jax.ShapeDtypeStruct((n, w), F32)
                   for w in (D_ATT, D_ATT, D_ATT, D_SSM, CONV_DIM, LANES)],
        compiler_params=_cparams(("arbitrary",)),
        name="inproj_sample",
    )(x1, norm_g, wq, wk, wv, wz, wx, wsm, bias_s)


def _attn_body(q_ref, k_ref, v_ref, o_ref, s_scr, m_scr, acc_scr, *, tq, nt):
    t = pl.program_id(2)
    nl = tq // LANES
    r = lax.broadcasted_iota(jnp.int32, (tq, tq), 0)
    c = lax.broadcasted_iota(jnp.int32, (tq, tq), 1)
    causal = c <= r
    m_scr[...] = jnp.full(m_scr.shape, NEG_BIG, F32)
    acc_scr[...] = jnp.zeros_like(acc_scr)

    def chunk_of(ci):
        first = ci <= t
        sel = jnp.where(first, 0, 1)
        q0 = pl.multiple_of(jnp.where(first, t, nt - 1 - t) * tq, tq)
        k0 = pl.multiple_of(jnp.where(first, ci, ci - t - 1) * tq, tq)
        return sel, q0, k0

    for ci in range(nt + 1):
        sel, q0, k0 = chunk_of(ci)
        for j in range(2):
            lanes = slice(j * LANES, (j + 1) * LANES)
            s = _dot_nt(q_ref[0, pl.ds(q0, tq), lanes], k_ref[0, pl.ds(k0, tq), lanes])
            if ci == nt:
                s = jnp.where(causal, s, NEG_BIG)
            elif ci < nt // 2:
                s = jnp.where(jnp.logical_or(causal, ci != t), s, NEG_BIG)
            s_scr[j, ci] = s
            m = m_scr[sel, j]
            for u in range(nl):
                m = jnp.maximum(m, s[:, u * LANES:(u + 1) * LANES])
            m_scr[sel, j] = m

    for x in range(2):
        for j in range(2):
            m_scr[x, j] = jnp.broadcast_to(jnp.max(m_scr[x, j], axis=1, keepdims=True), (tq, LANES))

    for ci in range(nt + 1):
        sel, q0, k0 = chunk_of(ci)
        for j in range(2):
            lanes = slice(j * LANES, (j + 1) * LANES)
            s = s_scr[j, ci]
            mrep = m_scr[sel, j]
            p = jnp.concatenate([jnp.exp(s[:, u * LANES:(u + 1) * LANES] - mrep) for u in range(nl)], axis=1)
            acc_scr[sel, j] += _dot(p.astype(BF16), v_ref[0, pl.ds(k0, tq), lanes])

    lane = lax.broadcasted_iota(jnp.int32, (tq, LANES), 1)
    for x, tile in enumerate((t, nt - 1 - t)):
        acc_e, acc_o = acc_scr[x, 0], acc_scr[x, 1]
        out = jnp.where(lane < HEAD_DIM, acc_e / acc_e[:, HEAD_DIM:HEAD_DIM + 1], acc_o / acc_o[:, 0:1])
        o_ref[0, pl.ds(pl.multiple_of(tile * tq, tq), tq), :] = out


def _attention(qa, ka, va, *, tq):
    b, L, _ = qa.shape
    nt = L // tq
    assert nt % 2 == 0
    pair = pl.BlockSpec((1, L, 2 * LANES), lambda bi, hp, t: (bi, 0, hp))
    return pl.pallas_call(
        functools.partial(_attn_body, tq=tq, nt=nt),
        grid=(b, N_ATT_HEADS // 2, nt // 2),
        in_specs=[pair, pair, pair],
        out_specs=pl.BlockSpec((1, L, LANES), lambda bi, hp, t: (bi, 0, hp)),
        out_shape=jax.ShapeDtypeStruct((b, L, D_ATT), F32),
        scratch_shapes=[pltpu.VMEM((2, nt + 1, tq, tq), F32), pltpu.VMEM((2, 2, tq, LANES), F32),
                        pltpu.VMEM((2, 2, tq, LANES), F32)],
        compiler_params=_cparams(("arbitrary", "arbitrary", "arbitrary")),
        name="fox_prompt",
    )(qa, ka, va)


def _head_expand_mat():
    e = np.zeros((LANES, D_SSM), np.float32)
    for h in range(N_SSM_HEADS):
        e[DT_LANE0 + h, h * SSM_HEAD_DIM:(h + 1) * SSM_HEAD_DIM] = 1.0
    return jnp.asarray(e, BF16)


def _gate_groupnorm(y, z, gain):
    g = y * _silu(z)
    half = D_SSM // N_SSM_GROUPS
    parts = []
    for gi in range(N_SSM_GROUPS):
        gg = g[:, gi * half:(gi + 1) * half]
        parts.append(gg * lax.rsqrt(jnp.mean(gg * gg, axis=-1, keepdims=True) + EPS))
    return jnp.concatenate(parts, axis=1) * gain


def _ssd_body(xbc_ref, slab_ref, z_ref, cw_ref, cb_ref, arow_ref, dsk_ref, sg_ref, e8_ref,
              gn_ref, st_ref, conv_scr, ht_scr, *, n_chunks):
    c = pl.program_id(1)
    Q = SSD_CHUNK

    @pl.when(c == 0)
    def _():
        conv_scr[0:8, :] = jnp.zeros((8, CONV_DIM), F32)
        ht_scr[...] = jnp.zeros_like(ht_scr)

    u = xbc_ref[0]
    conv_scr[8:8 + Q, :] = u
    conv = (cb_ref[...] + cw_ref[0:1, :] * conv_scr[5:5 + Q, :] + cw_ref[1:2, :] * conv_scr[6:6 + Q, :]
            + cw_ref[2:3, :] * conv_scr[7:7 + Q, :] + cw_ref[3:4, :] * u)
    conv_scr[0:8, :] = u[Q - 8:Q, :]
    xc = _silu(conv)
    xs = xc[:, :D_SSM]
    e8 = e8_ref[...]

    slab = slab_ref[0]
    lane = lax.broadcasted_iota(jnp.int32, slab.shape, 1)
    dtm = jnp.where((lane >= DT_LANE0) & (lane < 2 * DT_LANE0), slab, 0.0)
    a = dtm * arow_ref[...]
    a_c = _dot3r(_tril(Q), a)
    a_ct = a_c.T
    a_last = a_c[Q - 1:Q, :]
    ea_x = _dot3(jnp.exp(a_c), e8)
    dec_x = _dot3(jnp.exp(a_last - a_c), e8)
    xdt = xs * _dot3(dtm, e8)
    xdt_b = xdt.astype(BF16)
    xd_b = (xdt * dec_x).astype(BF16)
    cd_x = ea_x[Q - 1:Q, :]

    r_i = lax.broadcasted_iota(jnp.int32, (Q, Q), 0)
    c_i = lax.broadcasted_iota(jnp.int32, (Q, Q), 1)
    tri = c_i <= r_i
    lane_q = lax.broadcasted_iota(jnp.int32, (Q, LANES), 1)
    ht_old = ht_scr[...]
    gw = D_SSM // N_SSM_GROUPS
    hpg = N_SSM_HEADS // N_SSM_GROUPS
    y_parts, ht_parts = [], []
    for g in range(N_SSM_GROUPS):
        bm = xc[:, D_SSM + g * D_STATE:D_SSM + (g + 1) * D_STATE]
        cm = xc[:, D_SSM + (N_SSM_GROUPS + g) * D_STATE:D_SSM + (N_SSM_GROUPS + g + 1) * D_STATE]
        cb16 = cm.astype(BF16)
        cbm = _dot_nt(cb16, bm.astype(BF16))
        y_off = _dot(cb16, ht_old[:, g * gw:(g + 1) * gw].astype(BF16))
        y_diag = []
        for pr in range(hpg // 2):
            halves = []
            for hh in range(2):
                h = g * hpg + pr * 2 + hh
                col = a_c[:, DT_LANE0 + h:DT_LANE0 + h + 1]
                row = a_ct[DT_LANE0 + h:DT_LANE0 + h + 1, :]
                lm = jnp.where(tri, jnp.exp(col - row), 0.0)
                sc = (cbm * lm).astype(BF16)
                lo = g * gw + pr * LANES
                halves.append(_dot(sc, xdt_b[:, lo:lo + LANES]))
            y_diag.append(jnp.where(lane_q < SSM_HEAD_DIM, halves[0], halves[1]))
        y_parts.append(jnp.concatenate(y_diag, axis=1) + y_off * ea_x[:, g * gw:(g + 1) * gw])
        new = _dot(bm.T.astype(BF16), xd_b[:, g * gw:(g + 1) * gw])
        ht_parts.append(ht_old[:, g * gw:(g + 1) * gw] * cd_x[:, g * gw:(g + 1) * gw] + new)
    ht_new = jnp.concatenate(ht_parts, axis=1)
    ht_scr[...] = ht_new
    y = jnp.concatenate(y_parts, axis=1) + dsk_ref[...] * xs
    gn_ref[0] = _gate_groupnorm(y, z_ref[0], sg_ref[...]).astype(BF16)

    @pl.when(c == n_chunks - 1)
    def _():
        st_ref[0] = ht_new.T


def _ssd_prompt(xbc, slab, z, conv_w, conv_b, a_row, dsk_x, ssm_g):
    b, L, _ = xbc.shape
    nc = L // SSD_CHUNK
    e8 = _head_expand_mat()
    blk = lambda w: pl.BlockSpec((1, SSD_CHUNK, w), lambda bi, ci: (bi, ci, 0))
    full = lambda a: pl.BlockSpec(a.shape, lambda bi, ci: (0, 0))
    return pl.pallas_call(
        functools.partial(_ssd_body, n_chunks=nc),
        grid=(b, nc),
        in_specs=[blk(CONV_DIM), blk(LANES), blk(D_SSM), full(conv_w), full(conv_b), full(a_row),
                  full(dsk_x), full(ssm_g), full(e8)],
        out_specs=[blk(D_SSM), pl.BlockSpec((1, D_SSM, D_STATE), lambda bi, ci: (bi, 0, 0))],
        out_shape=[jax.ShapeDtypeStruct((b, L, D_SSM), BF16),
                   jax.ShapeDtypeStruct((b, D_SSM, D_STATE), F32)],
        scratch_shapes=[pltpu.VMEM((8 + SSD_CHUNK, CONV_DIM), F32), pltpu.VMEM((D_STATE, D_SSM), F32)],
        compiler_params=_cparams(("arbitrary", "arbitrary")),
        name="ssd_prompt",
    )(xbc, slab, z, conv_w, conv_b, a_row, dsk_x, ssm_g, e8)


def _ssd_step_body(xbc_ref, sc_ref, slab_ref, z_ref, st_ref, cw_ref, cb_ref, arow_ref, dsk_ref,
                   sg_ref, e8_ref, gn_ref, so_ref, xc_scr, dtx_scr, decx_scr):
    b = pl.program_id(0)

    @pl.when(b == 0)
    def _():
        conv = (cb_ref[...] + cw_ref[0:1, :] * sc_ref[0] + cw_ref[1:2, :] * sc_ref[1]
                + cw_ref[2:3, :] * sc_ref[2] + cw_ref[3:4, :] * xbc_ref[...])
        xc_scr[...] = _silu(conv)
        slab = slab_ref[...]
        lane = lax.broadcasted_iota(jnp.int32, slab.shape, 1)
        dtm = jnp.where((lane >= DT_LANE0) & (lane < 2 * DT_LANE0), slab, 0.0)
        e8 = e8_ref[...]
        dtx_scr[...] = _dot3(dtm, e8)
        decx_scr[...] = _dot3(jnp.exp(dtm * arow_ref[...]), e8)

    xrow = xc_scr[pl.ds(b, 1), :]
    xs = xrow[:, :D_SSM]
    xdt = xs * dtx_scr[pl.ds(b, 1), :]
    dec = decx_scr[pl.ds(b, 1), :]
    gw = D_SSM // N_SSM_GROUPS
    y_rows = []
    for j in range(D_SSM // LANES):
        g = (j * LANES) // gw
        bm = xrow[:, D_SSM + g * D_STATE:D_SSM + (g + 1) * D_STATE]
        cm = xrow[:, D_SSM + (N_SSM_GROUPS + g) * D_STATE:D_SSM + (N_SSM_GROUPS + g + 1) * D_STATE]
        xcol = _row_to_col(xdt[:, j * LANES:(j + 1) * LANES])
        dcol = _row_to_col(dec[:, j * LANES:(j + 1) * LANES])
        hs = dcol * st_ref[0, j * LANES:(j + 1) * LANES, :] + xcol * bm
        so_ref[0, j * LANES:(j + 1) * LANES, :] = hs
        y_rows.append(_col_to_row(jnp.sum(hs * cm, axis=1, keepdims=True)))
    y = jnp.concatenate(y_rows, axis=1) + dsk_ref[...] * xs
    gn_ref[0] = _gate_groupnorm(y, z_ref[pl.ds(b, 1), :], sg_ref[...]).astype(BF16)


def _ssd_step(xbc, sconv_t, slab, z, state, conv_w, conv_b, a_row, dsk_x, ssm_g):
    nb = xbc.shape[0]
    e8 = _head_expand_mat()
    full2 = lambda a: pl.BlockSpec(a.shape, lambda bi: (0,) * a.ndim)
    gn, st = pl.pallas_call(
        _ssd_step_body,
        grid=(nb,),
        in_specs=[full2(xbc), full2(sconv_t), full2(slab), full2(z),
                  pl.BlockSpec((1, D_SSM, D_STATE), lambda bi: (bi, 0, 0)),
                  full2(conv_w), full2(conv_b), full2(a_row), full2(dsk_x), full2(ssm_g), full2(e8)],
        out_specs=[pl.BlockSpec((1, 1, D_SSM), lambda bi: (bi, 0, 0)),
                   pl.BlockSpec((1, D_SSM, D_STATE), lambda bi: (bi, 0, 0))],
        out_shape=[jax.ShapeDtypeStruct((nb, 1, D_SSM), BF16),
                   jax.ShapeDtypeStruct((nb, D_SSM, D_STATE), F32)],
        scratch_shapes=[pltpu.VMEM((nb, CONV_DIM), F32), pltpu.VMEM((nb, D_SSM), F32),
                        pltpu.VMEM((nb, D_SSM), F32)],
        compiler_params=_cparams(("arbitrary",)),
        name="ssd_step",
    )(xbc, sconv_t, slab, z, state, conv_w, conv_b, a_row, dsk_x, ssm_g, e8)
    return gn.reshape(nb, D_SSM), st


def _decode_body(pt_ref, q_ref, k2_ref, v2_ref, lf2_ref, *refs, pps, n_groups):
    k_refs = refs[:pps]
    v_refs = refs[pps:2 * pps]
    lf_refs = refs[2 * pps:3 * pps]
    o_ref = refs[3 * pps]
    qblk_scr, m_scr, l_scr, acc_scr, carry_scr = refs[3 * pps + 1:]
    g = pl.program_id(1)
    H, HD, P = N_ATT_HEADS, HEAD_DIM, PAGE_SIZE
    r8 = lax.broadcasted_iota(jnp.int32, (H, D_ATT), 0)
    c8 = lax.broadcasted_iota(jnp.int32, (H, D_ATT), 1)
    own_head = c8 // HD == r8

    @pl.when(g == 0)
    def _():
        qs = (q_ref[0] * ATT_SCALE).astype(BF16).astype(F32)
        qblk = jnp.where(own_head, qs, 0.0)
        qblk_scr[...] = qblk.astype(BF16)
        k2 = k2_ref[0].astype(BF16).astype(F32)
        m_scr[...] = jnp.broadcast_to(jnp.sum(qblk * k2, axis=1, keepdims=True), m_scr.shape)
        l_scr[...] = jnp.ones_like(l_scr)
        acc_scr[...] = jnp.broadcast_to(v2_ref[0].astype(BF16).astype(F32), acc_scr.shape)
        carry_scr[...] = jnp.broadcast_to(_row_to_col(lf2_ref[0])[0:H, :], carry_scr.shape)

    kcat = jnp.concatenate([kr[0].astype(BF16) for kr in k_refs], axis=1)
    s = _dot(qblk_scr[...], kcat)
    x = jnp.concatenate([lr[0] for lr in lf_refs], axis=0)
    rr = lax.broadcasted_iota(jnp.int32, (P, P), 0)
    cc = lax.broadcasted_iota(jnp.int32, (P, P), 1)
    later = (rr > cc).astype(BF16)
    rev_local = _dot3(x, later)
    tot = jnp.sum(x, axis=1, keepdims=True)
    carry = carry_scr[:, 0:1]
    s_pages = [None] * pps
    for i in reversed(range(pps)):
        s_pages[i] = s[:, i * P:(i + 1) * P] + rev_local[i * H:(i + 1) * H, :] + carry
        carry = carry + tot[i * H:(i + 1) * H, :]
    carry_scr[...] = jnp.broadcast_to(carry, carry_scr.shape)
    st = jnp.concatenate(s_pages, axis=1)

    m_old = m_scr[:, 0:1]
    m_new = jnp.maximum(m_old, jnp.max(st, axis=1, keepdims=True))
    alpha = jnp.exp(m_old - m_new)
    p = jnp.exp(st - m_new)
    l_new = alpha * l_scr[:, 0:1] + jnp.sum(p, axis=1, keepdims=True)
    vcat = jnp.concatenate([vr[0].astype(BF16) for vr in v_refs], axis=1)
    acc = alpha * acc_scr[...] + _dot_nt(p.astype(BF16), vcat)
    m_scr[...] = jnp.broadcast_to(m_new, m_scr.shape)
    l_scr[...] = jnp.broadcast_to(l_new, l_scr.shape)
    acc_scr[...] = acc

    @pl.when(g == n_groups - 1)
    def _():
        o_ref[0] = jnp.sum(jnp.where(own_head, acc / l_new, 0.0), axis=0, keepdims=True)


def _decode_attention(q, k2, v2, slab2, cache_k, cache_v, cache_logf, page_table, *, pps):
    nb, n_pages = page_table.shape
    ng = n_pages // pps
    row3 = lambda a: a.reshape(nb, 1, a.shape[-1])
    rowspec = lambda w: pl.BlockSpec((1, 1, w), lambda b, g, pt: (b, 0, 0))

    def page_spec(rows, i):
        return pl.BlockSpec((1, rows, PAGE_SIZE),
                            lambda b, g, pt, i=i: (pt[b, (ng - 1 - g) * pps + i], 0, 0))

    in_specs = ([rowspec(D_ATT), rowspec(D_ATT), rowspec(D_ATT), rowspec(LANES)]
                + [page_spec(D_ATT, i) for i in range(pps)]
                + [page_spec(D_ATT, i) for i in range(pps)]
                + [page_spec(N_ATT_HEADS, i) for i in range(pps)])
    out = pl.pallas_call(
        functools.partial(_decode_body, pps=pps, n_groups=ng),
        grid_spec=pltpu.PrefetchScalarGridSpec(
            num_scalar_prefetch=1,
            grid=(nb, ng),
            in_specs=in_specs,
            out_specs=pl.BlockSpec((1, 1, D_ATT), lambda b, g, pt: (b, 0, 0)),
            scratch_shapes=[pltpu.VMEM((N_ATT_HEADS, D_ATT), BF16), pltpu.VMEM((8, LANES), F32),
                            pltpu.VMEM((8, LANES), F32), pltpu.VMEM((8, D_ATT), F32),
                            pltpu.VMEM((8, LANES), F32)]),
        out_shape=jax.ShapeDtypeStruct((nb, 1, D_ATT), F32),
        compiler_params=_cparams(("arbitrary", "arbitrary")),
        name="fox_decode",
    )(page_table, row3(q), row3(k2), row3(v2), row3(slab2),
      *([cache_k] * pps), *([cache_v] * pps), *([cache_logf] * pps))
    return out.reshape(nb, D_ATT)


def _lane_pad(vec, lane0):
    out = jnp.zeros((1, LANES), F32)
    return out.at[0, lane0:lane0 + vec.shape[0]].set(vec.astype(F32))


def kernel(x_prompt, x_sample, cache_k, cache_v, cache_logf, page_table, state_conv, state_ssm,
           ffn1_norm, w_ffn1_in, w_ffn1_out, mix_norm, w_in, b_f, conv_w, conv_b, dt_bias, a_log,
           d_skip, att_out_norm, ssm_out_norm, w_out, ffn2_norm, w_ffn2_in, w_ffn2_out, final_norm):
    depth = w_in.shape[0]
    assert depth == 1
    B, L, _ = x_prompt.shape
    nb, T, _ = x_sample.shape
    assert T == 1
    H, HD = N_ATT_HEADS, HEAD_DIM

    l0 = 0
    w1i, w1o = w_ffn1_in[l0].astype(BF16), w_ffn1_out[l0].astype(BF16)
    w2i, w2o = w_ffn2_in[l0].astype(BF16), w_ffn2_out[l0].astype(BF16)
    wi = w_in[l0]
    o_f = 3 * D_ATT
    o_z = o_f + H
    o_x = o_z + D_SSM
    o_dt = o_x + CONV_DIM
    w_small = jnp.zeros((D_MODEL, LANES), F32)
    w_small = w_small.at[:, 0:H].set(wi[:, o_f:o_f + H])
    w_small = w_small.at[:, DT_LANE0:DT_LANE0 + N_SSM_HEADS].set(wi[:, o_dt:o_dt + N_SSM_HEADS])
    ws = tuple(w.astype(BF16) for w in (wi[:, 0:D_ATT], wi[:, D_ATT:2 * D_ATT], wi[:, 2 * D_ATT:3 * D_ATT],
                                        wi[:, o_z:o_z + D_SSM], wi[:, o_x:o_x + CONV_DIM], w_small))
    bias_s = _lane_pad(b_f[l0], 0) + _lane_pad(dt_bias[l0], DT_LANE0)
    a_row = _lane_pad(-jnp.exp(a_log[l0].astype(F32)), DT_LANE0)
    dsk_x = jnp.repeat(d_skip[l0].astype(F32), SSM_HEAD_DIM)[None, :]
    row = lambda v: v.astype(F32)[None, :]
    g1, gm, g2, gf = row(ffn1_norm[l0]), row(mix_norm[l0]), row(ffn2_norm[l0]), row(final_norm)
    ga, gs = row(att_out_norm[l0]), row(ssm_out_norm[l0])
    cw, cb = conv_w[l0].astype(F32), row(conv_b[l0])
    wm = w_out[l0].astype(BF16)

    TM, TF = 512, 1408
    xp = x_prompt.reshape(B * L, D_MODEL)
    x1 = _ffn(xp, g1, w1i, w1o, tm=TM, tf=TF)
    k_p, v_p, qa, ka, va, z_p, xbc_p, slab_p = _inproj_aug(x1, gm, ws, bias_s, tm=TM, seq=L)
    att = _attention(qa.reshape(B, L, -1), ka.reshape(B, L, -1), va.reshape(B, L, -1), tq=256)
    gn_p, st_p = _ssd_prompt(xbc_p.reshape(B, L, -1), slab_p.reshape(B, L, -1), z_p.reshape(B, L, -1),
                             cw, cb, a_row, dsk_x, gs)
    y_p = _ffn2(x1, att.reshape(B * L, D_ATT), gn_p.reshape(B * L, D_SSM), ga, wm, g2, w2i, w2o, gf,
                tm=TM, tf=TF)

    xs = x_sample.reshape(nb, D_MODEL)
    x1s = _ffn(xs, g1, w1i, w1o, tm=nb, tf=TF)
    q_s, k_s, v_s, z_s, xbc_s, slab_s = _inproj_plain(x1s, gm, ws, bias_s)
    n_pool = cache_k.shape[1]
    ck = jnp.transpose(cache_k[l0], (0, 2, 3, 1)).reshape(n_pool, D_ATT, PAGE_SIZE)
    cv = jnp.transpose(cache_v[l0], (0, 2, 3, 1)).reshape(n_pool, D_ATT, PAGE_SIZE)
    clf = jnp.transpose(cache_logf[l0], (0, 2, 1))
    att_s = _decode_attention(q_s, k_s, v_s, slab_s, ck, cv, clf, page_table, pps=16)
    sconv_t = jnp.swapaxes(state_conv[l0], 0, 1)
    gn_s, st_s = _ssd_step(xbc_s, sconv_t, slab_s, z_s,
                           state_ssm[l0].reshape(nb, D_SSM, D_STATE), cw, cb, a_row, dsk_x, gs)
    y_s = _ffn2(x1s, att_s, gn_s, ga, wm, g2, w2i, w2o, gf, tm=nb, tf=TF)

    xbc_p3 = xbc_p.reshape(B, L, CONV_DIM)
    return (
        y_p.reshape(B, L, D_MODEL),
        y_s.reshape(nb, 1, D_MODEL),
        jnp.transpose(k_p.reshape(1, B, H, HD, L), (0, 1, 4, 2, 3)),
        jnp.transpose(v_p.reshape(1, B, H, HD, L), (0, 1, 4, 2, 3)),
        slab_p[:, :H].reshape(1, B, L, H),
        xbc_p3[:, L - (CONV_W - 1):, :][None],
        st_p.reshape(1, B, N_SSM_HEADS, SSM_HEAD_DIM, D_STATE),
        k_s.reshape(1, nb, 1, H, HD),
        v_s.reshape(1, nb, 1, H, HD),
        slab_s[:, :H].reshape(1, nb, 1, H),
        jnp.concatenate([state_conv[l0][:, 1:, :], xbc_s[:, None, :]], axis=1)[None],
        st_s.reshape(1, nb, N_SSM_HEADS, SSM_HEAD_DIM, D_STATE),
    )
```

```python
import functools

import numpy as np
import jax
import jax.numpy as jnp
from jax import lax
from jax.experimental import pallas as pl
from jax.experimental.pallas import tpu as pltpu

F32 = jnp.float32
BF16 = jnp.bfloat16

D_MODEL = 1024
D_ATT = 512
D_SSM = 512
HEAD_DIM = 64
N_ATT_HEADS = 8
N_SSM_HEADS = 8
SSM_HEAD_DIM = 64
N_SSM_GROUPS = 2
D_STATE = 128
CONV_W = 4
CONV_DIM = D_SSM + 2 * N_SSM_GROUPS * D_STATE
SSD_CHUNK = 128
PAGE_SIZE = 128
D_FF = 2816
FFN_RESIDUAL = 0.5
EPS = 1e-6
ATT_SCALE = HEAD_DIM ** -0.5
LANES = 128
DT_LANE0 = 8
AUG0 = HEAD_DIM
NEG_BIG = -1e30

VMEM_LIMIT = 56 * 1024 * 1024
FFN_ROWS = 1024
FFN_CHUNK = 256
INPROJ_ROWS = 512
ATT_ROWS = 256


def _cparams(sem):
    return pltpu.CompilerParams(dimension_semantics=sem, vmem_limit_bytes=VMEM_LIMIT)


def _dot(a, b):
    return jnp.dot(a, b, preferred_element_type=F32)


def _dot_nt(a, b):
    return lax.dot_general(a, b, (((1,), (1,)), ((), ())), preferred_element_type=F32)


def _split3(x):
    hi = x.astype(BF16)
    r1 = x - hi.astype(F32)
    mid = r1.astype(BF16)
    lo = (r1 - mid.astype(F32)).astype(BF16)
    return hi, mid, lo


def _dot3(x, m):
    hi, mid, lo = _split3(x)
    return _dot(hi, m) + _dot(mid, m) + _dot(lo, m)


def _dot3r(m, x):
    hi, mid, lo = _split3(x)
    return _dot(m, hi) + _dot(m, mid) + _dot(m, lo)


def _rms(x, g):
    return x * lax.rsqrt(jnp.mean(x * x, axis=-1, keepdims=True) + EPS) * g


def _silu(x):
    return x / (1.0 + jnp.exp(-x))


def _tril(n, dtype=BF16):
    r = lax.broadcasted_iota(jnp.int32, (n, n), 0)
    c = lax.broadcasted_iota(jnp.int32, (n, n), 1)
    return (c <= r).astype(dtype)


def _row_to_col(row):
    n = row.shape[1]
    r = lax.broadcasted_iota(jnp.int32, (n, n), 0)
    c = lax.broadcasted_iota(jnp.int32, (n, n), 1)
    return jnp.sum(jnp.where(r == c, jnp.broadcast_to(row, (n, n)), 0.0), axis=1, keepdims=True)


def _col_to_row(col):
    n = col.shape[0]
    r = lax.broadcasted_iota(jnp.int32, (n, n), 0)
    c = lax.broadcasted_iota(jnp.int32, (n, n), 1)
    return jnp.sum(jnp.where(r == c, jnp.broadcast_to(col, (n, n)), 0.0), axis=0, keepdims=True)


def _resident(shape):
    return pl.BlockSpec(shape, lambda i: (0,) * len(shape), pipeline_mode=pl.Buffered(1))


def _swiglu_half(x, g_ref, wi_ref, wo_ref, a_scr, tf):
    h = _rms(x, g_ref[...]).astype(BF16)
    for c in range(D_FF // tf):
        gate = _dot(h, wi_ref[:, c * tf:(c + 1) * tf])
        up = _dot(h, wi_ref[:, D_FF + c * tf:D_FF + (c + 1) * tf])
        a_scr[:, c * tf:(c + 1) * tf] = (_silu(gate) * up).astype(BF16)
    return x + FFN_RESIDUAL * _dot(a_scr[...], wo_ref[...])


def _ffn_body(x_ref, g_ref, wi_ref, wo_ref, o_ref, a_scr, *, tf):
    o_ref[...] = _swiglu_half(x_ref[...], g_ref, wi_ref, wo_ref, a_scr, tf)


def _ffn(x, norm_g, w_in_b, w_out_b, *, tm, tf):
    n = x.shape[0]
    row = lambda w: pl.BlockSpec((tm, w), lambda i: (i, 0))
    return pl.pallas_call(
        functools.partial(_ffn_body, tf=tf),
        grid=(n // tm,),
        in_specs=[row(D_MODEL), _resident(norm_g.shape), _resident(w_in_b.shape), _resident(w_out_b.shape)],
        out_specs=row(D_MODEL),
        out_shape=jax.ShapeDtypeStruct((n, D_MODEL), F32),
        scratch_shapes=[pltpu.VMEM((tm, D_FF), BF16)],
        compiler_params=_cparams(("arbitrary",)),
        name="ffn1",
    )(x, norm_g, w_in_b, w_out_b)


def _ffn2_body(x_ref, att_ref, gn_ref, ag_ref, wm_ref, g_ref, wi_ref, wo_ref, fg_ref, o_ref, a_scr, *, tf):
    an = _rms(att_ref[...], ag_ref[...]).astype(BF16)
    x2 = x_ref[...] + _dot(an, wm_ref[:D_ATT, :]) + _dot(gn_ref[...], wm_ref[D_ATT:, :])
    o_ref[...] = _rms(_swiglu_half(x2, g_ref, wi_ref, wo_ref, a_scr, tf), fg_ref[...])


def _ffn2(x1, att, gn, att_g, w_mix_b, norm_g, w_in_b, w_out_b, final_g, *, tm, tf):
    n = x1.shape[0]
    row = lambda w: pl.BlockSpec((tm, w), lambda i: (i, 0))
    return pl.pallas_call(
        functools.partial(_ffn2_body, tf=tf),
        grid=(n // tm,),
        in_specs=[row(D_MODEL), row(D_ATT), row(D_SSM), _resident(att_g.shape), _resident(w_mix_b.shape),
                  _resident(norm_g.shape), _resident(w_in_b.shape), _resident(w_out_b.shape),
                  _resident(final_g.shape)],
        out_specs=row(D_MODEL),
        out_shape=jax.ShapeDtypeStruct((n, D_MODEL), F32),
        scratch_shapes=[pltpu.VMEM((tm, D_FF), BF16)],
        compiler_params=_cparams(("arbitrary",)),
        name="ffn2",
    )(x1, att, gn, att_g, w_mix_b, norm_g, w_in_b, w_out_b, final_g)


def _small_slab(raw, bias):
    v = raw + bias
    t = jnp.log1p(jnp.exp(-jnp.abs(v)))
    lane = lax.broadcasted_iota(jnp.int32, v.shape, 1)
    logf = -(jnp.maximum(-v, 0.0) + t)
    dt = jnp.maximum(v, 0.0) + t
    return jnp.where(lane < DT_LANE0, logf, jnp.where(lane < 2 * DT_LANE0, dt, 0.0))


def _inproj_aug_body(x_ref, g_ref, wq_ref, wk_ref, wv_ref, wz_ref, wx_ref, ws_ref, bs_ref,
                     eq_ref, ek_ref, ev_ref,
                     k_ref, v_ref, qa_ref, ka_ref, va_ref, z_ref, xbc_ref, slab_ref,
                     carry_scr, *, tiles_per_seq, tm):
    i = pl.program_id(0)

    @pl.when(i % tiles_per_seq == 0)
    def _():
        carry_scr[...] = jnp.zeros_like(carry_scr)

    h = _rms(x_ref[...], g_ref[...]).astype(BF16)
    q = _dot(h, wq_ref[...])
    k = _dot(h, wk_ref[...])
    v = _dot(h, wv_ref[...])
    k_ref[0] = k.T
    v_ref[0] = v.T
    z_ref[...] = _dot(h, wz_ref[...])
    xbc_ref[...] = _dot(h, wx_ref[...])
    slab = _small_slab(_dot(h, ws_ref[...]), bs_ref[...])
    slab_ref[...] = slab

    lane = lax.broadcasted_iota(jnp.int32, slab.shape, 1)
    logf = jnp.where(lane < DT_LANE0, slab, 0.0)
    c = _dot3r(_tril(tm), logf) + carry_scr[0:1, :]
    carry_scr[0:1, :] = c[tm - 1:tm, :]
    c_hi = c.astype(BF16).astype(F32)
    r1 = c - c_hi
    c_mid = r1.astype(BF16).astype(F32)
    c_lo = (r1 - c_mid).astype(BF16).astype(F32)
    aug = (c_hi + pltpu.roll(c_mid, DT_LANE0, 1) + pltpu.roll(c_lo, 2 * DT_LANE0, 1)
           + jnp.where(lane == 3 * DT_LANE0, 1.0, 0.0)).astype(BF16)
    qs = (q * ATT_SCALE).astype(BF16)
    qa_ref[...] = (_dot(qs, eq_ref[:D_ATT, :]) + _dot(aug, eq_ref[D_ATT:, :])).astype(BF16)
    ka_ref[...] = (_dot(k.astype(BF16), ek_ref[:D_ATT, :]) + _dot(aug, ek_ref[D_ATT:, :])).astype(BF16)
    va_ref[...] = (_dot(v.astype(BF16), ev_ref[:D_ATT, :]) + _dot(aug, ev_ref[D_ATT:, :])).astype(BF16)


def _aug_scatter_mats():
    eq = np.zeros((D_ATT + LANES, N_ATT_HEADS * LANES), np.float32)
    ek = np.zeros((D_ATT + LANES, N_ATT_HEADS * LANES), np.float32)
    ev = np.zeros((D_ATT + LANES, N_ATT_HEADS * LANES), np.float32)
    one_lane = D_ATT + 3 * DT_LANE0
    for h in range(N_ATT_HEADS):
        v0, vsum = (0, HEAD_DIM) if h % 2 == 0 else (HEAD_DIM, 0)
        for d in range(HEAD_DIM):
            eq[h * HEAD_DIM + d, h * LANES + d] = 1.0
            ek[h * HEAD_DIM + d, h * LANES + d] = 1.0
            ev[h * HEAD_DIM + d, h * LANES + v0 + d] = 1.0
        ev[one_lane, h * LANES + vsum] = 1.0
        for p in range(3):
            eq[D_ATT + p * DT_LANE0 + h, h * LANES + AUG0 + p] = 1.0
            eq[one_lane, h * LANES + AUG0 + 3 + p] = 1.0
            ek[one_lane, h * LANES + AUG0 + p] = 1.0
            ek[D_ATT + p * DT_LANE0 + h, h * LANES + AUG0 + 3 + p] = -1.0
    return jnp.asarray(eq, BF16), jnp.asarray(ek, BF16), jnp.asarray(ev, BF16)


def _inproj_aug(x1, norm_g, ws, bias_s, *, tm, seq):
    n = x1.shape[0]
    eq, ek, ev = _aug_scatter_mats()
    row = lambda w: pl.BlockSpec((tm, w), lambda i: (i, 0))
    full = lambda a: pl.BlockSpec(a.shape, lambda i: (0, 0))
    wq, wk, wv, wz, wx, wsm = ws
    aug_w = N_ATT_HEADS * LANES
    tps = seq // tm
    kv_t = pl.BlockSpec((1, D_ATT, tm), lambda i: (i // tps, 0, i % tps))
    return pl.pallas_call(
        functools.partial(_inproj_aug_body, tiles_per_seq=tps, tm=tm),
        grid=(n // tm,),
        in_specs=[row(D_MODEL), full(norm_g), full(wq), full(wk), full(wv), full(wz), full(wx),
                  full(wsm), full(bias_s), full(eq), full(ek), full(ev)],
        out_specs=[kv_t, kv_t, row(aug_w), row(aug_w), row(aug_w),
                   row(D_SSM), row(CONV_DIM), row(LANES)],
        out_shape=[
            jax.ShapeDtypeStruct((n // seq, D_ATT, seq), F32), jax.ShapeDtypeStruct((n // seq, D_ATT, seq), F32),
            jax.ShapeDtypeStruct((n, aug_w), BF16), jax.ShapeDtypeStruct((n, aug_w), BF16),
            jax.ShapeDtypeStruct((n, aug_w), BF16), jax.ShapeDtypeStruct((n, D_SSM), F32),
            jax.ShapeDtypeStruct((n, CONV_DIM), F32), jax.ShapeDtypeStruct((n, LANES), F32)],
        scratch_shapes=[pltpu.VMEM((8, LANES), F32)],
        compiler_params=_cparams(("arbitrary",)),
        name="inproj_prompt",
    )(x1, norm_g, wq, wk, wv, wz, wx, wsm, bias_s, eq, ek, ev)


def _inproj_plain_body(x_ref, g_ref, wq_ref, wk_ref, wv_ref, wz_ref, wx_ref, ws_ref, bs_ref,
                       q_ref, k_ref, v_ref, z_ref, xbc_ref, slab_ref):
    h = _rms(x_ref[...], g_ref[...]).astype(BF16)
    q_ref[...] = _dot(h, wq_ref[...])
    k_ref[...] = _dot(h, wk_ref[...])
    v_ref[...] = _dot(h, wv_ref[...])
    z_ref[...] = _dot(h, wz_ref[...])
    xbc_ref[...] = _dot(h, wx_ref[...])
    slab_ref[...] = _small_slab(_dot(h, ws_ref[...]), bs_ref[...])


def _inproj_plain(x1, norm_g, ws, bias_s):
    n = x1.shape[0]
    wq, wk, wv, wz, wx, wsm = ws
    full = lambda a: pl.BlockSpec(a.shape, lambda i: (0, 0))
    out = lambda w: pl.BlockSpec((n, w), lambda i: (0, 0))
    return pl.pallas_call(
        _inproj_plain_body,
        grid=(1,),
        in_specs=[full(x1), full(norm_g), full(wq), full(wk), full(wv), full(wz), full(wx),
                  full(wsm), full(bias_s)],
        out_specs=[out(D_ATT), out(D_ATT), out(D_ATT), out(D_SSM), out(CONV_DIM), out(LANES)],
        out_shape=[jax.ShapeDtypeStruct((n, w), F32)
                   for w in (D_ATT, D_ATT, D_ATT, D_SSM, CONV_DIM, LANES)],
        compiler_params=_cparams(("arbitrary",)),
        name="inproj_sample",
    )(x1, norm_g, wq, wk, wv, wz, wx, wsm, bias_s)


def _attn_body(q_ref, k_ref, v_ref, o_ref, s_scr, m_scr, acc_scr, *, tq, nt):
    t = pl.program_id(2)
    nl = tq // LANES
    r = lax.broadcasted_iota(jnp.int32, (tq, tq), 0)
    c = lax.broadcasted_iota(jnp.int32, (tq, tq), 1)
    causal = c <= r
    m_scr[...] = jnp.full(m_scr.shape, NEG_BIG, F32)
    acc_scr[...] = jnp.zeros_like(acc_scr)

    def chunk_of(ci):
        first = ci <= t
        sel = jnp.where(first, 0, 1)
        q0 = pl.multiple_of(jnp.where(first, t, nt - 1 - t) * tq, tq)
        k0 = pl.multiple_of(jnp.where(first, ci, ci - t - 1) * tq, tq)
        return sel, q0, k0

    for ci in range(nt + 1):
        sel, q0, k0 = chunk_of(ci)
        for j in range(2):
            lanes = slice(j * LANES, (j + 1) * LANES)
            s = _dot_nt(q_ref[0, pl.ds(q0, tq), lanes], k_ref[0, pl.ds(k0, tq), lanes])
            if ci == nt:
                s = jnp.where(causal, s, NEG_BIG)
            elif ci < nt // 2:
                s = jnp.where(jnp.logical_or(causal, ci != t), s, NEG_BIG)
            s_scr[j, ci] = s
            m = m_scr[sel, j]
            for u in range(nl):
                m = jnp.maximum(m, s[:, u * LANES:(u + 1) * LANES])
            m_scr[sel, j] = m

    for x in range(2):
        for j in range(2):
            m_scr[x, j] = jnp.broadcast_to(jnp.max(m_scr[x, j], axis=1, keepdims=True), (tq, LANES))

    for ci in range(nt + 1):
        sel, q0, k0 = chunk_of(ci)
        for j in range(2):
            lanes = slice(j * LANES, (j + 1) * LANES)
            s = s_scr[j, ci]
            mrep = m_scr[sel, j]
            p = jnp.concatenate([jnp.exp(s[:, u * LANES:(u + 1) * LANES] - mrep) for u in range(nl)], axis=1)
            acc_scr[sel, j] += _dot(p.astype(BF16), v_ref[0, pl.ds(k0, tq), lanes])

    lane = lax.broadcasted_iota(jnp.int32, (tq, LANES), 1)
    for x, tile in enumerate((t, nt - 1 - t)):
        acc_e, acc_o = acc_scr[x, 0], acc_scr[x, 1]
        out = jnp.where(lane < HEAD_DIM, acc_e / acc_e[:, HEAD_DIM:HEAD_DIM + 1], acc_o / acc_o[:, 0:1])
        o_ref[0, pl.ds(pl.multiple_of(tile * tq, tq), tq), :] = out


def _attention(qa, ka, va, *, tq):
    b, L, _ = qa.shape
    nt = L // tq
    assert nt % 2 == 0
    pair = pl.BlockSpec((1, L, 2 * LANES), lambda bi, hp, t: (bi, 0, hp))
    return pl.pallas_call(
        functools.partial(_attn_body, tq=tq, nt=nt),
        grid=(b, N_ATT_HEADS // 2, nt // 2),
        in_specs=[pair, pair, pair],
        out_specs=pl.BlockSpec((1, L, LANES), lambda bi, hp, t: (bi, 0, hp)),
        out_shape=jax.ShapeDtypeStruct((b, L, D_ATT), F32),
        scratch_shapes=[pltpu.VMEM((2, nt + 1, tq, tq), F32), pltpu.VMEM((2, 2, tq, LANES), F32),
                        pltpu.VMEM((2, 2, tq, LANES), F32)],
        compiler_params=_cparams(("arbitrary", "arbitrary", "arbitrary")),
        name="fox_prompt",
    )(qa, ka, va)


def _head_expand_mat():
    e = np.zeros((LANES, D_SSM), np.float32)
    for h in range(N_SSM_HEADS):
        e[DT_LANE0 + h, h * SSM_HEAD_DIM:(h + 1) * SSM_HEAD_DIM] = 1.0
    return jnp.asarray(e, BF16)


def _gate_groupnorm(y, z, gain):
    g = y * _silu(z)
    half = D_SSM // N_SSM_GROUPS
    parts = []
    for gi in range(N_SSM_GROUPS):
        gg = g[:, gi * half:(gi + 1) * half]
        parts.append(gg * lax.rsqrt(jnp.mean(gg * gg, axis=-1, keepdims=True) + EPS))
    return jnp.concatenate(parts, axis=1) * gain


def _ssd_body(xbc_ref, slab_ref, z_ref, cw_ref, cb_ref, arow_ref, dsk_ref, sg_ref, e8_ref,
              gn_ref, st_ref, conv_scr, ht_scr, *, n_chunks):
    c = pl.program_id(1)
    Q = SSD_CHUNK

    @pl.when(c == 0)
    def _():
        conv_scr[0:8, :] = jnp.zeros((8, CONV_DIM), F32)
        ht_scr[...] = jnp.zeros_like(ht_scr)

    u = xbc_ref[0]
    conv_scr[8:8 + Q, :] = u
    conv = (cb_ref[...] + cw_ref[0:1, :] * conv_scr[5:5 + Q, :] + cw_ref[1:2, :] * conv_scr[6:6 + Q, :]
            + cw_ref[2:3, :] * conv_scr[7:7 + Q, :] + cw_ref[3:4, :] * u)
    conv_scr[0:8, :] = u[Q - 8:Q, :]
    xc = _silu(conv)
    xs = xc[:, :D_SSM]
    e8 = e8_ref[...]

    slab = slab_ref[0]
    lane = lax.broadcasted_iota(jnp.int32, slab.shape, 1)
    dtm = jnp.where((lane >= DT_LANE0) & (lane < 2 * DT_LANE0), slab, 0.0)
    a = dtm * arow_ref[...]
    a_c = _dot3r(_tril(Q), a)
    a_ct = a_c.T
    a_last = a_c[Q - 1:Q, :]
    ea_x = _dot3(jnp.exp(a_c), e8)
    dec_x = _dot3(jnp.exp(a_last - a_c), e8)
    xdt = xs * _dot3(dtm, e8)
    xdt_b = xdt.astype(BF16)
    xd_b = (xdt * dec_x).astype(BF16)
    cd_x = ea_x[Q - 1:Q, :]

    r_i = lax.broadcasted_iota(jnp.int32, (Q, Q), 0)
    c_i = lax.broadcasted_iota(jnp.int32, (Q, Q), 1)
    tri = c_i <= r_i
    lane_q = lax.broadcasted_iota(jnp.int32, (Q, LANES), 1)
    ht_old = ht_scr[...]
    gw = D_SSM // N_SSM_GROUPS
    hpg = N_SSM_HEADS // N_SSM_GROUPS
    y_parts, ht_parts = [], []
    for g in range(N_SSM_GROUPS):
        bm = xc[:, D_SSM + g * D_STATE:D_SSM + (g + 1) * D_STATE]
        cm = xc[:, D_SSM + (N_SSM_GROUPS + g) * D_STATE:D_SSM + (N_SSM_GROUPS + g + 1) * D_STATE]
        cb16 = cm.astype(BF16)
        cbm = _dot_nt(cb16, bm.astype(BF16))
        y_off = _dot(cb16, ht_old[:, g * gw:(g + 1) * gw].astype(BF16))
        y_diag = []
        for pr in range(hpg // 2):
            halves = []
            for hh in range(2):
                h = g * hpg + pr * 2 + hh
                col = a_c[:, DT_LANE0 + h:DT_LANE0 + h + 1]
                row = a_ct[DT_LANE0 + h:DT_LANE0 + h + 1, :]
                lm = jnp.where(tri, jnp.exp(col - row), 0.0)
                sc = (cbm * lm).astype(BF16)
                lo = g * gw + pr * LANES
                halves.append(_dot(sc, xdt_b[:, lo:lo + LANES]))
            y_diag.append(jnp.where(lane_q < SSM_HEAD_DIM, halves[0], halves[1]))
        y_parts.append(jnp.concatenate(y_diag, axis=1) + y_off * ea_x[:, g * gw:(g + 1) * gw])
        new = _dot(bm.T.astype(BF16), xd_b[:, g * gw:(g + 1) * gw])
        ht_parts.append(ht_old[:, g * gw:(g + 1) * gw] * cd_x[:, g * gw:(g + 1) * gw] + new)
    ht_new = jnp.concatenate(ht_parts, axis=1)
    ht_scr[...] = ht_new
    y = jnp.concatenate(y_parts, axis=1) + dsk_ref[...] * xs
    gn_ref[0] = _gate_groupnorm(y, z_ref[0], sg_ref[...]).astype(BF16)

    @pl.when(c == n_chunks - 1)
    def _():
        st_ref[0] = ht_new.T


def _ssd_prompt(xbc, slab, z, conv_w, conv_b, a_row, dsk_x, ssm_g):
    b, L, _ = xbc.shape
    nc = L // SSD_CHUNK
    e8 = _head_expand_mat()
    blk = lambda w: pl.BlockSpec((1, SSD_CHUNK, w), lambda bi, ci: (bi, ci, 0))
    full = lambda a: pl.BlockSpec(a.shape, lambda bi, ci: (0, 0))
    return pl.pallas_call(
        functools.partial(_ssd_body, n_chunks=nc),
        grid=(b, nc),
        in_specs=[blk(CONV_DIM), blk(LANES), blk(D_SSM), full(conv_w), full(conv_b), full(a_row),
                  full(dsk_x), full(ssm_g), full(e8)],
        out_specs=[blk(D_SSM), pl.BlockSpec((1, D_SSM, D_STATE), lambda bi, ci: (bi, 0, 0))],
        out_shape=[jax.ShapeDtypeStruct((b, L, D_SSM), BF16),
                   jax.ShapeDtypeStruct((b, D_SSM, D_STATE), F32)],
        scratch_shapes=[pltpu.VMEM((8 + SSD_CHUNK, CONV_DIM), F32), pltpu.VMEM((D_STATE, D_SSM), F32)],
        compiler_params=_cparams(("arbitrary", "arbitrary")),
        name="ssd_prompt",
    )(xbc, slab, z, conv_w, conv_b, a_row, dsk_x, ssm_g, e8)


def _ssd_step_body(xbc_ref, sc_ref, slab_ref, z_ref, st_ref, cw_ref, cb_ref, arow_ref, dsk_ref,
                   sg_ref, e8_ref, gn_ref, so_ref, xc_scr, dtx_scr, decx_scr):
    b = pl.program_id(0)

    @pl.when(b == 0)
    def _():
        conv = (cb_ref[...] + cw_ref[0:1, :] * sc_ref[0] + cw_ref[1:2, :] * sc_ref[1]
                + cw_ref[2:3, :] * sc_ref[2] + cw_ref[3:4, :] * xbc_ref[...])
        xc_scr[...] = _silu(conv)
        slab = slab_ref[...]
        lane = lax.broadcasted_iota(jnp.int32, slab.shape, 1)
        dtm = jnp.where((lane >= DT_LANE0) & (lane < 2 * DT_LANE0), slab, 0.0)
        e8 = e8_ref[...]
        dtx_scr[...] = _dot3(dtm, e8)
        decx_scr[...] = _dot3(jnp.exp(dtm * arow_ref[...]), e8)

    xrow = xc_scr[pl.ds(b, 1), :]
    xs = xrow[:, :D_SSM]
    xdt = xs * dtx_scr[pl.ds(b, 1), :]
    dec = decx_scr[pl.ds(b, 1), :]
    gw = D_SSM // N_SSM_GROUPS
    y_rows = []
    for j in range(D_SSM // LANES):
        g = (j * LANES) // gw
        bm = xrow[:, D_SSM + g * D_STATE:D_SSM + (g + 1) * D_STATE]
        cm = xrow[:, D_SSM + (N_SSM_GROUPS + g) * D_STATE:D_SSM + (N_SSM_GROUPS + g + 1) * D_STATE]
        xcol = _row_to_col(xdt[:, j * LANES:(j + 1) * LANES])
        dcol = _row_to_col(dec[:, j * LANES:(j + 1) * LANES])
        hs = dcol * st_ref[0, j * LANES:(j + 1) * LANES, :] + xcol * bm
        so_ref[0, j * LANES:(j + 1) * LANES, :] = hs
        y_rows.append(_col_to_row(jnp.sum(hs * cm, axis=1, keepdims=True)))
    y = jnp.concatenate(y_rows, axis=1) + dsk_ref[...] * xs
    gn_ref[0] = _gate_groupnorm(y, z_ref[pl.ds(b, 1), :], sg_ref[...]).astype(BF16)


def _ssd_step(xbc, sconv_t, slab, z, state, conv_w, conv_b, a_row, dsk_x, ssm_g):
    nb = xbc.shape[0]
    e8 = _head_expand_mat()
    full2 = lambda a: pl.BlockSpec(a.shape, lambda bi: (0,) * a.ndim)
    gn, st = pl.pallas_call(
        _ssd_step_body,
        grid=(nb,),
        in_specs=[full2(xbc), full2(sconv_t), full2(slab), full2(z),
                  pl.BlockSpec((1, D_SSM, D_STATE), lambda bi: (bi, 0, 0)),
                  full2(conv_w), full2(conv_b), full2(a_row), full2(dsk_x), full2(ssm_g), full2(e8)],
        out_specs=[pl.BlockSpec((1, 1, D_SSM), lambda bi: (bi, 0, 0)),
                   pl.BlockSpec((1, D_SSM, D_STATE), lambda bi: (bi, 0, 0))],
        out_shape=[jax.ShapeDtypeStruct((nb, 1, D_SSM), BF16),
                   jax.ShapeDtypeStruct((nb, D_SSM, D_STATE), F32)],
        scratch_shapes=[pltpu.VMEM((nb, CONV_DIM), F32), pltpu.VMEM((nb, D_SSM), F32),
                        pltpu.VMEM((nb, D_SSM), F32)],
        compiler_params=_cparams(("arbitrary",)),
        name="ssd_step",
    )(xbc, sconv_t, slab, z, state, conv_w, conv_b, a_row, dsk_x, ssm_g, e8)
    return gn.reshape(nb, D_SSM), st


def _decode_body(pt_ref, q_ref, k2_ref, v2_ref, lf2_ref, *refs, pps, n_groups):
    k_refs = refs[:pps]
    v_refs = refs[pps:2 * pps]
    lf_refs = refs[2 * pps:3 * pps]
    o_ref = refs[3 * pps]
    qblk_scr, m_scr, l_scr, acc_scr, carry_scr = refs[3 * pps + 1:]
    g = pl.program_id(1)
    H, HD, P = N_ATT_HEADS, HEAD_DIM, PAGE_SIZE
    r8 = lax.broadcasted_iota(jnp.int32, (H, D_ATT), 0)
    c8 = lax.broadcasted_iota(jnp.int32, (H, D_ATT), 1)
    own_head = c8 // HD == r8

    @pl.when(g == 0)
    def _():
        qs = (q_ref[0] * ATT_SCALE).astype(BF16).astype(F32)
        qblk = jnp.where(own_head, qs, 0.0)
        qblk_scr[...] = qblk.astype(BF16)
        k2 = k2_ref[0].astype(BF16).astype(F32)
        m_scr[...] = jnp.broadcast_to(jnp.sum(qblk * k2, axis=1, keepdims=True), m_scr.shape)
        l_scr[...] = jnp.ones_like(l_scr)
        acc_scr[...] = jnp.broadcast_to(v2_ref[0].astype(BF16).astype(F32), acc_scr.shape)
        carry_scr[...] = jnp.broadcast_to(_row_to_col(lf2_ref[0])[0:H, :], carry_scr.shape)

    kcat = jnp.concatenate([kr[0].astype(BF16) for kr in k_refs], axis=1)
    s = _dot(qblk_scr[...], kcat)
    x = jnp.concatenate([lr[0] for lr in lf_refs], axis=0)
    rr = lax.broadcasted_iota(jnp.int32, (P, P), 0)
    cc = lax.broadcasted_iota(jnp.int32, (P, P), 1)
    later = (rr > cc).astype(BF16)
    rev_local = _dot3(x, later)
    tot = jnp.sum(x, axis=1, keepdims=True)
    carry = carry_scr[:, 0:1]
    s_pages = [None] * pps
    for i in reversed(range(pps)):
        s_pages[i] = s[:, i * P:(i + 1) * P] + rev_local[i * H:(i + 1) * H, :] + carry
        carry = carry + tot[i * H:(i + 1) * H, :]
    carry_scr[...] = jnp.broadcast_to(carry, carry_scr.shape)
    st = jnp.concatenate(s_pages, axis=1)

    m_old = m_scr[:, 0:1]
    m_new = jnp.maximum(m_old, jnp.max(st, axis=1, keepdims=True))
    alpha = jnp.exp(m_old - m_new)
    p = jnp.exp(st - m_new)
    l_new = alpha * l_scr[:, 0:1] + jnp.sum(p, axis=1, keepdims=True)
    vcat = jnp.concatenate([vr[0].astype(BF16) for vr in v_refs], axis=1)
    acc = alpha * acc_scr[...] + _dot_nt(p.astype(BF16), vcat)
    m_scr[...] = jnp.broadcast_to(m_new, m_scr.shape)
    l_scr[...] = jnp.broadcast_to(l_new, l_scr.shape)
    acc_scr[...] = acc

    @pl.when(g == n_groups - 1)
    def _():
        o_ref[0] = jnp.sum(jnp.where(own_head, acc / l_new, 0.0), axis=0, keepdims=True)


def _decode_attention(q, k2, v2, slab2, cache_k, cache_v, cache_logf, page_table, *, pps):
    nb, n_pages = page_table.shape
    ng = n_pages // pps
    row3 = lambda a: a.reshape(nb, 1, a.shape[-1])
    rowspec = lambda w: pl.BlockSpec((1, 1, w), lambda b, g, pt: (b, 0, 0))

    def page_spec(rows, i):
        return pl.BlockSpec((1, rows, PAGE_SIZE),
                            lambda b, g, pt, i=i: (pt[b, (ng - 1 - g) * pps + i], 0, 0))

    in_specs = ([rowspec(D_ATT), rowspec(D_ATT), rowspec(D_ATT), rowspec(LANES)]
                + [page_spec(D_ATT, i) for i in range(pps)]
                + [page_spec(D_ATT, i) for i in range(pps)]
                + [page_spec(N_ATT_HEADS, i) for i in range(pps)])
    out = pl.pallas_call(
        functools.partial(_decode_body, pps=pps, n_groups=ng),
        grid_spec=pltpu.PrefetchScalarGridSpec(
            num_scalar_prefetch=1,
            grid=(nb, ng),
            in_specs=in_specs,
            out_specs=pl.BlockSpec((1, 1, D_ATT), lambda b, g, pt: (b, 0, 0)),
            scratch_shapes=[pltpu.VMEM((N_ATT_HEADS, D_ATT), BF16), pltpu.VMEM((8, LANES), F32),
                            pltpu.VMEM((8, LANES), F32), pltpu.VMEM((8, D_ATT), F32),
                            pltpu.VMEM((8, LANES), F32)]),
        out_shape=jax.ShapeDtypeStruct((nb, 1, D_ATT), F32),
        compiler_params=_cparams(("arbitrary", "arbitrary")),
        name="fox_decode",
    )(page_table, row3(q), row3(k2), row3(v2), row3(slab2),
      *([cache_k] * pps), *([cache_v] * pps), *([cache_logf] * pps))
    return out.reshape(nb, D_ATT)


DECODE_GROUP = 16
DECODE_SLOTS = 2


def _fox_body(pt_ref, qa_ref, ka_ref, va_ref, qs_ref, ks_ref, vs_ref, lfs_ref, ck_hbm, cv_hbm, clf_hbm,
              o_ref, os_ref,
              s_scr, m_scr, acc_scr, kbuf, vbuf, lbuf, sems, qblk_scr, dm_scr, dl_scr, dacc_scr, carry_scr,
              *, tq, nt, gps, gpseq, n_groups):
    G, NS = DECODE_GROUP, DECODE_SLOTS
    H, HD, P = N_ATT_HEADS, HEAD_DIM, PAGE_SIZE
    t = pl.program_id(2)
    sid = (pl.program_id(0) * pl.num_programs(1) + pl.program_id(1)) * pl.num_programs(2) + t
    nl = tq // LANES

    def group_copies(gg):
        slot = gg % NS
        src = jnp.minimum(gg, n_groups - 1)
        seq = src // gpseq
        base = (gpseq - 1 - src % gpseq) * G
        copies = []
        for i in range(G):
            page = pt_ref[seq, base + i]
            copies.append(pltpu.make_async_copy(ck_hbm.at[page], kbuf.at[slot, i], sems.at[slot, 0]))
            copies.append(pltpu.make_async_copy(cv_hbm.at[page], vbuf.at[slot, i], sems.at[slot, 1]))
            copies.append(pltpu.make_async_copy(clf_hbm.at[page], lbuf.at[slot, i], sems.at[slot, 2]))
        return copies

    def start_group(gg):
        for cp in group_copies(gg):
            cp.start()

    def wait_group(gg):
        for cp in group_copies(gg):
            cp.wait()

    r8 = lax.broadcasted_iota(jnp.int32, (H, D_ATT), 0)
    c8 = lax.broadcasted_iota(jnp.int32, (H, D_ATT), 1)
    own_head = c8 // HD == r8

    def decode_init():
        qs = (qs_ref[0] * ATT_SCALE).astype(BF16).astype(F32)
        qblk = jnp.where(own_head, qs, 0.0)
        qblk_scr[...] = qblk.astype(BF16)
        k2 = ks_ref[0].astype(BF16).astype(F32)
        dm_scr[...] = jnp.broadcast_to(jnp.sum(qblk * k2, axis=1, keepdims=True), dm_scr.shape)
        dl_scr[...] = jnp.ones_like(dl_scr)
        dacc_scr[...] = jnp.broadcast_to(vs_ref[0].astype(BF16).astype(F32), dacc_scr.shape)
        carry_scr[...] = jnp.broadcast_to(_row_to_col(lfs_ref[0])[0:H, :], carry_scr.shape)

    def decode_group(slot):
        kcat = jnp.concatenate([kbuf[slot, i].astype(BF16) for i in range(G)], axis=1)
        s = _dot(qblk_scr[...], kcat)
        x = jnp.concatenate([lbuf[slot, i] for i in range(G)], axis=0)
        rr = lax.broadcasted_iota(jnp.int32, (P, P), 0)
        cc = lax.broadcasted_iota(jnp.int32, (P, P), 1)
        rev_local = _dot3(x, (rr > cc).astype(BF16))
        tot = jnp.sum(x, axis=1, keepdims=True)
        carry = carry_scr[:, 0:1]
        s_pages = [None] * G
        for i in reversed(range(G)):
            s_pages[i] = s[:, i * P:(i + 1) * P] + rev_local[i * H:(i + 1) * H, :] + carry
            carry = carry + tot[i * H:(i + 1) * H, :]
        carry_scr[...] = jnp.broadcast_to(carry, carry_scr.shape)
        st = jnp.concatenate(s_pages, axis=1)
        m_old = dm_scr[:, 0:1]
        m_new = jnp.maximum(m_old, jnp.max(st, axis=1, keepdims=True))
        alpha = jnp.exp(m_old - m_new)
        p = jnp.exp(st - m_new)
        l_new = alpha * dl_scr[:, 0:1] + jnp.sum(p, axis=1, keepdims=True)
        vcat = jnp.concatenate([vbuf[slot, i].astype(BF16) for i in range(G)], axis=1)
        dacc_scr[...] = alpha * dacc_scr[...] + _dot_nt(p.astype(BF16), vcat)
        dm_scr[...] = jnp.broadcast_to(m_new, dm_scr.shape)
        dl_scr[...] = jnp.broadcast_to(l_new, dl_scr.shape)

    def decode_finish():
        o = dacc_scr[...] / dl_scr[:, 0:1]
        os_ref[0] = jnp.sum(jnp.where(own_head, o, 0.0), axis=0, keepdims=True)

    r = lax.broadcasted_iota(jnp.int32, (tq, tq), 0)
    c = lax.broadcasted_iota(jnp.int32, (tq, tq), 1)
    causal = c <= r

    def chunk_of(ci):
        first = ci <= t
        sel = jnp.where(first, 0, 1)
        q0 = pl.multiple_of(jnp.where(first, t, nt - 1 - t) * tq, tq)
        k0 = pl.multiple_of(jnp.where(first, ci, ci - t - 1) * tq, tq)
        return sel, q0, k0

    def scores(ci):
        sel, q0, k0 = chunk_of(ci)
        for j in range(2):
            lanes = slice(j * LANES, (j + 1) * LANES)
            s = _dot_nt(qa_ref[0, pl.ds(q0, tq), lanes], ka_ref[0, pl.ds(k0, tq), lanes])
            if ci == nt:
                s = jnp.where(causal, s, NEG_BIG)
            elif ci < nt // 2:
                s = jnp.where(jnp.logical_or(causal, ci != t), s, NEG_BIG)
            s_scr[j, ci] = s
            m = m_scr[sel, j]
            for u in range(nl):
                m = jnp.maximum(m, s[:, u * LANES:(u + 1) * LANES])
            m_scr[sel, j] = m

    def row_max():
        for x in range(2):
            for j in range(2):
                m_scr[x, j] = jnp.broadcast_to(jnp.max(m_scr[x, j], axis=1, keepdims=True), (tq, LANES))

    def weighted(ci):
        sel, q0, k0 = chunk_of(ci)
        for j in range(2):
            lanes = slice(j * LANES, (j + 1) * LANES)
            s = s_scr[j, ci]
            mrep = m_scr[sel, j]
            p = jnp.concatenate([jnp.exp(s[:, u * LANES:(u + 1) * LANES] - mrep) for u in range(nl)], axis=1)
            acc_scr[sel, j] += _dot(p.astype(BF16), va_ref[0, pl.ds(k0, tq), lanes])

    def write_out():
        lane = lax.broadcasted_iota(jnp.int32, (tq, LANES), 1)
        for x, tile in enumerate((t, nt - 1 - t)):
            acc_e, acc_o = acc_scr[x, 0], acc_scr[x, 1]
            out = jnp.where(lane < HEAD_DIM, acc_e / acc_e[:, HEAD_DIM:HEAD_DIM + 1], acc_o / acc_o[:, 0:1])
            o_ref[0, pl.ds(pl.multiple_of(tile * tq, tq), tq), :] = out

    items = ([functools.partial(scores, ci) for ci in range(nt + 1)] + [row_max]
             + [functools.partial(weighted, ci) for ci in range(nt + 1)] + [write_out])
    per_part = -(-len(items) // gps)
    parts = [items[i * per_part:(i + 1) * per_part] for i in range(gps)]

    @pl.when(sid == 0)
    def _():
        for g0 in range(NS - 1):
            start_group(g0)

    pl.when((sid * gps) % gpseq == 0)(decode_init)
    m_scr[...] = jnp.full(m_scr.shape, NEG_BIG, F32)
    acc_scr[...] = jnp.zeros_like(acc_scr)
    for gi in range(gps):
        gg = sid * gps + gi
        wait_group(gg)
        start_group(gg + (NS - 1))
        for item in parts[gi]:
            item()
        decode_group(gg % NS)
    pl.when((sid * gps + gps - 1) % gpseq == gpseq - 1)(decode_finish)

    @pl.when(sid == pl.num_programs(0) * pl.num_programs(1) * pl.num_programs(2) - 1)
    def _():
        for extra in range(NS - 1):
            wait_group(n_groups + extra)


def _fox_attention(qa, ka, va, q_s, k_s, v_s, slab_s, cache_k, cache_v, cache_logf, page_table, *, tq):
    b, L, _ = qa.shape
    nt = L // tq
    nb, n_pages = page_table.shape
    n_steps = b * (N_ATT_HEADS // 2) * (nt // 2)
    n_groups = nb * n_pages // DECODE_GROUP
    gps = n_groups // n_steps
    gpseq = n_pages // DECODE_GROUP
    assert nt % 2 == 0 and n_pages % DECODE_GROUP == 0
    assert gps * n_steps == n_groups and gpseq % gps == 0 and n_groups >= DECODE_SLOTS
    spq = gpseq // gps
    hp_n, t_n = N_ATT_HEADS // 2, nt // 2
    seq_of = lambda bi, hp, t: ((bi * hp_n + hp) * t_n + t) // spq
    pair = pl.BlockSpec((1, L, 2 * LANES), lambda bi, hp, t, pt: (bi, 0, hp))
    srow = lambda w: pl.BlockSpec((1, 1, w), lambda bi, hp, t, pt: (seq_of(bi, hp, t), 0, 0))
    anyspec = pl.BlockSpec(memory_space=pl.ANY)
    row3 = lambda a: a.reshape(nb, 1, a.shape[-1])
    G, NS = DECODE_GROUP, DECODE_SLOTS
    att, att_s = pl.pallas_call(
        functools.partial(_fox_body, tq=tq, nt=nt, gps=gps, gpseq=gpseq, n_groups=n_groups),
        grid_spec=pltpu.PrefetchScalarGridSpec(
            num_scalar_prefetch=1,
            grid=(b, hp_n, t_n),
            in_specs=[pair, pair, pair, srow(D_ATT), srow(D_ATT), srow(D_ATT), srow(LANES),
                      anyspec, anyspec, anyspec],
            out_specs=[pl.BlockSpec((1, L, LANES), lambda bi, hp, t, pt: (bi, 0, hp)),
                       pl.BlockSpec((1, 1, D_ATT), lambda bi, hp, t, pt: (seq_of(bi, hp, t), 0, 0))],
            scratch_shapes=[
                pltpu.VMEM((2, nt + 1, tq, tq), F32), pltpu.VMEM((2, 2, tq, LANES), F32),
                pltpu.VMEM((2, 2, tq, LANES), F32),
                pltpu.VMEM((NS, G, D_ATT, PAGE_SIZE), F32), pltpu.VMEM((NS, G, D_ATT, PAGE_SIZE), F32),
                pltpu.VMEM((NS, G, N_ATT_HEADS, PAGE_SIZE), F32), pltpu.SemaphoreType.DMA((NS, 3)),
                pltpu.VMEM((N_ATT_HEADS, D_ATT), BF16), pltpu.VMEM((8, LANES), F32),
                pltpu.VMEM((8, LANES), F32), pltpu.VMEM((8, D_ATT), F32), pltpu.VMEM((8, LANES), F32)]),
        out_shape=[jax.ShapeDtypeStruct((b, L, D_ATT), F32), jax.ShapeDtypeStruct((nb, 1, D_ATT), F32)],
        compiler_params=_cparams(("arbitrary", "arbitrary", "arbitrary")),
        name="fox_attention",
    )(page_table, qa, ka, va, row3(q_s), row3(k_s), row3(v_s), row3(slab_s), cache_k, cache_v, cache_logf)
    return att, att_s.reshape(nb, D_ATT)


def _lane_pad(vec, lane0):
    out = jnp.zeros((1, LANES), F32)
    return out.at[0, lane0:lane0 + vec.shape[0]].set(vec.astype(F32))


def kernel(x_prompt, x_sample, cache_k, cache_v, cache_logf, page_table, state_conv, state_ssm,
           ffn1_norm, w_ffn1_in, w_ffn1_out, mix_norm, w_in, b_f, conv_w, conv_b, dt_bias, a_log,
           d_skip, att_out_norm, ssm_out_norm, w_out, ffn2_norm, w_ffn2_in, w_ffn2_out, final_norm):
    depth = w_in.shape[0]
    assert depth == 1
    B, L, _ = x_prompt.shape
    nb, T, _ = x_sample.shape
    assert T == 1
    H, HD = N_ATT_HEADS, HEAD_DIM

    l0 = 0
    w1i, w1o = w_ffn1_in[l0].astype(BF16), w_ffn1_out[l0].astype(BF16)
    w2i, w2o = w_ffn2_in[l0].astype(BF16), w_ffn2_out[l0].astype(BF16)
    wi = w_in[l0]
    o_f = 3 * D_ATT
    o_z = o_f + H
    o_x = o_z + D_SSM
    o_dt = o_x + CONV_DIM
    w_small = jnp.zeros((D_MODEL, LANES), F32)
    w_small = w_small.at[:, 0:H].set(wi[:, o_f:o_f + H])
    w_small = w_small.at[:, DT_LANE0:DT_LANE0 + N_SSM_HEADS].set(wi[:, o_dt:o_dt + N_SSM_HEADS])
    ws = tuple(w.astype(BF16) for w in (wi[:, 0:D_ATT], wi[:, D_ATT:2 * D_ATT], wi[:, 2 * D_ATT:3 * D_ATT],
                                        wi[:, o_z:o_z + D_SSM], wi[:, o_x:o_x + CONV_DIM], w_small))
    bias_s = _lane_pad(b_f[l0], 0) + _lane_pad(dt_bias[l0], DT_LANE0)
    a_row = _lane_pad(-jnp.exp(a_log[l0].astype(F32)), DT_LANE0)
    dsk_x = jnp.repeat(d_skip[l0].astype(F32), SSM_HEAD_DIM)[None, :]
    row = lambda v: v.astype(F32)[None, :]
    g1, gm, g2, gf = row(ffn1_norm[l0]), row(mix_norm[l0]), row(ffn2_norm[l0]), row(final_norm)
    ga, gs = row(att_out_norm[l0]), row(ssm_out_norm[l0])
    cw, cb = conv_w[l0].astype(F32), row(conv_b[l0])
    wm = w_out[l0].astype(BF16)

    TM, TF = FFN_ROWS, FFN_CHUNK
    xp = x_prompt.reshape(B * L, D_MODEL)
    x1 = _ffn(xp, g1, w1i, w1o, tm=TM, tf=TF)
    k_p, v_p, qa, ka, va, z_p, xbc_p, slab_p = _inproj_aug(x1, gm, ws, bias_s, tm=INPROJ_ROWS, seq=L)
    xs = x_sample.reshape(nb, D_MODEL)
    x1s = _ffn(xs, g1, w1i, w1o, tm=nb, tf=TF)
    q_s, k_s, v_s, z_s, xbc_s, slab_s = _inproj_plain(x1s, gm, ws, bias_s)

    n_pool = cache_k.shape[1]
    ck = jnp.transpose(cache_k[l0], (0, 2, 3, 1)).reshape(n_pool, D_ATT, PAGE_SIZE)
    cv = jnp.transpose(cache_v[l0], (0, 2, 3, 1)).reshape(n_pool, D_ATT, PAGE_SIZE)
    clf = jnp.transpose(cache_logf[l0], (0, 2, 1))
    att, att_s = _fox_attention(qa.reshape(B, L, -1), ka.reshape(B, L, -1), va.reshape(B, L, -1),
                                q_s, k_s, v_s, slab_s, ck, cv, clf, page_table, tq=ATT_ROWS)

    gn_p, st_p = _ssd_prompt(xbc_p.reshape(B, L, -1), slab_p.reshape(B, L, -1), z_p.reshape(B, L, -1),
                             cw, cb, a_row, dsk_x, gs)
    y_p = _ffn2(x1, att.reshape(B * L, D_ATT), gn_p.reshape(B * L, D_SSM), ga, wm, g2, w2i, w2o, gf,
                tm=TM, tf=TF)

    sconv_t = jnp.swapaxes(state_conv[l0], 0, 1)
    gn_s, st_s = _ssd_step(xbc_s, sconv_t, slab_s, z_s,
                           state_ssm[l0].reshape(nb, D_SSM, D_STATE), cw, cb, a_row, dsk_x, gs)
    y_s = _ffn2(x1s, att_s, gn_s, ga, wm, g2, w2i, w2o, gf, tm=nb, tf=TF)

    xbc_p3 = xbc_p.reshape(B, L, CONV_DIM)
    return (
        y_p.reshape(B, L, D_MODEL),
        y_s.reshape(nb, 1, D_MODEL),
        jnp.transpose(k_p.reshape(1, B, H, HD, L), (0, 1, 4, 2, 3)),
        jnp.transpose(v_p.reshape(1, B, H, HD, L), (0, 1, 4, 2, 3)),
        slab_p[:, :H].reshape(1, B, L, H),
        xbc_p3[:, L - (CONV_W - 1):, :][None],
        st_p.reshape(1, B, N_SSM_HEADS, SSM_HEAD_DIM, D_STATE),
        k_s.reshape(1, nb, 1, H, HD),
        v_s.reshape(1, nb, 1, H, HD),
        slab_s[:, :H].reshape(1, nb, 1, H),
        jnp.concatenate([state_conv[l0][:, 1:, :], xbc_s[:, None, :]], axis=1)[None],
        st_s.reshape(1, nb, N_SSM_HEADS, SSM_HEAD_DIM, D_STATE),
    )
```

```python
import functools

import numpy as np
import jax
import jax.numpy as jnp
from jax import lax
from jax.experimental import pallas as pl
from jax.experimental.pallas import tpu as pltpu

F32 = jnp.float32
BF16 = jnp.bfloat16

D_MODEL = 1024
D_ATT = 512
D_SSM = 512
HEAD_DIM = 64
N_ATT_HEADS = 8
N_SSM_HEADS = 8
SSM_HEAD_DIM = 64
N_SSM_GROUPS = 2
D_STATE = 128
CONV_W = 4
CONV_DIM = D_SSM + 2 * N_SSM_GROUPS * D_STATE
SSD_CHUNK = 128
PAGE_SIZE = 128
D_FF = 2816
FFN_RESIDUAL = 0.5
EPS = 1e-6
ATT_SCALE = HEAD_DIM ** -0.5
LANES = 128
DT_LANE0 = 8
AUG0 = HEAD_DIM
NEG_BIG = -1e30

VMEM_LIMIT = 56 * 1024 * 1024
FFN_ROWS = 1024
FFN_CHUNK = 256
INPROJ_ROWS = 512
ATT_ROWS = 256


def _cparams(sem):
    return pltpu.CompilerParams(dimension_semantics=sem, vmem_limit_bytes=VMEM_LIMIT)


def _dot(a, b):
    return jnp.dot(a, b, preferred_element_type=F32)


def _dot_nt(a, b):
    return lax.dot_general(a, b, (((1,), (1,)), ((), ())), preferred_element_type=F32)


def _split3(x):
    hi = x.astype(BF16)
    r1 = x - hi.astype(F32)
    mid = r1.astype(BF16)
    lo = (r1 - mid.astype(F32)).astype(BF16)
    return hi, mid, lo


def _dot3(x, m):
    hi, mid, lo = _split3(x)
    return _dot(hi, m) + _dot(mid, m) + _dot(lo, m)


def _dot3r(m, x):
    hi, mid, lo = _split3(x)
    return _dot(m, hi) + _dot(m, mid) + _dot(m, lo)


def _rms(x, g):
    return x * lax.rsqrt(jnp.mean(x * x, axis=-1, keepdims=True) + EPS) * g


def _silu(x):
    return x / (1.0 + jnp.exp(-x))


def _tril(n, dtype=BF16):
    r = lax.broadcasted_iota(jnp.int32, (n, n), 0)
    c = lax.broadcasted_iota(jnp.int32, (n, n), 1)
    return (c <= r).astype(dtype)


def _row_to_col(row):
    n = row.shape[1]
    r = lax.broadcasted_iota(jnp.int32, (n, n), 0)
    c = lax.broadcasted_iota(jnp.int32, (n, n), 1)
    return jnp.sum(jnp.where(r == c, jnp.broadcast_to(row, (n, n)), 0.0), axis=1, keepdims=True)


def _col_to_row(col):
    n = col.shape[0]
    r = lax.broadcasted_iota(jnp.int32, (n, n), 0)
    c = lax.broadcasted_iota(jnp.int32, (n, n), 1)
    return jnp.sum(jnp.where(r == c, jnp.broadcast_to(col, (n, n)), 0.0), axis=0, keepdims=True)


def _resident(shape):
    return pl.BlockSpec(shape, lambda i: (0,) * len(shape), pipeline_mode=pl.Buffered(1))


def _swiglu_half(x, g_ref, wi_ref, wo_ref, a_scr, tf):
    h = _rms(x, g_ref[...]).astype(BF16)
    for c in range(D_FF // tf):
        gate = _dot(h, wi_ref[:, c * tf:(c + 1) * tf])
        up = _dot(h, wi_ref[:, D_FF + c * tf:D_FF + (c + 1) * tf])
        a_scr[:, c * tf:(c + 1) * tf] = (_silu(gate) * up).astype(BF16)
    return x + FFN_RESIDUAL * _dot(a_scr[...], wo_ref[...])


def _ffn_body(x_ref, g_ref, wi_ref, wo_ref, o_ref, a_scr, *, tf):
    o_ref[...] = _swiglu_half(x_ref[...], g_ref, wi_ref, wo_ref, a_scr, tf)


def _ffn(x, norm_g, w_in_b, w_out_b, *, tm, tf):
    n = x.shape[0]
    row = lambda w: pl.BlockSpec((tm, w), lambda i: (i, 0))
    return pl.pallas_call(
        functools.partial(_ffn_body, tf=tf),
        grid=(n // tm,),
        in_specs=[row(D_MODEL), _resident(norm_g.shape), _resident(w_in_b.shape), _resident(w_out_b.shape)],
        out_specs=row(D_MODEL),
        out_shape=jax.ShapeDtypeStruct((n, D_MODEL), F32),
        scratch_shapes=[pltpu.VMEM((tm, D_FF), BF16)],
        compiler_params=_cparams(("arbitrary",)),
        name="ffn1",
    )(x, norm_g, w_in_b, w_out_b)


def _ffn2_body(x_ref, att_ref, gn_ref, ag_ref, wm_ref, g_ref, wi_ref, wo_ref, fg_ref, o_ref, a_scr, *, tf):
    an = _rms(att_ref[...], ag_ref[...]).astype(BF16)
    x2 = x_ref[...] + _dot(an, wm_ref[:D_ATT, :]) + _dot(gn_ref[...], wm_ref[D_ATT:, :])
    o_ref[...] = _rms(_swiglu_half(x2, g_ref, wi_ref, wo_ref, a_scr, tf), fg_ref[...])


def _ffn2(x1, att, gn, att_g, w_mix_b, norm_g, w_in_b, w_out_b, final_g, *, tm, tf):
    n = x1.shape[0]
    row = lambda w: pl.BlockSpec((tm, w), lambda i: (i, 0))
    return pl.pallas_call(
        functools.partial(_ffn2_body, tf=tf),
        grid=(n // tm,),
        in_specs=[row(D_MODEL), row(D_ATT), row(D_SSM), _resident(att_g.shape), _resident(w_mix_b.shape),
                  _resident(norm_g.shape), _resident(w_in_b.shape), _resident(w_out_b.shape),
                  _resident(final_g.shape)],
        out_specs=row(D_MODEL),
        out_shape=jax.ShapeDtypeStruct((n, D_MODEL), F32),
        scratch_shapes=[pltpu.VMEM((tm, D_FF), BF16)],
        compiler_params=_cparams(("arbitrary",)),
        name="ffn2",
    )(x1, att, gn, att_g, w_mix_b, norm_g, w_in_b, w_out_b, final_g)


def _small_slab(raw, bias):
    v = raw + bias
    t = jnp.log1p(jnp.exp(-jnp.abs(v)))
    lane = lax.broadcasted_iota(jnp.int32, v.shape, 1)
    logf = -(jnp.maximum(-v, 0.0) + t)
    dt = jnp.maximum(v, 0.0) + t
    return jnp.where(lane < DT_LANE0, logf, jnp.where(lane < 2 * DT_LANE0, dt, 0.0))


def _inproj_aug_body(x_ref, g_ref, wq_ref, wk_ref, wv_ref, wz_ref, wx_ref, ws_ref, bs_ref,
                     eq_ref, ek_ref, ev_ref,
                     k_ref, v_ref, qa_ref, ka_ref, va_ref, z_ref, xbc_ref, slab_ref,
                     carry_scr, *, tiles_per_seq, tm):
    i = pl.program_id(0)

    @pl.when(i % tiles_per_seq == 0)
    def _():
        carry_scr[...] = jnp.zeros_like(carry_scr)

    h = _rms(x_ref[...], g_ref[...]).astype(BF16)
    q = _dot(h, wq_ref[...])
    k = _dot(h, wk_ref[...])
    v = _dot(h, wv_ref[...])
    k_ref[0] = k.T
    v_ref[0] = v.T
    z_ref[...] = _dot(h, wz_ref[...])
    xbc_ref[...] = _dot(h, wx_ref[...])
    slab = _small_slab(_dot(h, ws_ref[...]), bs_ref[...])
    slab_ref[...] = slab

    lane = lax.broadcasted_iota(jnp.int32, slab.shape, 1)
    logf = jnp.where(lane < DT_LANE0, slab, 0.0)
    c = _dot3r(_tril(tm), logf) + carry_scr[0:1, :]
    carry_scr[0:1, :] = c[tm - 1:tm, :]
    c_hi = c.astype(BF16).astype(F32)
    r1 = c - c_hi
    c_mid = r1.astype(BF16).astype(F32)
    c_lo = (r1 - c_mid).astype(BF16).astype(F32)
    aug = (c_hi + pltpu.roll(c_mid, DT_LANE0, 1) + pltpu.roll(c_lo, 2 * DT_LANE0, 1)
           + jnp.where(lane == 3 * DT_LANE0, 1.0, 0.0)).astype(BF16)
    qs = (q * ATT_SCALE).astype(BF16)
    qa_ref[...] = (_dot(qs, eq_ref[:D_ATT, :]) + _dot(aug, eq_ref[D_ATT:, :])).astype(BF16)
    ka_ref[...] = (_dot(k.astype(BF16), ek_ref[:D_ATT, :]) + _dot(aug, ek_ref[D_ATT:, :])).astype(BF16)
    va_ref[...] = (_dot(v.astype(BF16), ev_ref[:D_ATT, :]) + _dot(aug, ev_ref[D_ATT:, :])).astype(BF16)


def _aug_scatter_mats():
    eq = np.zeros((D_ATT + LANES, N_ATT_HEADS * LANES), np.float32)
    ek = np.zeros((D_ATT + LANES, N_ATT_HEADS * LANES), np.float32)
    ev = np.zeros((D_ATT + LANES, N_ATT_HEADS * LANES), np.float32)
    one_lane = D_ATT + 3 * DT_LANE0
    for h in range(N_ATT_HEADS):
        v0, vsum = (0, HEAD_DIM) if h % 2 == 0 else (HEAD_DIM, 0)
        for d in range(HEAD_DIM):
            eq[h * HEAD_DIM + d, h * LANES + d] = 1.0
            ek[h * HEAD_DIM + d, h * LANES + d] = 1.0
            ev[h * HEAD_DIM + d, h * LANES + v0 + d] = 1.0
        ev[one_lane, h * LANES + vsum] = 1.0
        for p in range(3):
            eq[D_ATT + p * DT_LANE0 + h, h * LANES + AUG0 + p] = 1.0
            eq[one_lane, h * LANES + AUG0 + 3 + p] = 1.0
            ek[one_lane, h * LANES + AUG0 + p] = 1.0
            ek[D_ATT + p * DT_LANE0 + h, h * LANES + AUG0 + 3 + p] = -1.0
    return jnp.asarray(eq, BF16), jnp.asarray(ek, BF16), jnp.asarray(ev, BF16)


def _inproj_aug(x1, norm_g, ws, bias_s, *, tm, seq):
    n = x1.shape[0]
    eq, ek, ev = _aug_scatter_mats()
    row = lambda w: pl.BlockSpec((tm, w), lambda i: (i, 0))
    full = lambda a: pl.BlockSpec(a.shape, lambda i: (0, 0))
    wq, wk, wv, wz, wx, wsm = ws
    aug_w = N_ATT_HEADS * LANES
    tps = seq // tm
    kv_t = pl.BlockSpec((1, D_ATT, tm), lambda i: (i // tps, 0, i % tps))
    return pl.pallas_call(
        functools.partial(_inproj_aug_body, tiles_per_seq=tps, tm=tm),
        grid=(n // tm,),
        in_specs=[row(D_MODEL), full(norm_g), full(wq), full(wk), full(wv), full(wz), full(wx),
                  full(wsm), full(bias_s), full(eq), full(ek), full(ev)],
        out_specs=[kv_t, kv_t, row(aug_w), row(aug_w), row(aug_w),
                   row(D_SSM), row(CONV_DIM), row(LANES)],
        out_shape=[
            jax.ShapeDtypeStruct((n // seq, D_ATT, seq), F32), jax.ShapeDtypeStruct((n // seq, D_ATT, seq), F32),
            jax.ShapeDtypeStruct((n, aug_w), BF16), jax.ShapeDtypeStruct((n, aug_w), BF16),
            jax.ShapeDtypeStruct((n, aug_w), BF16), jax.ShapeDtypeStruct((n, D_SSM), F32),
            jax.ShapeDtypeStruct((n, CONV_DIM), F32), jax.ShapeDtypeStruct((n, LANES), F32)],
        scratch_shapes=[pltpu.VMEM((8, LANES), F32)],
        compiler_params=_cparams(("arbitrary",)),
        name="inproj_prompt",
    )(x1, norm_g, wq, wk, wv, wz, wx, wsm, bias_s, eq, ek, ev)


def _inproj_plain_body(x_ref, g_ref, wq_ref, wk_ref, wv_ref, wz_ref, wx_ref, ws_ref, bs_ref,
                       q_ref, k_ref, v_ref, z_ref, xbc_ref, slab_ref):
    h = _rms(x_ref[...], g_ref[...]).astype(BF16)
    q_ref[...] = _dot(h, wq_ref[...])
    k_ref[...] = _dot(h, wk_ref[...])
    v_ref[...] = _dot(h, wv_ref[...])
    z_ref[...] = _dot(h, wz_ref[...])
    xbc_ref[...] = _dot(h, wx_ref[...])
    slab_ref[...] = _small_slab(_dot(h, ws_ref[...]), bs_ref[...])


def _inproj_plain(x1, norm_g, ws, bias_s):
    n = x1.shape[0]
    wq, wk, wv, wz, wx, wsm = ws
    full = lambda a: pl.BlockSpec(a.shape, lambda i: (0, 0))
    out = lambda w: pl.BlockSpec((n, w), lambda i: (0, 0))
    return pl.pallas_call(
        _inproj_plain_body,
        grid=(1,),
        in_specs=[full(x1), full(norm_g), full(wq), full(wk), full(wv), full(wz), full(wx),
                  full(wsm), full(bias_s)],
        out_specs=[out(D_ATT), out(D_ATT), out(D_ATT), out(D_SSM), out(CONV_DIM), out(LANES)],
        out_shape=[jax.ShapeDtypeStruct((n, w), F32)
                   for w in (D_ATT, D_ATT, D_ATT, D_SSM, CONV_DIM, LANES)],
        compiler_params=_cparams(("arbitrary",)),
        name="inproj_sample",
    )(x1, norm_g, wq, wk, wv, wz, wx, wsm, bias_s)


def _attn_body(q_ref, k_ref, v_ref, o_ref, s_scr, m_scr, acc_scr, *, tq, nt):
    t = pl.program_id(2)
    nl = tq // LANES
    r = lax.broadcasted_iota(jnp.int32, (tq, tq), 0)
    c = lax.broadcasted_iota(jnp.int32, (tq, tq), 1)
    causal = c <= r
    m_scr[...] = jnp.full(m_scr.shape, NEG_BIG, F32)
    acc_scr[...] = jnp.zeros_like(acc_scr)

    def chunk_of(ci):
        first = ci <= t
        sel = jnp.where(first, 0, 1)
        q0 = pl.multiple_of(jnp.where(first, t, nt - 1 - t) * tq, tq)
        k0 = pl.multiple_of(jnp.where(first, ci, ci - t - 1) * tq, tq)
        return sel, q0, k0

    for ci in range(nt + 1):
        sel, q0, k0 = chunk_of(ci)
        for j in range(2):
            lanes = slice(j * LANES, (j + 1) * LANES)
            s = _dot_nt(q_ref[0, pl.ds(q0, tq), lanes], k_ref[0, pl.ds(k0, tq), lanes])
            if ci == nt:
                s = jnp.where(causal, s, NEG_BIG)
            elif ci < nt // 2:
                s = jnp.where(jnp.logical_or(causal, ci != t), s, NEG_BIG)
            s_scr[j, ci] = s
            m = m_scr[sel, j]
            for u in range(nl):
                m = jnp.maximum(m, s[:, u * LANES:(u + 1) * LANES])
            m_scr[sel, j] = m

    for x in range(2):
        for j in range(2):
            m_scr[x, j] = jnp.broadcast_to(jnp.max(m_scr[x, j], axis=1, keepdims=True), (tq, LANES))

    for ci in range(nt + 1):
        sel, q0, k0 = chunk_of(ci)
        for j in range(2):
            lanes = slice(j * LANES, (j + 1) * LANES)
            s = s_scr[j, ci]
            mrep = m_scr[sel, j]
            p = jnp.concatenate([jnp.exp(s[:, u * LANES:(u + 1) * LANES] - mrep) for u in range(nl)], axis=1)
            acc_scr[sel, j] += _dot(p.astype(BF16), v_ref[0, pl.ds(k0, tq), lanes])

    lane = lax.broadcasted_iota(jnp.int32, (tq, LANES), 1)
    for x, tile in enumerate((t, nt - 1 - t)):
        acc_e, acc_o = acc_scr[x, 0], acc_scr[x, 1]
        out = jnp.where(lane < HEAD_DIM, acc_e / acc_e[:, HEAD_DIM:HEAD_DIM + 1], acc_o / acc_o[:, 0:1])
        o_ref[0, pl.ds(pl.multiple_of(tile * tq, tq), tq), :] = out


def _attention(qa, ka, va, *, tq):
    b, L, _ = qa.shape
    nt = L // tq
    assert nt % 2 == 0
    pair = pl.BlockSpec((1, L, 2 * LANES), lambda bi, hp, t: (bi, 0, hp))
    return pl.pallas_call(
        functools.partial(_attn_body, tq=tq, nt=nt),
        grid=(b, N_ATT_HEADS // 2, nt // 2),
        in_specs=[pair, pair, pair],
        out_specs=pl.BlockSpec((1, L, LANES), lambda bi, hp, t: (bi, 0, hp)),
        out_shape=jax.ShapeDtypeStruct((b, L, D_ATT), F32),
        scratch_shapes=[pltpu.VMEM((2, nt + 1, tq, tq), F32), pltpu.VMEM((2, 2, tq, LANES), F32),
                        pltpu.VMEM((2, 2, tq, LANES), F32)],
        compiler_params=_cparams(("arbitrary", "arbitrary", "arbitrary")),
        name="fox_prompt",
    )(qa, ka, va)


def _head_expand_mat():
    e = np.zeros((LANES, D_SSM), np.float32)
    for h in range(N_SSM_HEADS):
        e[DT_LANE0 + h, h * SSM_HEAD_DIM:(h + 1) * SSM_HEAD_DIM] = 1.0
    return jnp.asarray(e, BF16)


def _gate_groupnorm(y, z, gain):
    g = y * _silu(z)
    half = D_SSM // N_SSM_GROUPS
    parts = []
    for gi in range(N_SSM_GROUPS):
        gg = g[:, gi * half:(gi + 1) * half]
        parts.append(gg * lax.rsqrt(jnp.mean(gg * gg, axis=-1, keepdims=True) + EPS))
    return jnp.concatenate(parts, axis=1) * gain


def _ssd_body(xbc_ref, slab_ref, z_ref, cw_ref, cb_ref, arow_ref, dsk_ref, sg_ref, e8_ref,
              gn_ref, st_ref, conv_scr, ht_scr, *, n_chunks):
    c = pl.program_id(1)
    Q = SSD_CHUNK

    @pl.when(c == 0)
    def _():
        conv_scr[0:8, :] = jnp.zeros((8, CONV_DIM), F32)
        ht_scr[...] = jnp.zeros_like(ht_scr)

    u = xbc_ref[0]
    conv_scr[8:8 + Q, :] = u
    conv = (cb_ref[...] + cw_ref[0:1, :] * conv_scr[5:5 + Q, :] + cw_ref[1:2, :] * conv_scr[6:6 + Q, :]
            + cw_ref[2:3, :] * conv_scr[7:7 + Q, :] + cw_ref[3:4, :] * u)
    conv_scr[0:8, :] = u[Q - 8:Q, :]
    xc = _silu(conv)
    xs = xc[:, :D_SSM]
    e8 = e8_ref[...]

    slab = slab_ref[0]
    lane = lax.broadcasted_iota(jnp.int32, slab.shape, 1)
    dtm = jnp.where((lane >= DT_LANE0) & (lane < 2 * DT_LANE0), slab, 0.0)
    a = dtm * arow_ref[...]
    a_c = _dot3r(_tril(Q), a)
    a_ct = a_c.T
    a_last = a_c[Q - 1:Q, :]
    ea_x = _dot3(jnp.exp(a_c), e8)
    dec_x = _dot3(jnp.exp(a_last - a_c), e8)
    xdt = xs * _dot3(dtm, e8)
    xdt_b = xdt.astype(BF16)
    xd_b = (xdt * dec_x).astype(BF16)
    cd_x = ea_x[Q - 1:Q, :]

    r_i = lax.broadcasted_iota(jnp.int32, (Q, Q), 0)
    c_i = lax.broadcasted_iota(jnp.int32, (Q, Q), 1)
    tri = c_i <= r_i
    lane_q = lax.broadcasted_iota(jnp.int32, (Q, LANES), 1)
    ht_old = ht_scr[...]
    gw = D_SSM // N_SSM_GROUPS
    hpg = N_SSM_HEADS // N_SSM_GROUPS
    y_parts, ht_parts = [], []
    for g in range(N_SSM_GROUPS):
        bm = xc[:, D_SSM + g * D_STATE:D_SSM + (g + 1) * D_STATE]
        cm = xc[:, D_SSM + (N_SSM_GROUPS + g) * D_STATE:D_SSM + (N_SSM_GROUPS + g + 1) * D_STATE]
        cb16 = cm.astype(BF16)
        cbm = _dot_nt(cb16, bm.astype(BF16))
        y_off = _dot(cb16, ht_old[:, g * gw:(g + 1) * gw].astype(BF16))
        y_diag = []
        for pr in range(hpg // 2):
            halves = []
            for hh in range(2):
                h = g * hpg + pr * 2 + hh
                col = a_c[:, DT_LANE0 + h:DT_LANE0 + h + 1]
                row = a_ct[DT_LANE0 + h:DT_LANE0 + h + 1, :]
                lm = jnp.where(tri, jnp.exp(col - row), 0.0)
                sc = (cbm * lm).astype(BF16)
                lo = g * gw + pr * LANES
                halves.append(_dot(sc, xdt_b[:, lo:lo + LANES]))
            y_diag.append(jnp.where(lane_q < SSM_HEAD_DIM, halves[0], halves[1]))
        y_parts.append(jnp.concatenate(y_diag, axis=1) + y_off * ea_x[:, g * gw:(g + 1) * gw])
        new = _dot(bm.T.astype(BF16), xd_b[:, g * gw:(g + 1) * gw])
        ht_parts.append(ht_old[:, g * gw:(g + 1) * gw] * cd_x[:, g * gw:(g + 1) * gw] + new)
    ht_new = jnp.concatenate(ht_parts, axis=1)
    ht_scr[...] = ht_new
    y = jnp.concatenate(y_parts, axis=1) + dsk_ref[...] * xs
    gn_ref[0] = _gate_groupnorm(y, z_ref[0], sg_ref[...]).astype(BF16)

    @pl.when(c == n_chunks - 1)
    def _():
        st_ref[0] = ht_new.T


def _ssd_prompt(xbc, slab, z, conv_w, conv_b, a_row, dsk_x, ssm_g):
    b, L, _ = xbc.shape
    nc = L // SSD_CHUNK
    e8 = _head_expand_mat()
    blk = lambda w: pl.BlockSpec((1, SSD_CHUNK, w), lambda bi, ci: (bi, ci, 0))
    full = lambda a: pl.BlockSpec(a.shape, lambda bi, ci: (0, 0))
    return pl.pallas_call(
        functools.partial(_ssd_body, n_chunks=nc),
        grid=(b, nc),
        in_specs=[blk(CONV_DIM), blk(LANES), blk(D_SSM), full(conv_w), full(conv_b), full(a_row),
                  full(dsk_x), full(ssm_g), full(e8)],
        out_specs=[blk(D_SSM), pl.BlockSpec((1, D_SSM, D_STATE), lambda bi, ci: (bi, 0, 0))],
        out_shape=[jax.ShapeDtypeStruct((b, L, D_SSM), BF16),
                   jax.ShapeDtypeStruct((b, D_SSM, D_STATE), F32)],
        scratch_shapes=[pltpu.VMEM((8 + SSD_CHUNK, CONV_DIM), F32), pltpu.VMEM((D_STATE, D_SSM), F32)],
        compiler_params=_cparams(("arbitrary", "arbitrary")),
        name="ssd_prompt",
    )(xbc, slab, z, conv_w, conv_b, a_row, dsk_x, ssm_g, e8)


def _ssd_step_body(xbc_ref, sc_ref, slab_ref, z_ref, st_ref, cw_ref, cb_ref, arow_ref, dsk_ref,
                   sg_ref, e8_ref, gn_ref, so_ref, xc_scr, dtx_scr, decx_scr):
    b = pl.program_id(0)

    @pl.when(b == 0)
    def _():
        conv = (cb_ref[...] + cw_ref[0:1, :] * sc_ref[0] + cw_ref[1:2, :] * sc_ref[1]
                + cw_ref[2:3, :] * sc_ref[2] + cw_ref[3:4, :] * xbc_ref[...])
        xc_scr[...] = _silu(conv)
        slab = slab_ref[...]
        lane = lax.broadcasted_iota(jnp.int32, slab.shape, 1)
        dtm = jnp.where((lane >= DT_LANE0) & (lane < 2 * DT_LANE0), slab, 0.0)
        e8 = e8_ref[...]
        dtx_scr[...] = _dot3(dtm, e8)
        decx_scr[...] = _dot3(jnp.exp(dtm * arow_ref[...]), e8)

    xrow = xc_scr[pl.ds(b, 1), :]
    xs = xrow[:, :D_SSM]
    xdt = xs * dtx_scr[pl.ds(b, 1), :]
    dec = decx_scr[pl.ds(b, 1), :]
    gw = D_SSM // N_SSM_GROUPS
    y_rows = []
    for j in range(D_SSM // LANES):
        g = (j * LANES) // gw
        bm = xrow[:, D_SSM + g * D_STATE:D_SSM + (g + 1) * D_STATE]
        cm = xrow[:, D_SSM + (N_SSM_GROUPS + g) * D_STATE:D_SSM + (N_SSM_GROUPS + g + 1) * D_STATE]
        xcol = _row_to_col(xdt[:, j * LANES:(j + 1) * LANES])
        dcol = _row_to_col(dec[:, j * LANES:(j + 1) * LANES])
        hs = dcol * st_ref[0, j * LANES:(j + 1) * LANES, :] + xcol * bm
        so_ref[0, j * LANES:(j + 1) * LANES, :] = hs
        y_rows.append(_col_to_row(jnp.sum(hs * cm, axis=1, keepdims=True)))
    y = jnp.concatenate(y_rows, axis=1) + dsk_ref[...] * xs
    gn_ref[0] = _gate_groupnorm(y, z_ref[pl.ds(b, 1), :], sg_ref[...]).astype(BF16)


def _ssd_step(xbc, sconv_t, slab, z, state, conv_w, conv_b, a_row, dsk_x, ssm_g):
    nb = xbc.shape[0]
    e8 = _head_expand_mat()
    full2 = lambda a: pl.BlockSpec(a.shape, lambda bi: (0,) * a.ndim)
    gn, st = pl.pallas_call(
        _ssd_step_body,
        grid=(nb,),
        in_specs=[full2(xbc), full2(sconv_t), full2(slab), full2(z),
                  pl.BlockSpec((1, D_SSM, D_STATE), lambda bi: (bi, 0, 0)),
                  full2(conv_w), full2(conv_b), full2(a_row), full2(dsk_x), full2(ssm_g), full2(e8)],
        out_specs=[pl.BlockSpec((1, 1, D_SSM), lambda bi: (bi, 0, 0)),
                   pl.BlockSpec((1, D_SSM, D_STATE), lambda bi: (bi, 0, 0))],
        out_shape=[jax.ShapeDtypeStruct((nb, 1, D_SSM), BF16),
                   jax.ShapeDtypeStruct((nb, D_SSM, D_STATE), F32)],
        scratch_shapes=[pltpu.VMEM((nb, CONV_DIM), F32), pltpu.VMEM((nb, D_SSM), F32),
                        pltpu.VMEM((nb, D_SSM), F32)],
        compiler_params=_cparams(("arbitrary",)),
        name="ssd_step",
    )(xbc, sconv_t, slab, z, state, conv_w, conv_b, a_row, dsk_x, ssm_g, e8)
    return gn.reshape(nb, D_SSM), st


def _decode_body(pt_ref, q_ref, k2_ref, v2_ref, lf2_ref, *refs, pps, n_groups):
    k_refs = refs[:pps]
    v_refs = refs[pps:2 * pps]
    lf_refs = refs[2 * pps:3 * pps]
    o_ref = refs[3 * pps]
    qblk_scr, m_scr, l_scr, acc_scr, carry_scr = refs[3 * pps + 1:]
    g = pl.program_id(1)
    H, HD, P = N_ATT_HEADS, HEAD_DIM, PAGE_SIZE
    r8 = lax.broadcasted_iota(jnp.int32, (H, D_ATT), 0)
    c8 = lax.broadcasted_iota(jnp.int32, (H, D_ATT), 1)
    own_head = c8 // HD == r8

    @pl.when(g == 0)
    def _():
        qs = (q_ref[0] * ATT_SCALE).astype(BF16).astype(F32)
        qblk = jnp.where(own_head, qs, 0.0)
        qblk_scr[...] = qblk.astype(BF16)
        k2 = k2_ref[0].astype(BF16).astype(F32)
        m_scr[...] = jnp.broadcast_to(jnp.sum(qblk * k2, axis=1, keepdims=True), m_scr.shape)
        l_scr[...] = jnp.ones_like(l_scr)
        acc_scr[...] = jnp.broadcast_to(v2_ref[0].astype(BF16).astype(F32), acc_scr.shape)
        carry_scr[...] = jnp.broadcast_to(_row_to_col(lf2_ref[0])[0:H, :], carry_scr.shape)

    kcat = jnp.concatenate([kr[0].astype(BF16) for kr in k_refs], axis=1)
    s = _dot(qblk_scr[...], kcat)
    x = jnp.concatenate([lr[0] for lr in lf_refs], axis=0)
    rr = lax.broadcasted_iota(jnp.int32, (P, P), 0)
    cc = lax.broadcasted_iota(jnp.int32, (P, P), 1)
    later = (rr > cc).astype(BF16)
    rev_local = _dot3(x, later)
    tot = jnp.sum(x, axis=1, keepdims=True)
    carry = carry_scr[:, 0:1]
    s_pages = [None] * pps
    for i in reversed(range(pps)):
        s_pages[i] = s[:, i * P:(i + 1) * P] + rev_local[i * H:(i + 1) * H, :] + carry
        carry = carry + tot[i * H:(i + 1) * H, :]
    carry_scr[...] = jnp.broadcast_to(carry, carry_scr.shape)
    st = jnp.concatenate(s_pages, axis=1)

    m_old = m_scr[:, 0:1]
    m_new = jnp.maximum(m_old, jnp.max(st, axis=1, keepdims=True))
    alpha = jnp.exp(m_old - m_new)
    p = jnp.exp(st - m_new)
    l_new = alpha * l_scr[:, 0:1] + jnp.sum(p, axis=1, keepdims=True)
    vcat = jnp.concatenate([vr[0].astype(BF16) for vr in v_refs], axis=1)
    acc = alpha * acc_scr[...] + _dot_nt(p.astype(BF16), vcat)
    m_scr[...] = jnp.broadcast_to(m_new, m_scr.shape)
    l_scr[...] = jnp.broadcast_to(l_new, l_scr.shape)
    acc_scr[...] = acc

    @pl.when(g == n_groups - 1)
    def _():
        o_ref[0] = jnp.sum(jnp.where(own_head, acc / l_new, 0.0), axis=0, keepdims=True)


def _decode_attention(q, k2, v2, slab2, cache_k, cache_v, cache_logf, page_table, *, pps):
    nb, n_pages = page_table.shape
    ng = n_pages // pps
    row3 = lambda a: a.reshape(nb, 1, a.shape[-1])
    rowspec = lambda w: pl.BlockSpec((1, 1, w), lambda b, g, pt: (b, 0, 0))

    def page_spec(rows, i):
        return pl.BlockSpec((1, rows, PAGE_SIZE),
                            lambda b, g, pt, i=i: (pt[b, (ng - 1 - g) * pps + i], 0, 0))

    in_specs = ([rowspec(D_ATT), rowspec(D_ATT), rowspec(D_ATT), rowspec(LANES)]
                + [page_spec(D_ATT, i) for i in range(pps)]
                + [page_spec(D_ATT, i) for i in range(pps)]
                + [page_spec(N_ATT_HEADS, i) for i in range(pps)])
    out = pl.pallas_call(
        functools.partial(_decode_body, pps=pps, n_groups=ng),
        grid_spec=pltpu.PrefetchScalarGridSpec(
            num_scalar_prefetch=1,
            grid=(nb, ng),
            in_specs=in_specs,
            out_specs=pl.BlockSpec((1, 1, D_ATT), lambda b, g, pt: (b, 0, 0)),
            scratch_shapes=[pltpu.VMEM((N_ATT_HEADS, D_ATT), BF16), pltpu.VMEM((8, LANES), F32),
                            pltpu.VMEM((8, LANES), F32), pltpu.VMEM((8, D_ATT), F32),
                            pltpu.VMEM((8, LANES), F32)]),
        out_shape=jax.ShapeDtypeStruct((nb, 1, D_ATT), F32),
        compiler_params=_cparams(("arbitrary", "arbitrary")),
        name="fox_decode",
    )(page_table, row3(q), row3(k2), row3(v2), row3(slab2),
      *([cache_k] * pps), *([cache_v] * pps), *([cache_logf] * pps))
    return out.reshape(nb, D_ATT)


DECODE_GROUP = 16
DECODE_SLOTS = 3


def _fox_body(pt_ref, qa_ref, ka_ref, va_ref, qs_ref, ks_ref, vs_ref, lfs_ref, ck_hbm, cv_hbm, clf_hbm,
              o_ref, os_ref,
              s_scr, m_scr, acc_scr, kbuf, vbuf, lbuf, sems, qblk_scr, dm_scr, dl_scr, dacc_scr, carry_scr,
              *, tq, nt, gps, gpseq, n_groups):
    G, NS = DECODE_GROUP, DECODE_SLOTS
    H, HD, P = N_ATT_HEADS, HEAD_DIM, PAGE_SIZE
    t = pl.program_id(2)
    sid = (pl.program_id(0) * pl.num_programs(1) + pl.program_id(1)) * pl.num_programs(2) + t
    nl = tq // LANES

    def group_copies(gg):
        slot = gg % NS
        src = jnp.minimum(gg, n_groups - 1)
        seq = src // gpseq
        base = (gpseq - 1 - src % gpseq) * G
        copies = []
        for i in range(G):
            page = pt_ref[seq, base + i]
            copies.append(pltpu.make_async_copy(ck_hbm.at[page], kbuf.at[slot, i], sems.at[slot, 0]))
            copies.append(pltpu.make_async_copy(cv_hbm.at[page], vbuf.at[slot, i], sems.at[slot, 1]))
            copies.append(pltpu.make_async_copy(clf_hbm.at[page], lbuf.at[slot, i], sems.at[slot, 2]))
        return copies

    def start_group(gg):
        for n, cp in enumerate(group_copies(gg)):
            cp.start(priority=1 if n % 3 == 1 else 0)

    def wait_group(gg):
        for cp in group_copies(gg):
            cp.wait()

    r8 = lax.broadcasted_iota(jnp.int32, (H, D_ATT), 0)
    c8 = lax.broadcasted_iota(jnp.int32, (H, D_ATT), 1)
    own_head = c8 // HD == r8

    def decode_init():
        qs = (qs_ref[0] * ATT_SCALE).astype(BF16).astype(F32)
        qblk = jnp.where(own_head, qs, 0.0)
        qblk_scr[...] = qblk.astype(BF16)
        k2 = ks_ref[0].astype(BF16).astype(F32)
        dm_scr[...] = jnp.broadcast_to(jnp.sum(qblk * k2, axis=1, keepdims=True), dm_scr.shape)
        dl_scr[...] = jnp.ones_like(dl_scr)
        dacc_scr[...] = jnp.broadcast_to(vs_ref[0].astype(BF16).astype(F32), dacc_scr.shape)
        carry_scr[...] = jnp.broadcast_to(_row_to_col(lfs_ref[0])[0:H, :], carry_scr.shape)

    def decode_group(slot):
        kcat = jnp.concatenate([kbuf[slot, i].astype(BF16) for i in range(G)], axis=1)
        s = _dot(qblk_scr[...], kcat)
        x = jnp.concatenate([lbuf[slot, i] for i in range(G)], axis=0)
        rr = lax.broadcasted_iota(jnp.int32, (P, P), 0)
        cc = lax.broadcasted_iota(jnp.int32, (P, P), 1)
        rev_local = _dot3(x, (rr > cc).astype(BF16))
        tot = jnp.sum(x, axis=1, keepdims=True)
        carry = carry_scr[:, 0:1]
        s_pages = [None] * G
        for i in reversed(range(G)):
            s_pages[i] = s[:, i * P:(i + 1) * P] + rev_local[i * H:(i + 1) * H, :] + carry
            carry = carry + tot[i * H:(i + 1) * H, :]
        carry_scr[...] = jnp.broadcast_to(carry, carry_scr.shape)
        st = jnp.concatenate(s_pages, axis=1)
        m_old = dm_scr[:, 0:1]
        m_new = jnp.maximum(m_old, jnp.max(st, axis=1, keepdims=True))
        alpha = jnp.exp(m_old - m_new)
        p = jnp.exp(st - m_new)
        l_new = alpha * dl_scr[:, 0:1] + jnp.sum(p, axis=1, keepdims=True)
        vcat = jnp.concatenate([vbuf[slot, i].astype(BF16) for i in range(G)], axis=1)
        dacc_scr[...] = alpha * dacc_scr[...] + _dot_nt(p.astype(BF16), vcat)
        dm_scr[...] = jnp.broadcast_to(m_new, dm_scr.shape)
        dl_scr[...] = jnp.broadcast_to(l_new, dl_scr.shape)

    def decode_finish():
        o = dacc_scr[...] / dl_scr[:, 0:1]
        os_ref[0] = jnp.sum(jnp.where(own_head, o, 0.0), axis=0, keepdims=True)

    r = lax.broadcasted_iota(jnp.int32, (tq, tq), 0)
    c = lax.broadcasted_iota(jnp.int32, (tq, tq), 1)
    causal = c <= r

    def chunk_of(ci):
        first = ci <= t
        sel = jnp.where(first, 0, 1)
        q0 = pl.multiple_of(jnp.where(first, t, nt - 1 - t) * tq, tq)
        k0 = pl.multiple_of(jnp.where(first, ci, ci - t - 1) * tq, tq)
        return sel, q0, k0

    def scores(ci):
        sel, q0, k0 = chunk_of(ci)
        for j in range(2):
            lanes = slice(j * LANES, (j + 1) * LANES)
            s = _dot_nt(qa_ref[0, pl.ds(q0, tq), lanes], ka_ref[0, pl.ds(k0, tq), lanes])
            if ci == nt:
                s = jnp.where(causal, s, NEG_BIG)
            elif ci < nt // 2:
                s = jnp.where(jnp.logical_or(causal, ci != t), s, NEG_BIG)
            s_scr[j, ci] = s
            m = m_scr[sel, j]
            for u in range(nl):
                m = jnp.maximum(m, s[:, u * LANES:(u + 1) * LANES])
            m_scr[sel, j] = m

    def row_max():
        for x in range(2):
            for j in range(2):
                m_scr[x, j] = jnp.broadcast_to(jnp.max(m_scr[x, j], axis=1, keepdims=True), (tq, LANES))

    def weighted(ci):
        sel, q0, k0 = chunk_of(ci)
        for j in range(2):
            lanes = slice(j * LANES, (j + 1) * LANES)
            s = s_scr[j, ci]
            mrep = m_scr[sel, j]
            p = jnp.concatenate([jnp.exp(s[:, u * LANES:(u + 1) * LANES] - mrep) for u in range(nl)], axis=1)
            acc_scr[sel, j] += _dot(p.astype(BF16), va_ref[0, pl.ds(k0, tq), lanes])

    def write_out():
        lane = lax.broadcasted_iota(jnp.int32, (tq, LANES), 1)
        for x, tile in enumerate((t, nt - 1 - t)):
            acc_e, acc_o = acc_scr[x, 0], acc_scr[x, 1]
            out = jnp.where(lane < HEAD_DIM, acc_e / acc_e[:, HEAD_DIM:HEAD_DIM + 1], acc_o / acc_o[:, 0:1])
            o_ref[0, pl.ds(pl.multiple_of(tile * tq, tq), tq), :] = out

    items = ([functools.partial(scores, ci) for ci in range(nt + 1)] + [row_max]
             + [functools.partial(weighted, ci) for ci in range(nt + 1)] + [write_out])
    per_part = -(-len(items) // gps)
    parts = [items[i * per_part:(i + 1) * per_part] for i in range(gps)]

    @pl.when(sid == 0)
    def _():
        for g0 in range(NS - 1):
            start_group(g0)

    pl.when((sid * gps) % gpseq == 0)(decode_init)
    m_scr[...] = jnp.full(m_scr.shape, NEG_BIG, F32)
    acc_scr[...] = jnp.zeros_like(acc_scr)
    for gi in range(gps):
        gg = sid * gps + gi
        wait_group(gg)
        start_group(gg + (NS - 1))
        for item in parts[gi]:
            item()
        decode_group(gg % NS)
    pl.when((sid * gps + gps - 1) % gpseq == gpseq - 1)(decode_finish)

    @pl.when(sid == pl.num_programs(0) * pl.num_programs(1) * pl.num_programs(2) - 1)
    def _():
        for extra in range(NS - 1):
            wait_group(n_groups + extra)


def _fox_attention(qa, ka, va, q_s, k_s, v_s, slab_s, cache_k, cache_v, cache_logf, page_table, *, tq):
    b, L, _ = qa.shape
    nt = L // tq
    nb, n_pages = page_table.shape
    n_steps = b * (N_ATT_HEADS // 2) * (nt // 2)
    n_groups = nb * n_pages // DECODE_GROUP
    gps = n_groups // n_steps
    gpseq = n_pages // DECODE_GROUP
    assert nt % 2 == 0 and n_pages % DECODE_GROUP == 0
    assert gps * n_steps == n_groups and gpseq % gps == 0 and n_groups >= DECODE_SLOTS
    spq = gpseq // gps
    hp_n, t_n = N_ATT_HEADS // 2, nt // 2
    seq_of = lambda bi, hp, t: ((bi * hp_n + hp) * t_n + t) // spq
    pair = pl.BlockSpec((1, L, 2 * LANES), lambda bi, hp, t, pt: (bi, 0, hp), pipeline_mode=pl.Buffered(1))
    srow = lambda w: pl.BlockSpec((1, 1, w), lambda bi, hp, t, pt: (seq_of(bi, hp, t), 0, 0))
    anyspec = pl.BlockSpec(memory_space=pl.ANY)
    row3 = lambda a: a.reshape(nb, 1, a.shape[-1])
    G, NS = DECODE_GROUP, DECODE_SLOTS
    att, att_s = pl.pallas_call(
        functools.partial(_fox_body, tq=tq, nt=nt, gps=gps, gpseq=gpseq, n_groups=n_groups),
        grid_spec=pltpu.PrefetchScalarGridSpec(
            num_scalar_prefetch=1,
            grid=(b, hp_n, t_n),
            in_specs=[pair, pair, pair, srow(D_ATT), srow(D_ATT), srow(D_ATT), srow(LANES),
                      anyspec, anyspec, anyspec],
            out_specs=[pl.BlockSpec((1, L, LANES), lambda bi, hp, t, pt: (bi, 0, hp)),
                       pl.BlockSpec((1, 1, D_ATT), lambda bi, hp, t, pt: (seq_of(bi, hp, t), 0, 0))],
            scratch_shapes=[
                pltpu.VMEM((2, nt + 1, tq, tq), F32), pltpu.VMEM((2, 2, tq, LANES), F32),
                pltpu.VMEM((2, 2, tq, LANES), F32),
                pltpu.VMEM((NS, G, D_ATT, PAGE_SIZE), F32), pltpu.VMEM((NS, G, D_ATT, PAGE_SIZE), F32),
                pltpu.VMEM((NS, G, N_ATT_HEADS, PAGE_SIZE), F32), pltpu.SemaphoreType.DMA((NS, 3)),
                pltpu.VMEM((N_ATT_HEADS, D_ATT), BF16), pltpu.VMEM((8, LANES), F32),
                pltpu.VMEM((8, LANES), F32), pltpu.VMEM((8, D_ATT), F32), pltpu.VMEM((8, LANES), F32)]),
        out_shape=[jax.ShapeDtypeStruct((b, L, D_ATT), F32), jax.ShapeDtypeStruct((nb, 1, D_ATT), F32)],
        compiler_params=_cparams(("arbitrary", "arbitrary", "arbitrary")),
        name="fox_attention",
    )(page_table, qa, ka, va, row3(q_s), row3(k_s), row3(v_s), row3(slab_s), cache_k, cache_v, cache_logf)
    return att, att_s.reshape(nb, D_ATT)


def _lane_pad(vec, lane0):
    out = jnp.zeros((1, LANES), F32)
    return out.at[0, lane0:lane0 + vec.shape[0]].set(vec.astype(F32))


def kernel(x_prompt, x_sample, cache_k, cache_v, cache_logf, page_table, state_conv, state_ssm,
           ffn1_norm, w_ffn1_in, w_ffn1_out, mix_norm, w_in, b_f, conv_w, conv_b, dt_bias, a_log,
           d_skip, att_out_norm, ssm_out_norm, w_out, ffn2_norm, w_ffn2_in, w_ffn2_out, final_norm):
    depth = w_in.shape[0]
    assert depth == 1
    B, L, _ = x_prompt.shape
    nb, T, _ = x_sample.shape
    assert T == 1
    H, HD = N_ATT_HEADS, HEAD_DIM

    l0 = 0
    w1i, w1o = w_ffn1_in[l0].astype(BF16), w_ffn1_out[l0].astype(BF16)
    w2i, w2o = w_ffn2_in[l0].astype(BF16), w_ffn2_out[l0].astype(BF16)
    wi = w_in[l0]
    o_f = 3 * D_ATT
    o_z = o_f + H
    o_x = o_z + D_SSM
    o_dt = o_x + CONV_DIM
    w_small = jnp.zeros((D_MODEL, LANES), F32)
    w_small = w_small.at[:, 0:H].set(wi[:, o_f:o_f + H])
    w_small = w_small.at[:, DT_LANE0:DT_LANE0 + N_SSM_HEADS].set(wi[:, o_dt:o_dt + N_SSM_HEADS])
    ws = tuple(w.astype(BF16) for w in (wi[:, 0:D_ATT], wi[:, D_ATT:2 * D_ATT], wi[:, 2 * D_ATT:3 * D_ATT],
                                        wi[:, o_z:o_z + D_SSM], wi[:, o_x:o_x + CONV_DIM], w_small))
    bias_s = _lane_pad(b_f[l0], 0) + _lane_pad(dt_bias[l0], DT_LANE0)
    a_row = _lane_pad(-jnp.exp(a_log[l0].astype(F32)), DT_LANE0)
    dsk_x = jnp.repeat(d_skip[l0].astype(F32), SSM_HEAD_DIM)[None, :]
    row = lambda v: v.astype(F32)[None, :]
    g1, gm, g2, gf = row(ffn1_norm[l0]), row(mix_norm[l0]), row(ffn2_norm[l0]), row(final_norm)
    ga, gs = row(att_out_norm[l0]), row(ssm_out_norm[l0])
    cw, cb = conv_w[l0].astype(F32), row(conv_b[l0])
    wm = w_out[l0].astype(BF16)

    TM, TF = FFN_ROWS, FFN_CHUNK
    xp = x_prompt.reshape(B * L, D_MODEL)
    x1 = _ffn(xp, g1, w1i, w1o, tm=TM, tf=TF)
    k_p, v_p, qa, ka, va, z_p, xbc_p, slab_p = _inproj_aug(x1, gm, ws, bias_s, tm=INPROJ_ROWS, seq=L)
    xs = x_sample.reshape(nb, D_MODEL)
    x1s = _ffn(xs, g1, w1i, w1o, tm=nb, tf=TF)
    q_s, k_s, v_s, z_s, xbc_s, slab_s = _inproj_plain(x1s, gm, ws, bias_s)

    n_pool = cache_k.shape[1]
    ck = jnp.transpose(cache_k[l0], (0, 2, 3, 1)).reshape(n_pool, D_ATT, PAGE_SIZE)
    cv = jnp.transpose(cache_v[l0], (0, 2, 3, 1)).reshape(n_pool, D_ATT, PAGE_SIZE)
    clf = jnp.transpose(cache_logf[l0], (0, 2, 1))
    att, att_s = _fox_attention(qa.reshape(B, L, -1), ka.reshape(B, L, -1), va.reshape(B, L, -1),
                                q_s, k_s, v_s, slab_s, ck, cv, clf, page_table, tq=ATT_ROWS)

    gn_p, st_p = _ssd_prompt(xbc_p.reshape(B, L, -1), slab_p.reshape(B, L, -1), z_p.reshape(B, L, -1),
                             cw, cb, a_row, dsk_x, gs)
    y_p = _ffn2(x1, att.reshape(B * L, D_ATT), gn_p.reshape(B * L, D_SSM), ga, wm, g2, w2i, w2o, gf,
                tm=TM, tf=TF)

    sconv_t = jnp.swapaxes(state_conv[l0], 0, 1)
    gn_s, st_s = _ssd_step(xbc_s, sconv_t, slab_s, z_s,
                           state_ssm[l0].reshape(nb, D_SSM, D_STATE), cw, cb, a_row, dsk_x, gs)
    y_s = _ffn2(x1s, att_s, gn_s, ga, wm, g2, w2i, w2o, gf, tm=nb, tf=TF)

    xbc_p3 = xbc_p.reshape(B, L, CONV_DIM)
    return (
        y_p.reshape(B, L, D_MODEL),
        y_s.reshape(nb, 1, D_MODEL),
        jnp.transpose(k_p.reshape(1, B, H, HD, L), (0, 1, 4, 2, 3)),
        jnp.transpose(v_p.reshape(1, B, H, HD, L), (0, 1, 4, 2, 3)),
        slab_p[:, :H].reshape(1, B, L, H),
        xbc_p3[:, L - (CONV_W - 1):, :][None],
        st_p.reshape(1, B, N_SSM_HEADS, SSM_HEAD_DIM, D_STATE),
        k_s.reshape(1, nb, 1, H, HD),
        v_s.reshape(1, nb, 1, H, HD),
        slab_s[:, :H].reshape(1, nb, 1, H),
        jnp.concatenate([state_conv[l0][:, 1:, :], xbc_s[:, None, :]], axis=1)[None],
        st_s.reshape(1, nb, N_SSM_HEADS, SSM_HEAD_DIM, D_STATE),
    )
```

```python
import functools

import numpy as np
import jax
import jax.numpy as jnp
from jax import lax
from jax.experimental import pallas as pl
from jax.experimental.pallas import tpu as pltpu

F32 = jnp.float32
BF16 = jnp.bfloat16

D_MODEL = 1024
D_ATT = 512
D_SSM = 512
HEAD_DIM = 64
N_ATT_HEADS = 8
N_SSM_HEADS = 8
SSM_HEAD_DIM = 64
N_SSM_GROUPS = 2
D_STATE = 128
CONV_W = 4
CONV_DIM = D_SSM + 2 * N_SSM_GROUPS * D_STATE
SSD_CHUNK = 128
PAGE_SIZE = 128
D_FF = 2816
FFN_RESIDUAL = 0.5
EPS = 1e-6
ATT_SCALE = HEAD_DIM ** -0.5
LANES = 128
DT_LANE0 = 8
NEG_BIG = -1e30

VMEM_LIMIT = 56 * 1024 * 1024
FFN_ROWS = 1024
FFN_CHUNK = 256
INPROJ_ROWS = 512
ATT_ROWS = 256
SSD_CHUNKS_PER_STEP = 2


def _cparams(sem):
    return pltpu.CompilerParams(dimension_semantics=sem, vmem_limit_bytes=VMEM_LIMIT)


def _dot(a, b):
    return jnp.dot(a, b, preferred_element_type=F32)


def _dot_nt(a, b):
    return lax.dot_general(a, b, (((1,), (1,)), ((), ())), preferred_element_type=F32)


def _split3(x):
    hi = x.astype(BF16)
    r1 = x - hi.astype(F32)
    mid = r1.astype(BF16)
    lo = (r1 - mid.astype(F32)).astype(BF16)
    return hi, mid, lo


def _dot3(x, m):
    hi, mid, lo = _split3(x)
    return _dot(hi, m) + _dot(mid, m) + _dot(lo, m)


def _dot3r(m, x):
    hi, mid, lo = _split3(x)
    return _dot(m, hi) + _dot(m, mid) + _dot(m, lo)


def _rms(x, g):
    return x * lax.rsqrt(jnp.mean(x * x, axis=-1, keepdims=True) + EPS) * g


def _silu(x):
    return x / (1.0 + jnp.exp(-x))


def _tril(n, dtype=BF16):
    r = lax.broadcasted_iota(jnp.int32, (n, n), 0)
    c = lax.broadcasted_iota(jnp.int32, (n, n), 1)
    return (c <= r).astype(dtype)


def _row_to_col(row):
    n = row.shape[1]
    r = lax.broadcasted_iota(jnp.int32, (n, n), 0)
    c = lax.broadcasted_iota(jnp.int32, (n, n), 1)
    return jnp.sum(jnp.where(r == c, jnp.broadcast_to(row, (n, n)), 0.0), axis=1, keepdims=True)


def _col_to_row(col):
    n = col.shape[0]
    r = lax.broadcasted_iota(jnp.int32, (n, n), 0)
    c = lax.broadcasted_iota(jnp.int32, (n, n), 1)
    return jnp.sum(jnp.where(r == c, jnp.broadcast_to(col, (n, n)), 0.0), axis=0, keepdims=True)


def _resident(shape):
    return pl.BlockSpec(shape, lambda i: (0,) * len(shape), pipeline_mode=pl.Buffered(1))


def _swiglu_half(x, g_ref, wi_ref, wo_ref, a_scr, tf):
    h = _rms(x, g_ref[...]).astype(BF16)
    for c in range(D_FF // tf):
        gate = _dot(h, wi_ref[:, c * tf:(c + 1) * tf])
        up = _dot(h, wi_ref[:, D_FF + c * tf:D_FF + (c + 1) * tf])
        a_scr[:, c * tf:(c + 1) * tf] = (_silu(gate) * up).astype(BF16)
    return x + FFN_RESIDUAL * _dot(a_scr[...], wo_ref[...])


def _ffn_body(x_ref, g_ref, wi_ref, wo_ref, o_ref, a_scr, *, tf):
    o_ref[...] = _swiglu_half(x_ref[...], g_ref, wi_ref, wo_ref, a_scr, tf)


def _ffn(x, norm_g, w_in_b, w_out_b, *, tm, tf):
    n = x.shape[0]
    row = lambda w: pl.BlockSpec((tm, w), lambda i: (i, 0))
    return pl.pallas_call(
        functools.partial(_ffn_body, tf=tf),
        grid=(n // tm,),
        in_specs=[row(D_MODEL), _resident(norm_g.shape), _resident(w_in_b.shape), _resident(w_out_b.shape)],
        out_specs=row(D_MODEL),
        out_shape=jax.ShapeDtypeStruct((n, D_MODEL), F32),
        scratch_shapes=[pltpu.VMEM((tm, D_FF), BF16)],
        compiler_params=_cparams(("arbitrary",)),
        name="ffn1",
    )(x, norm_g, w_in_b, w_out_b)


def _ffn2_body(x_ref, att_ref, gn_ref, ag_ref, wm_ref, g_ref, wi_ref, wo_ref, fg_ref, o_ref, a_scr, *, tf):
    an = _rms(att_ref[...], ag_ref[...]).astype(BF16)
    x2 = x_ref[...] + _dot(an, wm_ref[:D_ATT, :]) + _dot(gn_ref[...], wm_ref[D_ATT:, :])
    o_ref[...] = _rms(_swiglu_half(x2, g_ref, wi_ref, wo_ref, a_scr, tf), fg_ref[...])


def _ffn2(x1, att, gn, att_g, w_mix_b, norm_g, w_in_b, w_out_b, final_g, *, tm, tf):
    n = x1.shape[0]
    row = lambda w: pl.BlockSpec((tm, w), lambda i: (i, 0))
    return pl.pallas_call(
        functools.partial(_ffn2_body, tf=tf),
        grid=(n // tm,),
        in_specs=[row(D_MODEL), row(D_ATT), row(D_SSM), _resident(att_g.shape), _resident(w_mix_b.shape),
                  _resident(norm_g.shape), _resident(w_in_b.shape), _resident(w_out_b.shape),
                  _resident(final_g.shape)],
        out_specs=row(D_MODEL),
        out_shape=jax.ShapeDtypeStruct((n, D_MODEL), F32),
        scratch_shapes=[pltpu.VMEM((tm, D_FF), BF16)],
        compiler_params=_cparams(("arbitrary",)),
        name="ffn2",
    )(x1, att, gn, att_g, w_mix_b, norm_g, w_in_b, w_out_b, final_g)


def _small_slab(raw, bias):
    v = raw + bias
    t = jnp.log1p(jnp.exp(-jnp.abs(v)))
    lane = lax.broadcasted_iota(jnp.int32, v.shape, 1)
    logf = -(jnp.maximum(-v, 0.0) + t)
    dt = jnp.maximum(v, 0.0) + t
    return jnp.where(lane < DT_LANE0, logf, jnp.where(lane < 2 * DT_LANE0, dt, 0.0))


def _inproj_aug_body(x_ref, g_ref, wq_ref, wk_ref, wv_ref, wz_ref, wx_ref, ws_ref, bs_ref,
                     eq_ref, ek_ref, cw_ref, cb_ref,
                     k_ref, v_ref, q16_ref, k16_ref, v16_ref, aq_ref, ak_ref, z_ref, xc_ref, tail_ref, slab_ref,
                     carry_scr, conv_scr, *, tiles_per_seq, tm):
    i = pl.program_id(0)

    @pl.when(i % tiles_per_seq == 0)
    def _():
        carry_scr[...] = jnp.zeros_like(carry_scr)
        conv_scr[0:8, :] = jnp.zeros((8, CONV_DIM), F32)

    h = _rms(x_ref[...], g_ref[...]).astype(BF16)
    q = _dot(h, wq_ref[...])
    k = _dot(h, wk_ref[...])
    v = _dot(h, wv_ref[...])
    k_ref[0] = k.T
    v_ref[0] = v.T
    q16_ref[...] = (q * ATT_SCALE).astype(BF16)
    k16_ref[...] = k.astype(BF16)
    v16_ref[...] = v.astype(BF16)
    z_ref[...] = _dot(h, wz_ref[...])
    slab = _small_slab(_dot(h, ws_ref[...]), bs_ref[...])
    slab_ref[...] = slab

    u = _dot(h, wx_ref[...])
    conv_scr[8:8 + tm, :] = u
    conv = (cb_ref[...] + cw_ref[0:1, :] * conv_scr[5:5 + tm, :] + cw_ref[1:2, :] * conv_scr[6:6 + tm, :]
            + cw_ref[2:3, :] * conv_scr[7:7 + tm, :] + cw_ref[3:4, :] * u)
    conv_scr[0:8, :] = u[tm - 8:tm, :]
    tail_ref[0] = u[tm - 8:tm, :]
    xc_ref[...] = _silu(conv)

    lane = lax.broadcasted_iota(jnp.int32, slab.shape, 1)
    logf = jnp.where(lane < DT_LANE0, slab, 0.0)
    c = _dot3r(_tril(tm), logf) + carry_scr[0:1, :]
    carry_scr[0:1, :] = c[tm - 1:tm, :]
    c_hi = c.astype(BF16).astype(F32)
    r1 = c - c_hi
    c_mid = r1.astype(BF16).astype(F32)
    c_lo = (r1 - c_mid).astype(BF16).astype(F32)
    aug = (c_hi + pltpu.roll(c_mid, DT_LANE0, 1) + pltpu.roll(c_lo, 2 * DT_LANE0, 1)
           + jnp.where(lane == 3 * DT_LANE0, 1.0, 0.0)).astype(BF16)
    aq_ref[...] = _dot(aug, eq_ref[...]).astype(BF16)
    ak_ref[...] = _dot(aug, ek_ref[...]).astype(BF16)


def _aug_scatter_mats():
    n_pairs = N_ATT_HEADS // 2
    eq = np.zeros((LANES, n_pairs * LANES), np.float32)
    ek = np.zeros((LANES, n_pairs * LANES), np.float32)
    one_lane = 3 * DT_LANE0
    for h in range(N_ATT_HEADS):
        base = (h // 2) * LANES + (0 if h % 2 else HEAD_DIM)
        for p in range(3):
            eq[p * DT_LANE0 + h, base + p] = 1.0
            eq[one_lane, base + 3 + p] = 1.0
            ek[one_lane, base + p] = 1.0
            ek[p * DT_LANE0 + h, base + 3 + p] = -1.0
    return jnp.asarray(eq, BF16), jnp.asarray(ek, BF16)


def _inproj_aug(x1, norm_g, ws, bias_s, conv_w, conv_b, *, tm, seq):
    n = x1.shape[0]
    eq, ek = _aug_scatter_mats()
    row = lambda w: pl.BlockSpec((tm, w), lambda i: (i, 0))
    full = lambda a: pl.BlockSpec(a.shape, lambda i: (0, 0))
    wq, wk, wv, wz, wx, wsm = ws
    tps = seq // tm
    nseq = n // seq
    kv_t = pl.BlockSpec((1, D_ATT, tm), lambda i: (i // tps, 0, i % tps))
    tail = pl.BlockSpec((1, 8, CONV_DIM), lambda i: (i // tps, 0, 0))
    sds = jax.ShapeDtypeStruct
    return pl.pallas_call(
        functools.partial(_inproj_aug_body, tiles_per_seq=tps, tm=tm),
        grid=(n // tm,),
        in_specs=[row(D_MODEL), full(norm_g), full(wq), full(wk), full(wv), full(wz), full(wx),
                  full(wsm), full(bias_s), full(eq), full(ek), full(conv_w), full(conv_b)],
        out_specs=[kv_t, kv_t, row(D_ATT), row(D_ATT), row(D_ATT), row(D_ATT), row(D_ATT),
                   row(D_SSM), row(CONV_DIM), tail, row(LANES)],
        out_shape=[
            sds((nseq, D_ATT, seq), F32), sds((nseq, D_ATT, seq), F32),
            sds((n, D_ATT), BF16), sds((n, D_ATT), BF16), sds((n, D_ATT), BF16),
            sds((n, D_ATT), BF16), sds((n, D_ATT), BF16),
            sds((n, D_SSM), F32), sds((n, CONV_DIM), F32), sds((nseq, 8, CONV_DIM), F32), sds((n, LANES), F32)],
        scratch_shapes=[pltpu.VMEM((8, LANES), F32), pltpu.VMEM((8 + tm, CONV_DIM), F32)],
        compiler_params=_cparams(("arbitrary",)),
        name="inproj_prompt",
    )(x1, norm_g, wq, wk, wv, wz, wx, wsm, bias_s, eq, ek, conv_w, conv_b)


def _inproj_plain_body(x_ref, g_ref, wq_ref, wk_ref, wv_ref, wz_ref, wx_ref, ws_ref, bs_ref,
                       q_ref, k_ref, v_ref, z_ref, xbc_ref, slab_ref):
    h = _rms(x_ref[...], g_ref[...]).astype(BF16)
    q_ref[...] = _dot(h, wq_ref[...])
    k_ref[...] = _dot(h, wk_ref[...])
    v_ref[...] = _dot(h, wv_ref[...])
    z_ref[...] = _dot(h, wz_ref[...])
    xbc_ref[...] = _dot(h, wx_ref[...])
    slab_ref[...] = _small_slab(_dot(h, ws_ref[...]), bs_ref[...])


def _inproj_plain(x1, norm_g, ws, bias_s):
    n = x1.shape[0]
    wq, wk, wv, wz, wx, wsm = ws
    full = lambda a: pl.BlockSpec(a.shape, lambda i: (0, 0))
    out = lambda w: pl.BlockSpec((n, w), lambda i: (0, 0))
    return pl.pallas_call(
        _inproj_plain_body,
        grid=(1,),
        in_specs=[full(x1), full(norm_g), full(wq), full(wk), full(wv), full(wz), full(wx),
                  full(wsm), full(bias_s)],
        out_specs=[out(D_ATT), out(D_ATT), out(D_ATT), out(D_SSM), out(CONV_DIM), out(LANES)],
        out_shape=[jax.ShapeDtypeStruct((n, w), F32)
                   for w in (D_ATT, D_ATT, D_ATT, D_SSM, CONV_DIM, LANES)],
        compiler_params=_cparams(("arbitrary",)),
        name="inproj_sample",
    )(x1, norm_g, wq, wk, wv, wz, wx, wsm, bias_s)


def _head_expand_mat():
    e = np.zeros((LANES, D_SSM), np.float32)
    for h in range(N_SSM_HEADS):
        e[DT_LANE0 + h, h * SSM_HEAD_DIM:(h + 1) * SSM_HEAD_DIM] = 1.0
    return jnp.asarray(e, BF16)


def _gate_groupnorm(y, z, gain):
    g = y * _silu(z)
    half = D_SSM // N_SSM_GROUPS
    parts = []
    for gi in range(N_SSM_GROUPS):
        gg = g[:, gi * half:(gi + 1) * half]
        parts.append(gg * lax.rsqrt(jnp.mean(gg * gg, axis=-1, keepdims=True) + EPS))
    return jnp.concatenate(parts, axis=1) * gain


def _ssd_chunk(xc, slab, z, ht_old, arow, dsk, sg, e8):
    Q = SSD_CHUNK
    xs = xc[:, :D_SSM]
    lane = lax.broadcasted_iota(jnp.int32, slab.shape, 1)
    dtm = jnp.where((lane >= DT_LANE0) & (lane < 2 * DT_LANE0), slab, 0.0)
    a = dtm * arow
    a_c = _dot3r(_tril(Q), a)
    a_ct = a_c.T
    a_last = a_c[Q - 1:Q, :]
    ea_x = _dot3(jnp.exp(a_c), e8)
    dec_x = _dot3(jnp.exp(a_last - a_c), e8)
    xdt = xs * _dot3(dtm, e8)
    xdt_b = xdt.astype(BF16)
    xd_b = (xdt * dec_x).astype(BF16)
    cd_x = ea_x[Q - 1:Q, :]

    r_i = lax.broadcasted_iota(jnp.int32, (Q, Q), 0)
    c_i = lax.broadcasted_iota(jnp.int32, (Q, Q), 1)
    tri = c_i <= r_i
    lane_q = lax.broadcasted_iota(jnp.int32, (Q, LANES), 1)
    gw = D_SSM // N_SSM_GROUPS
    hpg = N_SSM_HEADS // N_SSM_GROUPS
    y_parts, ht_parts = [], []
    for g in range(N_SSM_GROUPS):
        bm = xc[:, D_SSM + g * D_STATE:D_SSM + (g + 1) * D_STATE]
        cm = xc[:, D_SSM + (N_SSM_GROUPS + g) * D_STATE:D_SSM + (N_SSM_GROUPS + g + 1) * D_STATE]
        cb16 = cm.astype(BF16)
        cbm = _dot_nt(cb16, bm.astype(BF16))
        y_off = _dot(cb16, ht_old[:, g * gw:(g + 1) * gw].astype(BF16))
        y_diag = []
        for pr in range(hpg // 2):
            halves = []
            for hh in range(2):
                h = g * hpg + pr * 2 + hh
                col = a_c[:, DT_LANE0 + h:DT_LANE0 + h + 1]
                row = a_ct[DT_LANE0 + h:DT_LANE0 + h + 1, :]
                lm = jnp.where(tri, jnp.exp(col - row), 0.0)
                sc = (cbm * lm).astype(BF16)
                lo = g * gw + pr * LANES
                halves.append(_dot(sc, xdt_b[:, lo:lo + LANES]))
            y_diag.append(jnp.where(lane_q < SSM_HEAD_DIM, halves[0], halves[1]))
        y_parts.append(jnp.concatenate(y_diag, axis=1) + y_off * ea_x[:, g * gw:(g + 1) * gw])
        new = _dot(bm.T.astype(BF16), xd_b[:, g * gw:(g + 1) * gw])
        ht_parts.append(ht_old[:, g * gw:(g + 1) * gw] * cd_x[:, g * gw:(g + 1) * gw] + new)
    y = jnp.concatenate(y_parts, axis=1) + dsk * xs
    return _gate_groupnorm(y, z, sg).astype(BF16), jnp.concatenate(ht_parts, axis=1)


def _ssd_body(xc_ref, slab_ref, z_ref, arow_ref, dsk_ref, sg_ref, e8_ref, gn_ref, st_ref, ht_scr,
              *, n_steps, cps):
    c = pl.program_id(1)
    Q = SSD_CHUNK

    @pl.when(c == 0)
    def _():
        ht_scr[...] = jnp.zeros_like(ht_scr)

    ht = ht_scr[...]
    for ci in range(cps):
        rows = slice(ci * Q, (ci + 1) * Q)
        gn, ht = _ssd_chunk(xc_ref[0, rows, :], slab_ref[0, rows, :], z_ref[0, rows, :], ht,
                            arow_ref[...], dsk_ref[...], sg_ref[...], e8_ref[...])
        gn_ref[0, rows, :] = gn
    ht_scr[...] = ht

    @pl.when(c == n_steps - 1)
    def _():
        st_ref[0] = ht.T


def _ssd_prompt(xc, slab, z, a_row, dsk_x, ssm_g, *, cps):
    b, L, _ = xc.shape
    rows = cps * SSD_CHUNK
    ns = L // rows
    e8 = _head_expand_mat()
    blk = lambda w: pl.BlockSpec((1, rows, w), lambda bi, ci: (bi, ci, 0))
    full = lambda a: pl.BlockSpec(a.shape, lambda bi, ci: (0, 0))
    return pl.pallas_call(
        functools.partial(_ssd_body, n_steps=ns, cps=cps),
        grid=(b, ns),
        in_specs=[blk(CONV_DIM), blk(LANES), blk(D_SSM), full(a_row), full(dsk_x), full(ssm_g), full(e8)],
        out_specs=[blk(D_SSM), pl.BlockSpec((1, D_SSM, D_STATE), lambda bi, ci: (bi, 0, 0))],
        out_shape=[jax.ShapeDtypeStruct((b, L, D_SSM), BF16),
                   jax.ShapeDtypeStruct((b, D_SSM, D_STATE), F32)],
        scratch_shapes=[pltpu.VMEM((D_STATE, D_SSM), F32)],
        compiler_params=_cparams(("arbitrary", "arbitrary")),
        name="ssd_prompt",
    )(xc, slab, z, a_row, dsk_x, ssm_g, e8)


def _ssd_step_body(xbc_ref, sc_ref, slab_ref, z_ref, st_ref, cw_ref, cb_ref, arow_ref, dsk_ref,
                   sg_ref, e8_ref, gn_ref, so_ref, xc_scr, dtx_scr, decx_scr):
    b = pl.program_id(0)

    @pl.when(b == 0)
    def _():
        conv = (cb_ref[...] + cw_ref[0:1, :] * sc_ref[0] + cw_ref[1:2, :] * sc_ref[1]
                + cw_ref[2:3, :] * sc_ref[2] + cw_ref[3:4, :] * xbc_ref[...])
        xc_scr[...] = _silu(conv)
        slab = slab_ref[...]
        lane = lax.broadcasted_iota(jnp.int32, slab.shape, 1)
        dtm = jnp.where((lane >= DT_LANE0) & (lane < 2 * DT_LANE0), slab, 0.0)
        e8 = e8_ref[...]
        dtx_scr[...] = _dot3(dtm, e8)
        decx_scr[...] = _dot3(jnp.exp(dtm * arow_ref[...]), e8)

    xrow = xc_scr[pl.ds(b, 1), :]
    xs = xrow[:, :D_SSM]
    xdt = xs * dtx_scr[pl.ds(b, 1), :]
    dec = decx_scr[pl.ds(b, 1), :]
    gw = D_SSM // N_SSM_GROUPS
    y_rows = []
    for j in range(D_SSM // LANES):
        g = (j * LANES) // gw
        bm = xrow[:, D_SSM + g * D_STATE:D_SSM + (g + 1) * D_STATE]
        cm = xrow[:, D_SSM + (N_SSM_GROUPS + g) * D_STATE:D_SSM + (N_SSM_GROUPS + g + 1) * D_STATE]
        xcol = _row_to_col(xdt[:, j * LANES:(j + 1) * LANES])
        dcol = _row_to_col(dec[:, j * LANES:(j + 1) * LANES])
        hs = dcol * st_ref[0, j * LANES:(j + 1) * LANES, :] + xcol * bm
        so_ref[0, j * LANES:(j + 1) * LANES, :] = hs
        y_rows.append(_col_to_row(jnp.sum(hs * cm, axis=1, keepdims=True)))
    y = jnp.concatenate(y_rows, axis=1) + dsk_ref[...] * xs
    gn_ref[0] = _gate_groupnorm(y, z_ref[pl.ds(b, 1), :], sg_ref[...]).astype(BF16)


def _ssd_step(xbc, sconv_t, slab, z, state, conv_w, conv_b, a_row, dsk_x, ssm_g):
    nb = xbc.shape[0]
    e8 = _head_expand_mat()
    full2 = lambda a: pl.BlockSpec(a.shape, lambda bi: (0,) * a.ndim)
    gn, st = pl.pallas_call(
        _ssd_step_body,
        grid=(nb,),
        in_specs=[full2(xbc), full2(sconv_t), full2(slab), full2(z),
                  pl.BlockSpec((1, D_SSM, D_STATE), lambda bi: (bi, 0, 0)),
                  full2(conv_w), full2(conv_b), full2(a_row), full2(dsk_x), full2(ssm_g), full2(e8)],
        out_specs=[pl.BlockSpec((1, 1, D_SSM), lambda bi: (bi, 0, 0)),
                   pl.BlockSpec((1, D_SSM, D_STATE), lambda bi: (bi, 0, 0))],
        out_shape=[jax.ShapeDtypeStruct((nb, 1, D_SSM), BF16),
                   jax.ShapeDtypeStruct((nb, D_SSM, D_STATE), F32)],
        scratch_shapes=[pltpu.VMEM((nb, CONV_DIM), F32), pltpu.VMEM((nb, D_SSM), F32),
                        pltpu.VMEM((nb, D_SSM), F32)],
        compiler_params=_cparams(("arbitrary",)),
        name="ssd_step",
    )(xbc, sconv_t, slab, z, state, conv_w, conv_b, a_row, dsk_x, ssm_g, e8)
    return gn.reshape(nb, D_SSM), st


DECODE_GROUP = 16
DECODE_SLOTS = 3


def _fox_body(pt_ref, q_ref, k_ref, v_ref, aq_ref, ak_ref, qs_ref, ks_ref, vs_ref, lfs_ref,
              ck_hbm, cv_hbm, clf_hbm, o_ref, os_ref,
              s_scr, m_scr, acc_scr, qp_scr, kbuf, vbuf, lbuf, sems, qblk_scr, dm_scr, dl_scr, dacc_scr,
              carry_scr, *, tq, nt, gps, gpseq, n_groups):
    G, NS = DECODE_GROUP, DECODE_SLOTS
    H, HD, P = N_ATT_HEADS, HEAD_DIM, PAGE_SIZE
    t = pl.program_id(2)
    sid = (pl.program_id(0) * pl.num_programs(1) + pl.program_id(1)) * pl.num_programs(2) + t
    nl = tq // LANES

    def group_copies(gg):
        slot = gg % NS
        src = jnp.minimum(gg, n_groups - 1)
        seq = src // gpseq
        base = (gpseq - 1 - src % gpseq) * G
        copies = []
        for i in range(G):
            page = pt_ref[seq, base + i]
            copies.append(pltpu.make_async_copy(ck_hbm.at[page], kbuf.at[slot, i], sems.at[slot, 0]))
            copies.append(pltpu.make_async_copy(cv_hbm.at[page], vbuf.at[slot, i], sems.at[slot, 1]))
            copies.append(pltpu.make_async_copy(clf_hbm.at[page], lbuf.at[slot, i], sems.at[slot, 2]))
        return copies

    def start_group(gg):
        for n, cp in enumerate(group_copies(gg)):
            cp.start(priority=1 if n % 3 == 1 else 0)

    def wait_group(gg):
        for cp in group_copies(gg):
            cp.wait()

    r8 = lax.broadcasted_iota(jnp.int32, (H, D_ATT), 0)
    c8 = lax.broadcasted_iota(jnp.int32, (H, D_ATT), 1)
    own_head = c8 // HD == r8

    def decode_init():
        qs = (qs_ref[0] * ATT_SCALE).astype(BF16).astype(F32)
        qblk = jnp.where(own_head, qs, 0.0)
        qblk_scr[...] = qblk.astype(BF16)
        k2 = ks_ref[0].astype(BF16).astype(F32)
        dm_scr[...] = jnp.broadcast_to(jnp.sum(qblk * k2, axis=1, keepdims=True), dm_scr.shape)
        dl_scr[...] = jnp.ones_like(dl_scr)
        dacc_scr[...] = jnp.broadcast_to(vs_ref[0].astype(BF16).astype(F32), dacc_scr.shape)
        carry_scr[...] = jnp.broadcast_to(_row_to_col(lfs_ref[0])[0:H, :], carry_scr.shape)

    def decode_group(slot):
        kcat = jnp.concatenate([kbuf[slot, i].astype(BF16) for i in range(G)], axis=1)
        s = _dot(qblk_scr[...], kcat)
        x = jnp.concatenate([lbuf[slot, i] for i in range(G)], axis=0)
        rr = lax.broadcasted_iota(jnp.int32, (P, P), 0)
        cc = lax.broadcasted_iota(jnp.int32, (P, P), 1)
        rev_local = _dot3(x, (rr > cc).astype(BF16))
        tot = jnp.sum(x, axis=1, keepdims=True)
        carry = carry_scr[:, 0:1]
        s_pages = [None] * G
        for i in reversed(range(G)):
            s_pages[i] = s[:, i * P:(i + 1) * P] + rev_local[i * H:(i + 1) * H, :] + carry
            carry = carry + tot[i * H:(i + 1) * H, :]
        carry_scr[...] = jnp.broadcast_to(carry, carry_scr.shape)
        st = jnp.concatenate(s_pages, axis=1)
        m_old = dm_scr[:, 0:1]
        m_new = jnp.maximum(m_old, jnp.max(st, axis=1, keepdims=True))
        alpha = jnp.exp(m_old - m_new)
        p = jnp.exp(st - m_new)
        l_new = alpha * dl_scr[:, 0:1] + jnp.sum(p, axis=1, keepdims=True)
        vcat = jnp.concatenate([vbuf[slot, i].astype(BF16) for i in range(G)], axis=1)
        dacc_scr[...] = alpha * dacc_scr[...] + _dot_nt(p.astype(BF16), vcat)
        dm_scr[...] = jnp.broadcast_to(m_new, dm_scr.shape)
        dl_scr[...] = jnp.broadcast_to(l_new, dl_scr.shape)

    def decode_finish():
        o = dacc_scr[...] / dl_scr[:, 0:1]
        os_ref[0] = jnp.sum(jnp.where(own_head, o, 0.0), axis=0, keepdims=True)

    r = lax.broadcasted_iota(jnp.int32, (tq, tq), 0)
    c = lax.broadcasted_iota(jnp.int32, (tq, tq), 1)
    causal = c <= r

    def chunk_of(ci):
        first = ci <= t
        sel = jnp.where(first, 0, 1)
        q0 = pl.multiple_of(jnp.where(first, t, nt - 1 - t) * tq, tq)
        k0 = pl.multiple_of(jnp.where(first, ci, ci - t - 1) * tq, tq)
        return sel, q0, k0

    lane_q = lax.broadcasted_iota(jnp.int32, (tq, LANES), 1)
    even_dims = lane_q < HD

    def per_head(x2, a2):
        return jnp.where(even_dims, x2, a2), jnp.where(even_dims, a2, x2)

    def build_queries():
        for x, tile in enumerate((t, nt - 1 - t)):
            rows = pl.ds(pl.multiple_of(tile * tq, tq), tq)
            q_e, q_o = per_head(q_ref[0, rows, :], aq_ref[0, rows, :])
            qp_scr[x, 0] = q_e
            qp_scr[x, 1] = q_o

    def scores(ci):
        sel, q0, k0 = chunk_of(ci)
        ks = per_head(k_ref[0, pl.ds(k0, tq), :], ak_ref[0, pl.ds(k0, tq), :])
        for j in range(2):
            s = _dot_nt(qp_scr[sel, j], ks[j])
            if ci == nt:
                s = jnp.where(causal, s, NEG_BIG)
            elif ci < nt // 2:
                s = jnp.where(jnp.logical_or(causal, ci != t), s, NEG_BIG)
            s_scr[j, ci] = s
            m = m_scr[sel, j]
            for u in range(nl):
                m = jnp.maximum(m, s[:, u * LANES:(u + 1) * LANES])
            m_scr[sel, j] = m

    def row_max():
        for x in range(2):
            for j in range(2):
                m_scr[x, j] = jnp.broadcast_to(jnp.max(m_scr[x, j], axis=1, keepdims=True), (tq, LANES))

    one_at = lambda ln: jnp.where(lane_q == ln, 1.0, 0.0).astype(BF16)
    ones_e, ones_o = one_at(HD), one_at(0)

    def weighted(ci):
        sel, q0, k0 = chunk_of(ci)
        v2 = v_ref[0, pl.ds(k0, tq), :]
        vs = (jnp.where(even_dims, v2, ones_e), jnp.where(even_dims, ones_o, v2))
        for j in range(2):
            s = s_scr[j, ci]
            mrep = m_scr[sel, j]
            p = jnp.concatenate([jnp.exp(s[:, u * LANES:(u + 1) * LANES] - mrep) for u in range(nl)], axis=1)
            acc_scr[sel, j] += _dot(p.astype(BF16), vs[j])

    def write_out():
        lane = lax.broadcasted_iota(jnp.int32, (tq, LANES), 1)
        for x, tile in enumerate((t, nt - 1 - t)):
            acc_e, acc_o = acc_scr[x, 0], acc_scr[x, 1]
            out = jnp.where(lane < HEAD_DIM, acc_e / acc_e[:, HEAD_DIM:HEAD_DIM + 1], acc_o / acc_o[:, 0:1])
            o_ref[0, pl.ds(pl.multiple_of(tile * tq, tq), tq), :] = out

    items = ([functools.partial(scores, ci) for ci in range(nt + 1)] + [row_max]
             + [functools.partial(weighted, ci) for ci in range(nt + 1)] + [write_out])
    per_part = -(-len(items) // gps)
    parts = [items[i * per_part:(i + 1) * per_part] for i in range(gps)]

    @pl.when(sid == 0)
    def _():
        for g0 in range(NS - 1):
            start_group(g0)

    pl.when((sid * gps) % gpseq == 0)(decode_init)
    m_scr[...] = jnp.full(m_scr.shape, NEG_BIG, F32)
    acc_scr[...] = jnp.zeros_like(acc_scr)
    build_queries()
    for gi in range(gps):
        gg = sid * gps + gi
        wait_group(gg)
        start_group(gg + (NS - 1))
        for item in parts[gi]:
            item()
        decode_group(gg % NS)
    pl.when((sid * gps + gps - 1) % gpseq == gpseq - 1)(decode_finish)

    @pl.when(sid == pl.num_programs(0) * pl.num_programs(1) * pl.num_programs(2) - 1)
    def _():
        for extra in range(NS - 1):
            wait_group(n_groups + extra)


def _fox_attention(q16, k16, v16, aq, ak, q_s, k_s, v_s, slab_s, cache_k, cache_v, cache_logf, page_table,
                   *, tq):
    b, L, _ = q16.shape
    nt = L // tq
    nb, n_pages = page_table.shape
    n_steps = b * (N_ATT_HEADS // 2) * (nt // 2)
    n_groups = nb * n_pages // DECODE_GROUP
    gps = n_groups // n_steps
    gpseq = n_pages // DECODE_GROUP
    assert nt % 2 == 0 and n_pages % DECODE_GROUP == 0
    assert gps * n_steps == n_groups and gpseq % gps == 0 and n_groups >= DECODE_SLOTS
    spq = gpseq // gps
    hp_n, t_n = N_ATT_HEADS // 2, nt // 2
    seq_of = lambda bi, hp, t: ((bi * hp_n + hp) * t_n + t) // spq
    pair = pl.BlockSpec((1, L, LANES), lambda bi, hp, t, pt: (bi, 0, hp), pipeline_mode=pl.Buffered(1))
    srow = lambda w: pl.BlockSpec((1, 1, w), lambda bi, hp, t, pt: (seq_of(bi, hp, t), 0, 0))
    anyspec = pl.BlockSpec(memory_space=pl.ANY)
    row3 = lambda a: a.reshape(nb, 1, a.shape[-1])
    G, NS = DECODE_GROUP, DECODE_SLOTS
    att, att_s = pl.pallas_call(
        functools.partial(_fox_body, tq=tq, nt=nt, gps=gps, gpseq=gpseq, n_groups=n_groups),
        grid_spec=pltpu.PrefetchScalarGridSpec(
            num_scalar_prefetch=1,
            grid=(b, hp_n, t_n),
            in_specs=[pair, pair, pair, pair, pair, srow(D_ATT), srow(D_ATT), srow(D_ATT), srow(LANES),
                      anyspec, anyspec, anyspec],
            out_specs=[pl.BlockSpec((1, L, LANES), lambda bi, hp, t, pt: (bi, 0, hp)),
                       pl.BlockSpec((1, 1, D_ATT), lambda bi, hp, t, pt: (seq_of(bi, hp, t), 0, 0))],
            scratch_shapes=[
                pltpu.VMEM((2, nt + 1, tq, tq), F32), pltpu.VMEM((2, 2, tq, LANES), F32),
                pltpu.VMEM((2, 2, tq, LANES), F32), pltpu.VMEM((2, 2, tq, LANES), BF16),
                pltpu.VMEM((NS, G, D_ATT, PAGE_SIZE), F32), pltpu.VMEM((NS, G, D_ATT, PAGE_SIZE), F32),
                pltpu.VMEM((NS, G, N_ATT_HEADS, PAGE_SIZE), F32), pltpu.SemaphoreType.DMA((NS, 3)),
                pltpu.VMEM((N_ATT_HEADS, D_ATT), BF16), pltpu.VMEM((8, LANES), F32),
                pltpu.VMEM((8, LANES), F32), pltpu.VMEM((8, D_ATT), F32), pltpu.VMEM((8, LANES), F32)]),
        out_shape=[jax.ShapeDtypeStruct((b, L, D_ATT), F32), jax.ShapeDtypeStruct((nb, 1, D_ATT), F32)],
        compiler_params=_cparams(("arbitrary", "arbitrary", "arbitrary")),
        name="fox_attention",
    )(page_table, q16, k16, v16, aq, ak, row3(q_s), row3(k_s), row3(v_s), row3(slab_s),
      cache_k, cache_v, cache_logf)
    return att, att_s.reshape(nb, D_ATT)


def _lane_pad(vec, lane0):
    out = jnp.zeros((1, LANES), F32)
    return out.at[0, lane0:lane0 + vec.shape[0]].set(vec.astype(F32))


def kernel(x_prompt, x_sample, cache_k, cache_v, cache_logf, page_table, state_conv, state_ssm,
           ffn1_norm, w_ffn1_in, w_ffn1_out, mix_norm, w_in, b_f, conv_w, conv_b, dt_bias, a_log,
           d_skip, att_out_norm, ssm_out_norm, w_out, ffn2_norm, w_ffn2_in, w_ffn2_out, final_norm):
    depth = w_in.shape[0]
    assert depth == 1
    B, L, _ = x_prompt.shape
    nb, T, _ = x_sample.shape
    assert T == 1
    H, HD = N_ATT_HEADS, HEAD_DIM

    l0 = 0
    w1i, w1o = w_ffn1_in[l0].astype(BF16), w_ffn1_out[l0].astype(BF16)
    w2i, w2o = w_ffn2_in[l0].astype(BF16), w_ffn2_out[l0].astype(BF16)
    wi = w_in[l0]
    o_f = 3 * D_ATT
    o_z = o_f + H
    o_x = o_z + D_SSM
    o_dt = o_x + CONV_DIM
    w_small = jnp.zeros((D_MODEL, LANES), F32)
    w_small = w_small.at[:, 0:H].set(wi[:, o_f:o_f + H])
    w_small = w_small.at[:, DT_LANE0:DT_LANE0 + N_SSM_HEADS].set(wi[:, o_dt:o_dt + N_SSM_HEADS])
    ws = tuple(w.astype(BF16) for w in (wi[:, 0:D_ATT], wi[:, D_ATT:2 * D_ATT], wi[:, 2 * D_ATT:3 * D_ATT],
                                        wi[:, o_z:o_z + D_SSM], wi[:, o_x:o_x + CONV_DIM], w_small))
    bias_s = _lane_pad(b_f[l0], 0) + _lane_pad(dt_bias[l0], DT_LANE0)
    a_row = _lane_pad(-jnp.exp(a_log[l0].astype(F32)), DT_LANE0)
    dsk_x = jnp.repeat(d_skip[l0].astype(F32), SSM_HEAD_DIM)[None, :]
    row = lambda v: v.astype(F32)[None, :]
    g1, gm, g2, gf = row(ffn1_norm[l0]), row(mix_norm[l0]), row(ffn2_norm[l0]), row(final_norm)
    ga, gs = row(att_out_norm[l0]), row(ssm_out_norm[l0])
    cw, cb = conv_w[l0].astype(F32), row(conv_b[l0])
    wm = w_out[l0].astype(BF16)

    TM, TF = FFN_ROWS, FFN_CHUNK
    xp = x_prompt.reshape(B * L, D_MODEL)
    x1 = _ffn(xp, g1, w1i, w1o, tm=TM, tf=TF)
    (k_p, v_p, q16, k16, v16, aq, ak, z_p, xc_p, tail_p, slab_p) = _inproj_aug(
        x1, gm, ws, bias_s, cw, cb, tm=INPROJ_ROWS, seq=L)
    xs = x_sample.reshape(nb, D_MODEL)
    x1s = _ffn(xs, g1, w1i, w1o, tm=nb, tf=TF)
    q_s, k_s, v_s, z_s, xbc_s, slab_s = _inproj_plain(x1s, gm, ws, bias_s)

    n_pool = cache_k.shape[1]
    ck = jnp.transpose(cache_k[l0], (0, 2, 3, 1)).reshape(n_pool, D_ATT, PAGE_SIZE)
    cv = jnp.transpose(cache_v[l0], (0, 2, 3, 1)).reshape(n_pool, D_ATT, PAGE_SIZE)
    clf = jnp.transpose(cache_logf[l0], (0, 2, 1))
    seq3 = lambda a: a.reshape(B, L, a.shape[-1])
    att, att_s = _fox_attention(seq3(q16), seq3(k16), seq3(v16), seq3(aq), seq3(ak),
                                q_s, k_s, v_s, slab_s, ck, cv, clf, page_table, tq=ATT_ROWS)

    gn_p, st_p = _ssd_prompt(seq3(xc_p), seq3(slab_p), seq3(z_p), a_row, dsk_x, gs, cps=SSD_CHUNKS_PER_STEP)
    y_p = _ffn2(x1, att.reshape(B * L, D_ATT), gn_p.reshape(B * L, D_SSM), ga, wm, g2, w2i, w2o, gf,
                tm=TM, tf=TF)

    sconv_t = jnp.swapaxes(state_conv[l0], 0, 1)
    gn_s, st_s = _ssd_step(xbc_s, sconv_t, slab_s, z_s,
                           state_ssm[l0].reshape(nb, D_SSM, D_STATE), cw, cb, a_row, dsk_x, gs)
    y_s = _ffn2(x1s, att_s, gn_s, ga, wm, g2, w2i, w2o, gf, tm=nb, tf=TF)

    return (
        y_p.reshape(B, L, D_MODEL),
        y_s.reshape(nb, 1, D_MODEL),
        jnp.transpose(k_p.reshape(1, B, H, HD, L), (0, 1, 4, 2, 3)),
        jnp.transpose(v_p.reshape(1, B, H, HD, L), (0, 1, 4, 2, 3)),
        slab_p[:, :H].reshape(1, B, L, H),
        tail_p[:, 8 - (CONV_W - 1):, :][None],
        st_p.reshape(1, B, N_SSM_HEADS, SSM_HEAD_DIM, D_STATE),
        k_s.reshape(1, nb, 1, H, HD),
        v_s.reshape(1, nb, 1, H, HD),
        slab_s[:, :H].reshape(1, nb, 1, H),
        jnp.concatenate([state_conv[l0][:, 1:, :], xbc_s[:, None, :]], axis=1)[None],
        st_s.reshape(1, nb, N_SSM_HEADS, SSM_HEAD_DIM, D_STATE),
    )
```

```python
import functools

import numpy as np
import jax
import jax.numpy as jnp
from jax import lax
from jax.experimental import pallas as pl
from jax.experimental.pallas import tpu as pltpu

F32 = jnp.float32
BF16 = jnp.bfloat16

D_MODEL = 1024
D_ATT = 512
D_SSM = 512
HEAD_DIM = 64
N_ATT_HEADS = 8
N_SSM_HEADS = 8
SSM_HEAD_DIM = 64
N_SSM_GROUPS = 2
D_STATE = 128
CONV_W = 4
CONV_DIM = D_SSM + 2 * N_SSM_GROUPS * D_STATE
SSD_CHUNK = 128
PAGE_SIZE = 128
D_FF = 2816
FFN_RESIDUAL = 0.5
EPS = 1e-6
ATT_SCALE = HEAD_DIM ** -0.5
LOG2E = 1.4426950408889634
LANES = 128
DT_LANE0 = 8
NEG_BIG = -1e30

VMEM_LIMIT = 56 * 1024 * 1024
FFN_ROWS = 1024
FFN_CHUNK = 256
INPROJ_ROWS = 512
ATT_ROWS = 256
SSD_CHUNKS_PER_STEP = 4
SSD_STEP_SEQS = 4


def _cparams(sem):
    return pltpu.CompilerParams(dimension_semantics=sem, vmem_limit_bytes=VMEM_LIMIT)


def _dot(a, b):
    return jnp.dot(a, b, preferred_element_type=F32)


def _dot_nt(a, b):
    return lax.dot_general(a, b, (((1,), (1,)), ((), ())), preferred_element_type=F32)


def _split3(x):
    hi = x.astype(BF16)
    r1 = x - hi.astype(F32)
    mid = r1.astype(BF16)
    lo = (r1 - mid.astype(F32)).astype(BF16)
    return hi, mid, lo


def _dot3(x, m):
    hi, mid, lo = _split3(x)
    return _dot(hi, m) + _dot(mid, m) + _dot(lo, m)


def _dot3r(m, x):
    hi, mid, lo = _split3(x)
    return _dot(m, hi) + _dot(m, mid) + _dot(m, lo)


def _rms(x, g):
    return x * lax.rsqrt(jnp.mean(x * x, axis=-1, keepdims=True) + EPS) * g


def _silu(x):
    return x / (1.0 + jnp.exp(-x))


def _tril(n, dtype=BF16):
    r = lax.broadcasted_iota(jnp.int32, (n, n), 0)
    c = lax.broadcasted_iota(jnp.int32, (n, n), 1)
    return (c <= r).astype(dtype)


def _row_to_col(row):
    n = row.shape[1]
    r = lax.broadcasted_iota(jnp.int32, (n, n), 0)
    c = lax.broadcasted_iota(jnp.int32, (n, n), 1)
    return jnp.sum(jnp.where(r == c, jnp.broadcast_to(row, (n, n)), 0.0), axis=1, keepdims=True)


def _col_to_row(col):
    n = col.shape[0]
    r = lax.broadcasted_iota(jnp.int32, (n, n), 0)
    c = lax.broadcasted_iota(jnp.int32, (n, n), 1)
    return jnp.sum(jnp.where(r == c, jnp.broadcast_to(col, (n, n)), 0.0), axis=0, keepdims=True)


def _resident(shape):
    return pl.BlockSpec(shape, lambda i: (0,) * len(shape), pipeline_mode=pl.Buffered(1))


def _swiglu_half(x, g_ref, wi_ref, wo_ref, a_scr, tf):
    h = _rms(x, g_ref[...]).astype(BF16)
    for c in range(D_FF // tf):
        gate = _dot(h, wi_ref[:, c * tf:(c + 1) * tf])
        up = _dot(h, wi_ref[:, D_FF + c * tf:D_FF + (c + 1) * tf])
        a_scr[:, c * tf:(c + 1) * tf] = (_silu(gate) * up).astype(BF16)
    return x + FFN_RESIDUAL * _dot(a_scr[...], wo_ref[...])


def _ffn_body(x_ref, g_ref, wi_ref, wo_ref, o_ref, a_scr, *, tf):
    o_ref[...] = _swiglu_half(x_ref[...], g_ref, wi_ref, wo_ref, a_scr, tf)


def _ffn(x, norm_g, w_in_b, w_out_b, *, tm, tf):
    n = x.shape[0]
    row = lambda w: pl.BlockSpec((tm, w), lambda i: (i, 0))
    return pl.pallas_call(
        functools.partial(_ffn_body, tf=tf),
        grid=(n // tm,),
        in_specs=[row(D_MODEL), _resident(norm_g.shape), _resident(w_in_b.shape), _resident(w_out_b.shape)],
        out_specs=row(D_MODEL),
        out_shape=jax.ShapeDtypeStruct((n, D_MODEL), F32),
        scratch_shapes=[pltpu.VMEM((tm, D_FF), BF16)],
        compiler_params=_cparams(("arbitrary",)),
        name="ffn1",
    )(x, norm_g, w_in_b, w_out_b)


def _ffn2_body(x_ref, att_ref, gn_ref, ag_ref, wm_ref, g_ref, wi_ref, wo_ref, fg_ref, o_ref, a_scr, *, tf):
    an = _rms(att_ref[...], ag_ref[...]).astype(BF16)
    x2 = x_ref[...] + _dot(an, wm_ref[:D_ATT, :]) + _dot(gn_ref[...], wm_ref[D_ATT:, :])
    o_ref[...] = _rms(_swiglu_half(x2, g_ref, wi_ref, wo_ref, a_scr, tf), fg_ref[...])


def _ffn2(x1, att, gn, att_g, w_mix_b, norm_g, w_in_b, w_out_b, final_g, *, tm, tf):
    n = x1.shape[0]
    row = lambda w: pl.BlockSpec((tm, w), lambda i: (i, 0))
    return pl.pallas_call(
        functools.partial(_ffn2_body, tf=tf),
        grid=(n // tm,),
        in_specs=[row(D_MODEL), row(D_ATT), row(D_SSM), _resident(att_g.shape), _resident(w_mix_b.shape),
                  _resident(norm_g.shape), _resident(w_in_b.shape), _resident(w_out_b.shape),
                  _resident(final_g.shape)],
        out_specs=row(D_MODEL),
        out_shape=jax.ShapeDtypeStruct((n, D_MODEL), F32),
        scratch_shapes=[pltpu.VMEM((tm, D_FF), BF16)],
        compiler_params=_cparams(("arbitrary",)),
        name="ffn2",
    )(x1, att, gn, att_g, w_mix_b, norm_g, w_in_b, w_out_b, final_g)


def _small_slab(raw, bias):
    v = raw + bias
    t = jnp.log1p(jnp.exp(-jnp.abs(v)))
    lane = lax.broadcasted_iota(jnp.int32, v.shape, 1)
    logf = -(jnp.maximum(-v, 0.0) + t)
    dt = jnp.maximum(v, 0.0) + t
    return jnp.where(lane < DT_LANE0, logf, jnp.where(lane < 2 * DT_LANE0, dt, 0.0))


def _inproj_aug_body(x_ref, g_ref, wq_ref, wk_ref, wv_ref, wz_ref, wx_ref, ws_ref, bs_ref,
                     eq_ref, ek_ref, cw_ref, cb_ref,
                     k_ref, v_ref, q16_ref, k16_ref, v16_ref, aq_ref, ak_ref, z_ref, xc_ref, tail_ref, slab_ref,
                     carry_scr, conv_scr, *, tiles_per_seq, tm):
    i = pl.program_id(0)

    @pl.when(i % tiles_per_seq == 0)
    def _():
        carry_scr[...] = jnp.zeros_like(carry_scr)
        conv_scr[0:8, :] = jnp.zeros((8, CONV_DIM), F32)

    h = _rms(x_ref[...], g_ref[...]).astype(BF16)
    q = _dot(h, wq_ref[...])
    k = _dot(h, wk_ref[...])
    v = _dot(h, wv_ref[...])
    k_ref[0] = k.T
    v_ref[0] = v.T
    q16_ref[...] = (q * (ATT_SCALE * LOG2E)).astype(BF16)
    k16_ref[...] = k.astype(BF16)
    v16_ref[...] = v.astype(BF16)
    z_ref[...] = _dot(h, wz_ref[...])
    slab = _small_slab(_dot(h, ws_ref[...]), bs_ref[...])
    slab_ref[...] = slab

    u = _dot(h, wx_ref[...])
    conv_scr[8:8 + tm, :] = u
    conv = (cb_ref[...] + cw_ref[0:1, :] * conv_scr[5:5 + tm, :] + cw_ref[1:2, :] * conv_scr[6:6 + tm, :]
            + cw_ref[2:3, :] * conv_scr[7:7 + tm, :] + cw_ref[3:4, :] * u)
    conv_scr[0:8, :] = u[tm - 8:tm, :]
    tail_ref[0] = u[tm - 8:tm, :]
    xc_ref[...] = _silu(conv)

    lane = lax.broadcasted_iota(jnp.int32, slab.shape, 1)
    logf = jnp.where(lane < DT_LANE0, slab, 0.0)
    c = _dot3r(_tril(tm), logf) + carry_scr[0:1, :]
    carry_scr[0:1, :] = c[tm - 1:tm, :]
    c2 = c * LOG2E
    c_hi = c2.astype(BF16).astype(F32)
    r1 = c2 - c_hi
    c_mid = r1.astype(BF16).astype(F32)
    c_lo = (r1 - c_mid).astype(BF16).astype(F32)
    aug = (c_hi + pltpu.roll(c_mid, DT_LANE0, 1) + pltpu.roll(c_lo, 2 * DT_LANE0, 1)
           + jnp.where(lane == 3 * DT_LANE0, 1.0, 0.0)).astype(BF16)
    aq_ref[...] = _dot(aug, eq_ref[...]).astype(BF16)
    ak_ref[...] = _dot(aug, ek_ref[...]).astype(BF16)


def _aug_scatter_mats():
    n_pairs = N_ATT_HEADS // 2
    eq = np.zeros((LANES, n_pairs * LANES), np.float32)
    ek = np.zeros((LANES, n_pairs * LANES), np.float32)
    one_lane = 3 * DT_LANE0
    for h in range(N_ATT_HEADS):
        base = (h // 2) * LANES + (0 if h % 2 else HEAD_DIM)
        for p in range(3):
            eq[p * DT_LANE0 + h, base + p] = 1.0
            eq[one_lane, base + 3 + p] = 1.0
            ek[one_lane, base + p] = 1.0
            ek[p * DT_LANE0 + h, base + 3 + p] = -1.0
    return jnp.asarray(eq, BF16), jnp.asarray(ek, BF16)


def _inproj_aug(x1, norm_g, ws, bias_s, conv_w, conv_b, *, tm, seq):
    n = x1.shape[0]
    eq, ek = _aug_scatter_mats()
    row = lambda w: pl.BlockSpec((tm, w), lambda i: (i, 0))
    full = lambda a: pl.BlockSpec(a.shape, lambda i: (0, 0))
    wq, wk, wv, wz, wx, wsm = ws
    tps = seq // tm
    nseq = n // seq
    kv_t = pl.BlockSpec((1, D_ATT, tm), lambda i: (i // tps, 0, i % tps))
    tail = pl.BlockSpec((1, 8, CONV_DIM), lambda i: (i // tps, 0, 0))
    sds = jax.ShapeDtypeStruct
    return pl.pallas_call(
        functools.partial(_inproj_aug_body, tiles_per_seq=tps, tm=tm),
        grid=(n // tm,),
        in_specs=[row(D_MODEL), full(norm_g), full(wq), full(wk), full(wv), full(wz), full(wx),
                  full(wsm), full(bias_s), full(eq), full(ek), full(conv_w), full(conv_b)],
        out_specs=[kv_t, kv_t, row(D_ATT), row(D_ATT), row(D_ATT), row(D_ATT), row(D_ATT),
                   row(D_SSM), row(CONV_DIM), tail, row(LANES)],
        out_shape=[
            sds((nseq, D_ATT, seq), F32), sds((nseq, D_ATT, seq), F32),
            sds((n, D_ATT), BF16), sds((n, D_ATT), BF16), sds((n, D_ATT), BF16),
            sds((n, D_ATT), BF16), sds((n, D_ATT), BF16),
            sds((n, D_SSM), F32), sds((n, CONV_DIM), F32), sds((nseq, 8, CONV_DIM), F32), sds((n, LANES), F32)],
        scratch_shapes=[pltpu.VMEM((8, LANES), F32), pltpu.VMEM((8 + tm, CONV_DIM), F32)],
        compiler_params=_cparams(("arbitrary",)),
        name="inproj_prompt",
    )(x1, norm_g, wq, wk, wv, wz, wx, wsm, bias_s, eq, ek, conv_w, conv_b)


def _inproj_plain_body(x_ref, g_ref, wq_ref, wk_ref, wv_ref, wz_ref, wx_ref, ws_ref, bs_ref,
                       q_ref, k_ref, v_ref, z_ref, xbc_ref, slab_ref):
    h = _rms(x_ref[...], g_ref[...]).astype(BF16)
    q_ref[...] = _dot(h, wq_ref[...])
    k_ref[...] = _dot(h, wk_ref[...])
    v_ref[...] = _dot(h, wv_ref[...])
    z_ref[...] = _dot(h, wz_ref[...])
    xbc_ref[...] = _dot(h, wx_ref[...])
    slab_ref[...] = _small_slab(_dot(h, ws_ref[...]), bs_ref[...])


def _inproj_plain(x1, norm_g, ws, bias_s):
    n = x1.shape[0]
    wq, wk, wv, wz, wx, wsm = ws
    full = lambda a: pl.BlockSpec(a.shape, lambda i: (0, 0))
    out = lambda w: pl.BlockSpec((n, w), lambda i: (0, 0))
    return pl.pallas_call(
        _inproj_plain_body,
        grid=(1,),
        in_specs=[full(x1), full(norm_g), full(wq), full(wk), full(wv), full(wz), full(wx),
                  full(wsm), full(bias_s)],
        out_specs=[out(D_ATT), out(D_ATT), out(D_ATT), out(D_SSM), out(CONV_DIM), out(LANES)],
        out_shape=[jax.ShapeDtypeStruct((n, w), F32)
                   for w in (D_ATT, D_ATT, D_ATT, D_SSM, CONV_DIM, LANES)],
        compiler_params=_cparams(("arbitrary",)),
        name="inproj_sample",
    )(x1, norm_g, wq, wk, wv, wz, wx, wsm, bias_s)


def _head_expand_mat():
    e = np.zeros((LANES, D_SSM), np.float32)
    for h in range(N_SSM_HEADS):
        e[DT_LANE0 + h, h * SSM_HEAD_DIM:(h + 1) * SSM_HEAD_DIM] = 1.0
    return jnp.asarray(e, BF16)


def _gate_groupnorm(y, z, gain):
    g = y * _silu(z)
    half = D_SSM // N_SSM_GROUPS
    parts = []
    for gi in range(N_SSM_GROUPS):
        gg = g[:, gi * half:(gi + 1) * half]
        parts.append(gg * lax.rsqrt(jnp.mean(gg * gg, axis=-1, keepdims=True) + EPS))
    return jnp.concatenate(parts, axis=1) * gain


def _ssd_chunk(xc, slab, z, ht_old, arow, dsk, sg, e8):
    Q = SSD_CHUNK
    xs = xc[:, :D_SSM]
    lane = lax.broadcasted_iota(jnp.int32, slab.shape, 1)
    dtm = jnp.where((lane >= DT_LANE0) & (lane < 2 * DT_LANE0), slab, 0.0)
    a = dtm * arow
    a_c = _dot3r(_tril(Q), a)
    a_ct = a_c.T
    a_last = a_c[Q - 1:Q, :]
    ea_x = _dot3(jnp.exp(a_c), e8)
    dec_x = _dot3(jnp.exp(a_last - a_c), e8)
    xdt = xs * _dot3(dtm, e8)
    xdt_b = xdt.astype(BF16)
    xd_b = (xdt * dec_x).astype(BF16)
    cd_x = ea_x[Q - 1:Q, :]

    r_i = lax.broadcasted_iota(jnp.int32, (Q, Q), 0)
    c_i = lax.broadcasted_iota(jnp.int32, (Q, Q), 1)
    tri = c_i <= r_i
    lane_q = lax.broadcasted_iota(jnp.int32, (Q, LANES), 1)
    gw = D_SSM // N_SSM_GROUPS
    hpg = N_SSM_HEADS // N_SSM_GROUPS
    y_parts, ht_parts = [], []
    for g in range(N_SSM_GROUPS):
        bm = xc[:, D_SSM + g * D_STATE:D_SSM + (g + 1) * D_STATE]
        cm = xc[:, D_SSM + (N_SSM_GROUPS + g) * D_STATE:D_SSM + (N_SSM_GROUPS + g + 1) * D_STATE]
        cb16 = cm.astype(BF16)
        cbm = _dot_nt(cb16, bm.astype(BF16))
        y_off = _dot(cb16, ht_old[:, g * gw:(g + 1) * gw].astype(BF16))
        y_diag = []
        for pr in range(hpg // 2):
            halves = []
            for hh in range(2):
                h = g * hpg + pr * 2 + hh
                col = a_c[:, DT_LANE0 + h:DT_LANE0 + h + 1]
                row = a_ct[DT_LANE0 + h:DT_LANE0 + h + 1, :]
                lm = jnp.where(tri, jnp.exp(col - row), 0.0)
                sc = (cbm * lm).astype(BF16)
                lo = g * gw + pr * LANES
                halves.append(_dot(sc, xdt_b[:, lo:lo + LANES]))
            y_diag.append(jnp.where(lane_q < SSM_HEAD_DIM, halves[0], halves[1]))
        y_parts.append(jnp.concatenate(y_diag, axis=1) + y_off * ea_x[:, g * gw:(g + 1) * gw])
        new = _dot(bm.T.astype(BF16), xd_b[:, g * gw:(g + 1) * gw])
        ht_parts.append(ht_old[:, g * gw:(g + 1) * gw] * cd_x[:, g * gw:(g + 1) * gw] + new)
    y = jnp.concatenate(y_parts, axis=1) + dsk * xs
    return _gate_groupnorm(y, z, sg).astype(BF16), jnp.concatenate(ht_parts, axis=1)


def _ssd_body(xc_ref, slab_ref, z_ref, arow_ref, dsk_ref, sg_ref, e8_ref, gn_ref, st_ref, ht_scr,
              *, n_steps, cps):
    c = pl.program_id(1)
    Q = SSD_CHUNK

    @pl.when(c == 0)
    def _():
        ht_scr[...] = jnp.zeros_like(ht_scr)

    ht = ht_scr[...]
    for ci in range(cps):
        rows = slice(ci * Q, (ci + 1) * Q)
        gn, ht = _ssd_chunk(xc_ref[0, rows, :], slab_ref[0, rows, :], z_ref[0, rows, :], ht,
                            arow_ref[...], dsk_ref[...], sg_ref[...], e8_ref[...])
        gn_ref[0, rows, :] = gn
    ht_scr[...] = ht

    @pl.when(c == n_steps - 1)
    def _():
        st_ref[0] = ht.T


def _ssd_prompt(xc, slab, z, a_row, dsk_x, ssm_g, *, cps):
    b, L, _ = xc.shape
    rows = cps * SSD_CHUNK
    ns = L // rows
    e8 = _head_expand_mat()
    blk = lambda w: pl.BlockSpec((1, rows, w), lambda bi, ci: (bi, ci, 0))
    full = lambda a: pl.BlockSpec(a.shape, lambda bi, ci: (0, 0))
    return pl.pallas_call(
        functools.partial(_ssd_body, n_steps=ns, cps=cps),
        grid=(b, ns),
        in_specs=[blk(CONV_DIM), blk(LANES), blk(D_SSM), full(a_row), full(dsk_x), full(ssm_g), full(e8)],
        out_specs=[blk(D_SSM), pl.BlockSpec((1, D_SSM, D_STATE), lambda bi, ci: (bi, 0, 0))],
        out_shape=[jax.ShapeDtypeStruct((b, L, D_SSM), BF16),
                   jax.ShapeDtypeStruct((b, D_SSM, D_STATE), F32)],
        scratch_shapes=[pltpu.VMEM((D_STATE, D_SSM), F32)],
        compiler_params=_cparams(("arbitrary", "arbitrary")),
        name="ssd_prompt",
    )(xc, slab, z, a_row, dsk_x, ssm_g, e8)


def _ssd_step_body(xbc_ref, sc_ref, slab_ref, z_ref, st_ref, cw_ref, cb_ref, arow_ref, dsk_ref,
                   sg_ref, e8_ref, gn_ref, so_ref, xc_scr, dtx_scr, decx_scr, *, spb):
    step = pl.program_id(0)

    @pl.when(step == 0)
    def _():
        conv = (cb_ref[...] + cw_ref[0:1, :] * sc_ref[0] + cw_ref[1:2, :] * sc_ref[1]
                + cw_ref[2:3, :] * sc_ref[2] + cw_ref[3:4, :] * xbc_ref[...])
        xc_scr[...] = _silu(conv)
        slab = slab_ref[...]
        lane = lax.broadcasted_iota(jnp.int32, slab.shape, 1)
        dtm = jnp.where((lane >= DT_LANE0) & (lane < 2 * DT_LANE0), slab, 0.0)
        e8 = e8_ref[...]
        dtx_scr[...] = _dot3(dtm, e8)
        decx_scr[...] = _dot3(jnp.exp(dtm * arow_ref[...]), e8)

    gw = D_SSM // N_SSM_GROUPS
    for i in range(spb):
        b = step * spb + i
        xrow = xc_scr[pl.ds(b, 1), :]
        xs = xrow[:, :D_SSM]
        xdt = xs * dtx_scr[pl.ds(b, 1), :]
        dec = decx_scr[pl.ds(b, 1), :]
        y_rows = []
        for j in range(D_SSM // LANES):
            g = (j * LANES) // gw
            bm = xrow[:, D_SSM + g * D_STATE:D_SSM + (g + 1) * D_STATE]
            cm = xrow[:, D_SSM + (N_SSM_GROUPS + g) * D_STATE:D_SSM + (N_SSM_GROUPS + g + 1) * D_STATE]
            xcol = _row_to_col(xdt[:, j * LANES:(j + 1) * LANES])
            dcol = _row_to_col(dec[:, j * LANES:(j + 1) * LANES])
            hs = dcol * st_ref[i, j * LANES:(j + 1) * LANES, :] + xcol * bm
            so_ref[i, j * LANES:(j + 1) * LANES, :] = hs
            y_rows.append(_col_to_row(jnp.sum(hs * cm, axis=1, keepdims=True)))
        y = jnp.concatenate(y_rows, axis=1) + dsk_ref[...] * xs
        gn_ref[i] = _gate_groupnorm(y, z_ref[pl.ds(b, 1), :], sg_ref[...]).astype(BF16)


def _ssd_step(xbc, sconv_t, slab, z, state, conv_w, conv_b, a_row, dsk_x, ssm_g, *, spb):
    nb = xbc.shape[0]
    assert nb % spb == 0
    e8 = _head_expand_mat()
    full2 = lambda a: pl.BlockSpec(a.shape, lambda bi: (0,) * a.ndim)
    gn, st = pl.pallas_call(
        functools.partial(_ssd_step_body, spb=spb),
        grid=(nb // spb,),
        in_specs=[full2(xbc), full2(sconv_t), full2(slab), full2(z),
                  pl.BlockSpec((spb, D_SSM, D_STATE), lambda bi: (bi, 0, 0)),
                  full2(conv_w), full2(conv_b), full2(a_row), full2(dsk_x), full2(ssm_g), full2(e8)],
        out_specs=[pl.BlockSpec((spb, 1, D_SSM), lambda bi: (bi, 0, 0)),
                   pl.BlockSpec((spb, D_SSM, D_STATE), lambda bi: (bi, 0, 0))],
        out_shape=[jax.ShapeDtypeStruct((nb, 1, D_SSM), BF16),
                   jax.ShapeDtypeStruct((nb, D_SSM, D_STATE), F32)],
        scratch_shapes=[pltpu.VMEM((nb, CONV_DIM), F32), pltpu.VMEM((nb, D_SSM), F32),
                        pltpu.VMEM((nb, D_SSM), F32)],
        compiler_params=_cparams(("arbitrary",)),
        name="ssd_step",
    )(xbc, sconv_t, slab, z, state, conv_w, conv_b, a_row, dsk_x, ssm_g, e8)
    return gn.reshape(nb, D_SSM), st


DECODE_GROUP = 16
DECODE_SLOTS = 3


def _fox_body(pt_ref, q_ref, k_ref, v_ref, aq_ref, ak_ref, qs_ref, ks_ref, vs_ref, lfs_ref,
              ck_hbm, cv_hbm, clf_hbm, o_ref, os_ref,
              s_scr, m_scr, acc_scr, qp_scr, kbuf, vbuf, lbuf, sems, qblk_scr, dm_scr, dl_scr, dacc_scr,
              carry_scr, *, tq, nt, gps, gpseq, n_groups):
    G, NS = DECODE_GROUP, DECODE_SLOTS
    H, HD, P = N_ATT_HEADS, HEAD_DIM, PAGE_SIZE
    t = pl.program_id(2)
    sid = (pl.program_id(0) * pl.num_programs(1) + pl.program_id(1)) * pl.num_programs(2) + t
    nl = tq // LANES

    def group_copies(gg):
        slot = gg % NS
        src = jnp.minimum(gg, n_groups - 1)
        seq = src // gpseq
        base = (gpseq - 1 - src % gpseq) * G
        copies = []
        for i in range(G):
            page = pt_ref[seq, base + i]
            copies.append(pltpu.make_async_copy(ck_hbm.at[page], kbuf.at[slot, i], sems.at[slot, 0]))
            copies.append(pltpu.make_async_copy(cv_hbm.at[page], vbuf.at[slot, i], sems.at[slot, 1]))
            copies.append(pltpu.make_async_copy(clf_hbm.at[page], lbuf.at[slot, i], sems.at[slot, 2]))
        return copies

    def start_group(gg):
        for n, cp in enumerate(group_copies(gg)):
            cp.start(priority=1 if n % 3 == 1 else 0)

    def wait_group(gg):
        for cp in group_copies(gg):
            cp.wait()

    r8 = lax.broadcasted_iota(jnp.int32, (H, D_ATT), 0)
    c8 = lax.broadcasted_iota(jnp.int32, (H, D_ATT), 1)
    own_head = c8 // HD == r8

    def decode_init():
        qs = (qs_ref[0] * ATT_SCALE).astype(BF16).astype(F32)
        qblk = jnp.where(own_head, qs, 0.0)
        qblk_scr[...] = qblk.astype(BF16)
        k2 = ks_ref[0].astype(BF16).astype(F32)
        dm_scr[...] = jnp.broadcast_to(jnp.sum(qblk * k2, axis=1, keepdims=True), dm_scr.shape)
        dl_scr[...] = jnp.ones_like(dl_scr)
        dacc_scr[...] = jnp.broadcast_to(vs_ref[0].astype(BF16).astype(F32), dacc_scr.shape)
        carry_scr[...] = jnp.broadcast_to(_row_to_col(lfs_ref[0])[0:H, :], carry_scr.shape)

    def decode_group(slot):
        kcat = jnp.concatenate([kbuf[slot, i].astype(BF16) for i in range(G)], axis=1)
        vcat = jnp.concatenate([vbuf[slot, i].astype(BF16) for i in range(G)], axis=1)
        s = _dot(qblk_scr[...], kcat)
        x = jnp.concatenate([lbuf[slot, i] for i in range(G)], axis=0)
        rr = lax.broadcasted_iota(jnp.int32, (P, P), 0)
        cc = lax.broadcasted_iota(jnp.int32, (P, P), 1)
        rev_local = _dot3(x, (rr > cc).astype(BF16))
        tot = jnp.sum(x, axis=1, keepdims=True)
        carry = carry_scr[:, 0:1]
        s_pages = [None] * G
        for i in reversed(range(G)):
            s_pages[i] = s[:, i * P:(i + 1) * P] + rev_local[i * H:(i + 1) * H, :] + carry
            carry = carry + tot[i * H:(i + 1) * H, :]
        carry_scr[...] = jnp.broadcast_to(carry, carry_scr.shape)
        st = jnp.concatenate(s_pages, axis=1)
        m_old = dm_scr[:, 0:1]
        m_new = jnp.maximum(m_old, jnp.max(st, axis=1, keepdims=True))
        alpha = jnp.exp(m_old - m_new)
        p = jnp.exp(st - m_new)
        l_new = alpha * dl_scr[:, 0:1] + jnp.sum(p, axis=1, keepdims=True)
        dacc_scr[...] = alpha * dacc_scr[...] + _dot_nt(p.astype(BF16), vcat)
        dm_scr[...] = jnp.broadcast_to(m_new, dm_scr.shape)
        dl_scr[...] = jnp.broadcast_to(l_new, dl_scr.shape)

    def decode_finish():
        o = dacc_scr[...] / dl_scr[:, 0:1]
        os_ref[0] = jnp.sum(jnp.where(own_head, o, 0.0), axis=0, keepdims=True)

    r = lax.broadcasted_iota(jnp.int32, (tq, tq), 0)
    c = lax.broadcasted_iota(jnp.int32, (tq, tq), 1)
    causal = c <= r

    def chunk_of(ci):
        first = ci <= t
        sel = jnp.where(first, 0, 1)
        q0 = pl.multiple_of(jnp.where(first, t, nt - 1 - t) * tq, tq)
        k0 = pl.multiple_of(jnp.where(first, ci, ci - t - 1) * tq, tq)
        return sel, q0, k0

    lane_q = lax.broadcasted_iota(jnp.int32, (tq, LANES), 1)
    even_dims = lane_q < HD

    def per_head(x2, a2):
        return jnp.where(even_dims, x2, a2), jnp.where(even_dims, a2, x2)

    def build_queries():
        for x, tile in enumerate((t, nt - 1 - t)):
            rows = pl.ds(pl.multiple_of(tile * tq, tq), tq)
            q_e, q_o = per_head(q_ref[0, rows, :], aq_ref[0, rows, :])
            qp_scr[x, 0] = q_e
            qp_scr[x, 1] = q_o

    def scores(ci):
        sel, q0, k0 = chunk_of(ci)
        ks = per_head(k_ref[0, pl.ds(k0, tq), :], ak_ref[0, pl.ds(k0, tq), :])
        for j in range(2):
            s = _dot_nt(qp_scr[sel, j], ks[j])
            if ci == nt:
                s = jnp.where(causal, s, NEG_BIG)
            elif ci < nt // 2:
                s = jnp.where(jnp.logical_or(causal, ci != t), s, NEG_BIG)
            s_scr[j, ci] = s
            m = m_scr[sel, j]
            for u in range(nl):
                m = jnp.maximum(m, s[:, u * LANES:(u + 1) * LANES])
            m_scr[sel, j] = m

    def row_max():
        for x in range(2):
            for j in range(2):
                m_scr[x, j] = jnp.broadcast_to(jnp.max(m_scr[x, j], axis=1, keepdims=True), (tq, LANES))

    one_at = lambda ln: jnp.where(lane_q == ln, 1.0, 0.0).astype(BF16)
    ones_e, ones_o = one_at(HD), one_at(0)

    def weighted(ci):
        sel, q0, k0 = chunk_of(ci)
        v2 = v_ref[0, pl.ds(k0, tq), :]
        vs = (jnp.where(even_dims, v2, ones_e), jnp.where(even_dims, ones_o, v2))
        for j in range(2):
            s = s_scr[j, ci]
            mrep = m_scr[sel, j]
            p = jnp.concatenate([jnp.exp2(s[:, u * LANES:(u + 1) * LANES] - mrep) for u in range(nl)], axis=1)
            acc_scr[sel, j] += _dot(p.astype(BF16), vs[j])

    def write_out():
        lane = lax.broadcasted_iota(jnp.int32, (tq, LANES), 1)
        for x, tile in enumerate((t, nt - 1 - t)):
            acc_e, acc_o = acc_scr[x, 0], acc_scr[x, 1]
            out = jnp.where(lane < HEAD_DIM, acc_e / acc_e[:, HEAD_DIM:HEAD_DIM + 1], acc_o / acc_o[:, 0:1])
            o_ref[0, pl.ds(pl.multiple_of(tile * tq, tq), tq), :] = out

    items = ([functools.partial(scores, ci) for ci in range(nt + 1)] + [row_max]
             + [functools.partial(weighted, ci) for ci in range(nt + 1)] + [write_out])
    per_part = -(-len(items) // gps)
    parts = [items[i * per_part:(i + 1) * per_part] for i in range(gps)]

    @pl.when(sid == 0)
    def _():
        for g0 in range(NS - 1):
            start_group(g0)

    pl.when((sid * gps) % gpseq == 0)(decode_init)
    m_scr[...] = jnp.full(m_scr.shape, NEG_BIG, F32)
    acc_scr[...] = jnp.zeros_like(acc_scr)
    build_queries()
    for gi in range(gps):
        gg = sid * gps + gi
        wait_group(gg)
        start_group(gg + (NS - 1))
        for item in parts[gi]:
            item()
        decode_group(gg % NS)
    pl.when((sid * gps + gps - 1) % gpseq == gpseq - 1)(decode_finish)

    @pl.when(sid == pl.num_programs(0) * pl.num_programs(1) * pl.num_programs(2) - 1)
    def _():
        for extra in range(NS - 1):
            wait_group(n_groups + extra)


def _fox_attention(q16, k16, v16, aq, ak, q_s, k_s, v_s, slab_s, cache_k, cache_v, cache_logf, page_table,
                   *, tq):
    b, L, _ = q16.shape
    nt = L // tq
    nb, n_pages = page_table.shape
    n_steps = b * (N_ATT_HEADS // 2) * (nt // 2)
    n_groups = nb * n_pages // DECODE_GROUP
    gps = n_groups // n_steps
    gpseq = n_pages // DECODE_GROUP
    assert nt % 2 == 0 and n_pages % DECODE_GROUP == 0
    assert gps * n_steps == n_groups and gpseq % gps == 0 and n_groups >= DECODE_SLOTS
    spq = gpseq // gps
    hp_n, t_n = N_ATT_HEADS // 2, nt // 2
    seq_of = lambda bi, hp, t: ((bi * hp_n + hp) * t_n + t) // spq
    pair = pl.BlockSpec((1, L, LANES), lambda bi, hp, t, pt: (bi, 0, hp), pipeline_mode=pl.Buffered(1))
    srow = lambda w: pl.BlockSpec((1, 1, w), lambda bi, hp, t, pt: (seq_of(bi, hp, t), 0, 0))
    anyspec = pl.BlockSpec(memory_space=pl.ANY)
    row3 = lambda a: a.reshape(nb, 1, a.shape[-1])
    G, NS = DECODE_GROUP, DECODE_SLOTS
    att, att_s = pl.pallas_call(
        functools.partial(_fox_body, tq=tq, nt=nt, gps=gps, gpseq=gpseq, n_groups=n_groups),
        grid_spec=pltpu.PrefetchScalarGridSpec(
            num_scalar_prefetch=1,
            grid=(b, hp_n, t_n),
            in_specs=[pair, pair, pair, pair, pair, srow(D_ATT), srow(D_ATT), srow(D_ATT), srow(LANES),
                      anyspec, anyspec, anyspec],
            out_specs=[pl.BlockSpec((1, L, LANES), lambda bi, hp, t, pt: (bi, 0, hp)),
                       pl.BlockSpec((1, 1, D_ATT), lambda bi, hp, t, pt: (seq_of(bi, hp, t), 0, 0))],
            scratch_shapes=[
                pltpu.VMEM((2, nt + 1, tq, tq), F32), pltpu.VMEM((2, 2, tq, LANES), F32),
                pltpu.VMEM((2, 2, tq, LANES), F32), pltpu.VMEM((2, 2, tq, LANES), BF16),
                pltpu.VMEM((NS, G, D_ATT, PAGE_SIZE), F32), pltpu.VMEM((NS, G, D_ATT, PAGE_SIZE), F32),
                pltpu.VMEM((NS, G, N_ATT_HEADS, PAGE_SIZE), F32), pltpu.SemaphoreType.DMA((NS, 3)),
                pltpu.VMEM((N_ATT_HEADS, D_ATT), BF16), pltpu.VMEM((8, LANES), F32),
                pltpu.VMEM((8, LANES), F32), pltpu.VMEM((8, D_ATT), F32), pltpu.VMEM((8, LANES), F32)]),
        out_shape=[jax.ShapeDtypeStruct((b, L, D_ATT), F32), jax.ShapeDtypeStruct((nb, 1, D_ATT), F32)],
        compiler_params=_cparams(("arbitrary", "arbitrary", "arbitrary")),
        name="fox_attention",
    )(page_table, q16, k16, v16, aq, ak, row3(q_s), row3(k_s), row3(v_s), row3(slab_s),
      cache_k, cache_v, cache_logf)
    return att, att_s.reshape(nb, D_ATT)


def _lane_pad(vec, lane0):
    out = jnp.zeros((1, LANES), F32)
    return out.at[0, lane0:lane0 + vec.shape[0]].set(vec.astype(F32))


def kernel(x_prompt, x_sample, cache_k, cache_v, cache_logf, page_table, state_conv, state_ssm,
           ffn1_norm, w_ffn1_in, w_ffn1_out, mix_norm, w_in, b_f, conv_w, conv_b, dt_bias, a_log,
           d_skip, att_out_norm, ssm_out_norm, w_out, ffn2_norm, w_ffn2_in, w_ffn2_out, final_norm):
    depth = w_in.shape[0]
    assert depth == 1
    B, L, _ = x_prompt.shape
    nb, T, _ = x_sample.shape
    assert T == 1
    H, HD = N_ATT_HEADS, HEAD_DIM

    l0 = 0
    w1i, w1o = w_ffn1_in[l0].astype(BF16), w_ffn1_out[l0].astype(BF16)
    w2i, w2o = w_ffn2_in[l0].astype(BF16), w_ffn2_out[l0].astype(BF16)
    wi = w_in[l0]
    o_f = 3 * D_ATT
    o_z = o_f + H
    o_x = o_z + D_SSM
    o_dt = o_x + CONV_DIM
    w_small = jnp.zeros((D_MODEL, LANES), F32)
    w_small = w_small.at[:, 0:H].set(wi[:, o_f:o_f + H])
    w_small = w_small.at[:, DT_LANE0:DT_LANE0 + N_SSM_HEADS].set(wi[:, o_dt:o_dt + N_SSM_HEADS])
    ws = tuple(w.astype(BF16) for w in (wi[:, 0:D_ATT], wi[:, D_ATT:2 * D_ATT], wi[:, 2 * D_ATT:3 * D_ATT],
                                        wi[:, o_z:o_z + D_SSM], wi[:, o_x:o_x + CONV_DIM], w_small))
    bias_s = _lane_pad(b_f[l0], 0) + _lane_pad(dt_bias[l0], DT_LANE0)
    a_row = _lane_pad(-jnp.exp(a_log[l0].astype(F32)), DT_LANE0)
    dsk_x = jnp.repeat(d_skip[l0].astype(F32), SSM_HEAD_DIM)[None, :]
    row = lambda v: v.astype(F32)[None, :]
    g1, gm, g2, gf = row(ffn1_norm[l0]), row(mix_norm[l0]), row(ffn2_norm[l0]), row(final_norm)
    ga, gs = row(att_out_norm[l0]), row(ssm_out_norm[l0])
    cw, cb = conv_w[l0].astype(F32), row(conv_b[l0])
    wm = w_out[l0].astype(BF16)

    TM, TF = FFN_ROWS, FFN_CHUNK
    xp = x_prompt.reshape(B * L, D_MODEL)
    x1 = _ffn(xp, g1, w1i, w1o, tm=TM, tf=TF)
    (k_p, v_p, q16, k16, v16, aq, ak, z_p, xc_p, tail_p, slab_p) = _inproj_aug(
        x1, gm, ws, bias_s, cw, cb, tm=INPROJ_ROWS, seq=L)
    xs = x_sample.reshape(nb, D_MODEL)
    x1s = _ffn(xs, g1, w1i, w1o, tm=nb, tf=TF)
    q_s, k_s, v_s, z_s, xbc_s, slab_s = _inproj_plain(x1s, gm, ws, bias_s)

    n_pool = cache_k.shape[1]
    ck = jnp.transpose(cache_k[l0], (0, 2, 3, 1)).reshape(n_pool, D_ATT, PAGE_SIZE)
    cv = jnp.transpose(cache_v[l0], (0, 2, 3, 1)).reshape(n_pool, D_ATT, PAGE_SIZE)
    clf = jnp.transpose(cache_logf[l0], (0, 2, 1))
    seq3 = lambda a: a.reshape(B, L, a.shape[-1])
    att, att_s = _fox_attention(seq3(q16), seq3(k16), seq3(v16), seq3(aq), seq3(ak),
                                q_s, k_s, v_s, slab_s, ck, cv, clf, page_table, tq=ATT_ROWS)

    gn_p, st_p = _ssd_prompt(seq3(xc_p), seq3(slab_p), seq3(z_p), a_row, dsk_x, gs, cps=SSD_CHUNKS_PER_STEP)
    y_p = _ffn2(x1, att.reshape(B * L, D_ATT), gn_p.reshape(B * L, D_SSM), ga, wm, g2, w2i, w2o, gf,
                tm=TM, tf=TF)

    sconv_t = jnp.swapaxes(state_conv[l0], 0, 1)
    gn_s, st_s = _ssd_step(xbc_s, sconv_t, slab_s, z_s,
                           state_ssm[l0].reshape(nb, D_SSM, D_STATE), cw, cb, a_row, dsk_x, gs,
                           spb=SSD_STEP_SEQS if nb % SSD_STEP_SEQS == 0 else 1)
    y_s = _ffn2(x1s, att_s, gn_s, ga, wm, g2, w2i, w2o, gf, tm=nb, tf=TF)

    return (
        y_p.reshape(B, L, D_MODEL),
        y_s.reshape(nb, 1, D_MODEL),
        jnp.transpose(k_p.reshape(1, B, H, HD, L), (0, 1, 4, 2, 3)),
        jnp.transpose(v_p.reshape(1, B, H, HD, L), (0, 1, 4, 2, 3)),
        slab_p[:, :H].reshape(1, B, L, H),
        tail_p[:, 8 - (CONV_W - 1):, :][None],
        st_p.reshape(1, B, N_SSM_HEADS, SSM_HEAD_DIM, D_STATE),
        k_s.reshape(1, nb, 1, H, HD),
        v_s.reshape(1, nb, 1, H, HD),
        slab_s[:, :H].reshape(1, nb, 1, H),
        jnp.concatenate([state_conv[l0][:, 1:, :], xbc_s[:, None, :]], axis=1)[None],
        st_s.reshape(1, nb, N_SSM_HEADS, SSM_HEAD_DIM, D_STATE),
    )
```

```python
import functools

import numpy as np
import jax
import jax.numpy as jnp
from jax import lax
from jax.experimental import pallas as pl
from jax.experimental.pallas import tpu as pltpu

F32 = jnp.float32
BF16 = jnp.bfloat16

D_MODEL = 1024
D_ATT = 512
D_SSM = 512
HEAD_DIM = 64
N_ATT_HEADS = 8
N_SSM_HEADS = 8
SSM_HEAD_DIM = 64
N_SSM_GROUPS = 2
D_STATE = 128
CONV_W = 4
CONV_DIM = D_SSM + 2 * N_SSM_GROUPS * D_STATE
SSD_CHUNK = 128
PAGE_SIZE = 128
D_FF = 2816
FFN_RESIDUAL = 0.5
EPS = 1e-6
ATT_SCALE = HEAD_DIM ** -0.5
LOG2E = 1.4426950408889634
LANES = 128
DT_LANE0 = 8
NEG_BIG = -1e30

VMEM_LIMIT = 56 * 1024 * 1024
FFN_ROWS = 1024
FFN_CHUNK = 256
INPROJ_ROWS = 512
ATT_ROWS = 256
SSD_CHUNKS_PER_STEP = 8
SSD_STEP_SEQS = 4


def _cparams(sem):
    return pltpu.CompilerParams(dimension_semantics=sem, vmem_limit_bytes=VMEM_LIMIT)


def _dot(a, b):
    return jnp.dot(a, b, preferred_element_type=F32)


def _dot_nt(a, b):
    return lax.dot_general(a, b, (((1,), (1,)), ((), ())), preferred_element_type=F32)


def _split3(x):
    hi = x.astype(BF16)
    r1 = x - hi.astype(F32)
    mid = r1.astype(BF16)
    lo = (r1 - mid.astype(F32)).astype(BF16)
    return hi, mid, lo


def _dot3(x, m):
    hi, mid, lo = _split3(x)
    return _dot(hi, m) + _dot(mid, m) + _dot(lo, m)


def _dot3r(m, x):
    hi, mid, lo = _split3(x)
    return _dot(m, hi) + _dot(m, mid) + _dot(m, lo)


def _rms(x, g):
    return x * lax.rsqrt(jnp.mean(x * x, axis=-1, keepdims=True) + EPS) * g


def _silu(x):
    return x / (1.0 + jnp.exp(-x))


def _tril(n, dtype=BF16):
    r = lax.broadcasted_iota(jnp.int32, (n, n), 0)
    c = lax.broadcasted_iota(jnp.int32, (n, n), 1)
    return (c <= r).astype(dtype)


def _row_to_col(row):
    n = row.shape[1]
    r = lax.broadcasted_iota(jnp.int32, (n, n), 0)
    c = lax.broadcasted_iota(jnp.int32, (n, n), 1)
    return jnp.sum(jnp.where(r == c, jnp.broadcast_to(row, (n, n)), 0.0), axis=1, keepdims=True)


def _col_to_row(col):
    n = col.shape[0]
    r = lax.broadcasted_iota(jnp.int32, (n, n), 0)
    c = lax.broadcasted_iota(jnp.int32, (n, n), 1)
    return jnp.sum(jnp.where(r == c, jnp.broadcast_to(col, (n, n)), 0.0), axis=0, keepdims=True)


def _resident(shape):
    return pl.BlockSpec(shape, lambda i: (0,) * len(shape), pipeline_mode=pl.Buffered(1))


def _swiglu_half(x, g_ref, wi_ref, wo_ref, a_scr, tf):
    h = _rms(x, g_ref[...]).astype(BF16)
    for c in range(D_FF // tf):
        gate = _dot(h, wi_ref[:, c * tf:(c + 1) * tf])
        up = _dot(h, wi_ref[:, D_FF + c * tf:D_FF + (c + 1) * tf])
        a_scr[:, c * tf:(c + 1) * tf] = (_silu(gate) * up).astype(BF16)
    return x + FFN_RESIDUAL * _dot(a_scr[...], wo_ref[...])


def _ffn_body(x_ref, g_ref, wi_ref, wo_ref, o_ref, a_scr, *, tf):
    o_ref[...] = _swiglu_half(x_ref[...], g_ref, wi_ref, wo_ref, a_scr, tf)


def _ffn(x, norm_g, w_in_b, w_out_b, *, tm, tf):
    n = x.shape[0]
    row = lambda w: pl.BlockSpec((tm, w), lambda i: (i, 0))
    return pl.pallas_call(
        functools.partial(_ffn_body, tf=tf),
        grid=(n // tm,),
        in_specs=[row(D_MODEL), _resident(norm_g.shape), _resident(w_in_b.shape), _resident(w_out_b.shape)],
        out_specs=row(D_MODEL),
        out_shape=jax.ShapeDtypeStruct((n, D_MODEL), F32),
        scratch_shapes=[pltpu.VMEM((tm, D_FF), BF16)],
        compiler_params=_cparams(("arbitrary",)),
        name="ffn1",
    )(x, norm_g, w_in_b, w_out_b)


def _ffn2_body(x_ref, att_ref, gn_ref, ag_ref, wm_ref, g_ref, wi_ref, wo_ref, fg_ref, o_ref, a_scr, *, tf):
    an = _rms(att_ref[...], ag_ref[...]).astype(BF16)
    x2 = x_ref[...] + _dot(an, wm_ref[:D_ATT, :]) + _dot(gn_ref[...], wm_ref[D_ATT:, :])
    o_ref[...] = _rms(_swiglu_half(x2, g_ref, wi_ref, wo_ref, a_scr, tf), fg_ref[...])


def _ffn2(x1, att, gn, att_g, w_mix_b, norm_g, w_in_b, w_out_b, final_g, *, tm, tf):
    n = x1.shape[0]
    row = lambda w: pl.BlockSpec((tm, w), lambda i: (i, 0))
    return pl.pallas_call(
        functools.partial(_ffn2_body, tf=tf),
        grid=(n // tm,),
        in_specs=[row(D_MODEL), row(D_ATT), row(D_SSM), _resident(att_g.shape), _resident(w_mix_b.shape),
                  _resident(norm_g.shape), _resident(w_in_b.shape), _resident(w_out_b.shape),
                  _resident(final_g.shape)],
        out_specs=row(D_MODEL),
        out_shape=jax.ShapeDtypeStruct((n, D_MODEL), F32),
        scratch_shapes=[pltpu.VMEM((tm, D_FF), BF16)],
        compiler_params=_cparams(("arbitrary",)),
        name="ffn2",
    )(x1, att, gn, att_g, w_mix_b, norm_g, w_in_b, w_out_b, final_g)


def _small_slab(raw, bias):
    v = raw + bias
    t = jnp.log1p(jnp.exp(-jnp.abs(v)))
    lane = lax.broadcasted_iota(jnp.int32, v.shape, 1)
    logf = -(jnp.maximum(-v, 0.0) + t)
    dt = jnp.maximum(v, 0.0) + t
    return jnp.where(lane < DT_LANE0, logf, jnp.where(lane < 2 * DT_LANE0, dt, 0.0))


def _inproj_aug_body(x_ref, g_ref, wq_ref, wk_ref, wv_ref, wz_ref, wx_ref, ws_ref, bs_ref,
                     eq_ref, ek_ref, cw_ref, cb_ref,
                     k_ref, v_ref, q16_ref, k16_ref, v16_ref, aq_ref, ak_ref, z_ref, xc_ref, tail_ref, slab_ref,
                     carry_scr, conv_scr, *, tiles_per_seq, tm):
    i = pl.program_id(0)

    @pl.when(i % tiles_per_seq == 0)
    def _():
        carry_scr[...] = jnp.zeros_like(carry_scr)
        conv_scr[0:8, :] = jnp.zeros((8, CONV_DIM), F32)

    h = _rms(x_ref[...], g_ref[...]).astype(BF16)
    q = _dot(h, wq_ref[...])
    k = _dot(h, wk_ref[...])
    v = _dot(h, wv_ref[...])
    k_ref[0] = k.T
    v_ref[0] = v.T
    q16_ref[...] = (q * (ATT_SCALE * LOG2E)).astype(BF16)
    k16_ref[...] = k.astype(BF16)
    v16_ref[...] = v.astype(BF16)
    z_ref[...] = _dot(h, wz_ref[...])
    slab = _small_slab(_dot(h, ws_ref[...]), bs_ref[...])
    slab_ref[...] = slab

    u = _dot(h, wx_ref[...])
    conv_scr[8:8 + tm, :] = u
    conv = (cb_ref[...] + cw_ref[0:1, :] * conv_scr[5:5 + tm, :] + cw_ref[1:2, :] * conv_scr[6:6 + tm, :]
            + cw_ref[2:3, :] * conv_scr[7:7 + tm, :] + cw_ref[3:4, :] * u)
    conv_scr[0:8, :] = u[tm - 8:tm, :]
    tail_ref[0] = u[tm - 8:tm, :]
    xc_ref[...] = _silu(conv)

    lane = lax.broadcasted_iota(jnp.int32, slab.shape, 1)
    logf = jnp.where(lane < DT_LANE0, slab, 0.0)
    c = _dot3r(_tril(tm), logf) + carry_scr[0:1, :]
    carry_scr[0:1, :] = c[tm - 1:tm, :]
    c2 = c * LOG2E
    c_hi = c2.astype(BF16).astype(F32)
    r1 = c2 - c_hi
    c_mid = r1.astype(BF16).astype(F32)
    c_lo = (r1 - c_mid).astype(BF16).astype(F32)
    aug = (c_hi + pltpu.roll(c_mid, DT_LANE0, 1) + pltpu.roll(c_lo, 2 * DT_LANE0, 1)
           + jnp.where(lane == 3 * DT_LANE0, 1.0, 0.0)).astype(BF16)
    aq_ref[...] = _dot(aug, eq_ref[...]).astype(BF16)
    ak_ref[...] = _dot(aug, ek_ref[...]).astype(BF16)


def _aug_scatter_mats():
    n_pairs = N_ATT_HEADS // 2
    eq = np.zeros((LANES, n_pairs * LANES), np.float32)
    ek = np.zeros((LANES, n_pairs * LANES), np.float32)
    one_lane = 3 * DT_LANE0
    for h in range(N_ATT_HEADS):
        base = (h // 2) * LANES + (0 if h % 2 else HEAD_DIM)
        for p in range(3):
            eq[p * DT_LANE0 + h, base + p] = 1.0
            eq[one_lane, base + 3 + p] = 1.0
            ek[one_lane, base + p] = 1.0
            ek[p * DT_LANE0 + h, base + 3 + p] = -1.0
    return jnp.asarray(eq, BF16), jnp.asarray(ek, BF16)


def _inproj_aug(x1, norm_g, ws, bias_s, conv_w, conv_b, *, tm, seq):
    n = x1.shape[0]
    eq, ek = _aug_scatter_mats()
    row = lambda w: pl.BlockSpec((tm, w), lambda i: (i, 0))
    full = lambda a: pl.BlockSpec(a.shape, lambda i: (0, 0))
    wq, wk, wv, wz, wx, wsm = ws
    tps = seq // tm
    nseq = n // seq
    kv_t = pl.BlockSpec((1, D_ATT, tm), lambda i: (i // tps, 0, i % tps))
    tail = pl.BlockSpec((1, 8, CONV_DIM), lambda i: (i // tps, 0, 0))
    sds = jax.ShapeDtypeStruct
    return pl.pallas_call(
        functools.partial(_inproj_aug_body, tiles_per_seq=tps, tm=tm),
        grid=(n // tm,),
        in_specs=[row(D_MODEL), full(norm_g), full(wq), full(wk), full(wv), full(wz), full(wx),
                  full(wsm), full(bias_s), full(eq), full(ek), full(conv_w), full(conv_b)],
        out_specs=[kv_t, kv_t, row(D_ATT), row(D_ATT), row(D_ATT), row(D_ATT), row(D_ATT),
                   row(D_SSM), row(CONV_DIM), tail, row(LANES)],
        out_shape=[
            sds((nseq, D_ATT, seq), F32), sds((nseq, D_ATT, seq), F32),
            sds((n, D_ATT), BF16), sds((n, D_ATT), BF16), sds((n, D_ATT), BF16),
            sds((n, D_ATT), BF16), sds((n, D_ATT), BF16),
            sds((n, D_SSM), F32), sds((n, CONV_DIM), F32), sds((nseq, 8, CONV_DIM), F32), sds((n, LANES), F32)],
        scratch_shapes=[pltpu.VMEM((8, LANES), F32), pltpu.VMEM((8 + tm, CONV_DIM), F32)],
        compiler_params=_cparams(("arbitrary",)),
        name="inproj_prompt",
    )(x1, norm_g, wq, wk, wv, wz, wx, wsm, bias_s, eq, ek, conv_w, conv_b)


def _inproj_plain_body(x_ref, g_ref, wq_ref, wk_ref, wv_ref, wz_ref, wx_ref, ws_ref, bs_ref,
                       q_ref, k_ref, v_ref, z_ref, xbc_ref, slab_ref):
    h = _rms(x_ref[...], g_ref[...]).astype(BF16)
    q_ref[...] = _dot(h, wq_ref[...])
    k_ref[...] = _dot(h, wk_ref[...])
    v_ref[...] = _dot(h, wv_ref[...])
    z_ref[...] = _dot(h, wz_ref[...])
    xbc_ref[...] = _dot(h, wx_ref[...])
    slab_ref[...] = _small_slab(_dot(h, ws_ref[...]), bs_ref[...])


def _inproj_plain(x1, norm_g, ws, bias_s):
    n = x1.shape[0]
    wq, wk, wv, wz, wx, wsm = ws
    full = lambda a: pl.BlockSpec(a.shape, lambda i: (0, 0))
    out = lambda w: pl.BlockSpec((n, w), lambda i: (0, 0))
    return pl.pallas_call(
        _inproj_plain_body,
        grid=(1,),
        in_specs=[full(x1), full(norm_g), full(wq), full(wk), full(wv), full(wz), full(wx),
                  full(wsm), full(bias_s)],
        out_specs=[out(D_ATT), out(D_ATT), out(D_ATT), out(D_SSM), out(CONV_DIM), out(LANES)],
        out_shape=[jax.ShapeDtypeStruct((n, w), F32)
                   for w in (D_ATT, D_ATT, D_ATT, D_SSM, CONV_DIM, LANES)],
        compiler_params=_cparams(("arbitrary",)),
        name="inproj_sample",
    )(x1, norm_g, wq, wk, wv, wz, wx, wsm, bias_s)


def _head_expand_mat():
    e = np.zeros((LANES, D_SSM), np.float32)
    for h in range(N_SSM_HEADS):
        e[DT_LANE0 + h, h * SSM_HEAD_DIM:(h + 1) * SSM_HEAD_DIM] = 1.0
    return jnp.asarray(e, BF16)


def _gate_groupnorm(y, z, gain):
    g = y * _silu(z)
    half = D_SSM // N_SSM_GROUPS
    parts = []
    for gi in range(N_SSM_GROUPS):
        gg = g[:, gi * half:(gi + 1) * half]
        parts.append(gg * lax.rsqrt(jnp.mean(gg * gg, axis=-1, keepdims=True) + EPS))
    return jnp.concatenate(parts, axis=1) * gain


def _ssd_chunk(xc, slab, z, ht_old, arow, dsk, sg, e8):
    Q = SSD_CHUNK
    xs = xc[:, :D_SSM]
    lane = lax.broadcasted_iota(jnp.int32, slab.shape, 1)
    dtm = jnp.where((lane >= DT_LANE0) & (lane < 2 * DT_LANE0), slab, 0.0)
    a = dtm * arow
    a_c = _dot3r(_tril(Q), a)
    a_ct = a_c.T
    a_last = a_c[Q - 1:Q, :]
    ea_x = _dot3(jnp.exp(a_c), e8)
    dec_x = _dot3(jnp.exp(a_last - a_c), e8)
    xdt = xs * _dot3(dtm, e8)
    xdt_b = xdt.astype(BF16)
    xd_b = (xdt * dec_x).astype(BF16)
    cd_x = ea_x[Q - 1:Q, :]

    r_i = lax.broadcasted_iota(jnp.int32, (Q, Q), 0)
    c_i = lax.broadcasted_iota(jnp.int32, (Q, Q), 1)
    tri = c_i <= r_i
    lane_q = lax.broadcasted_iota(jnp.int32, (Q, LANES), 1)
    gw = D_SSM // N_SSM_GROUPS
    hpg = N_SSM_HEADS // N_SSM_GROUPS
    y_parts, ht_parts = [], []
    for g in range(N_SSM_GROUPS):
        bm = xc[:, D_SSM + g * D_STATE:D_SSM + (g + 1) * D_STATE]
        cm = xc[:, D_SSM + (N_SSM_GROUPS + g) * D_STATE:D_SSM + (N_SSM_GROUPS + g + 1) * D_STATE]
        cb16 = cm.astype(BF16)
        cbm = _dot_nt(cb16, bm.astype(BF16))
        y_off = _dot(cb16, ht_old[:, g * gw:(g + 1) * gw].astype(BF16))
        y_diag = []
        for pr in range(hpg // 2):
            halves = []
            for hh in range(2):
                h = g * hpg + pr * 2 + hh
                col = a_c[:, DT_LANE0 + h:DT_LANE0 + h + 1]
                row = a_ct[DT_LANE0 + h:DT_LANE0 + h + 1, :]
                lm = jnp.where(tri, jnp.exp(col - row), 0.0)
                sc = (cbm * lm).astype(BF16)
                lo = g * gw + pr * LANES
                halves.append(_dot(sc, xdt_b[:, lo:lo + LANES]))
            y_diag.append(jnp.where(lane_q < SSM_HEAD_DIM, halves[0], halves[1]))
        y_parts.append(jnp.concatenate(y_diag, axis=1) + y_off * ea_x[:, g * gw:(g + 1) * gw])
        new = _dot(bm.T.astype(BF16), xd_b[:, g * gw:(g + 1) * gw])
        ht_parts.append(ht_old[:, g * gw:(g + 1) * gw] * cd_x[:, g * gw:(g + 1) * gw] + new)
    y = jnp.concatenate(y_parts, axis=1) + dsk * xs
    return _gate_groupnorm(y, z, sg).astype(BF16), jnp.concatenate(ht_parts, axis=1)


def _ssd_body(xc_ref, slab_ref, z_ref, arow_ref, dsk_ref, sg_ref, e8_ref, gn_ref, st_ref, ht_scr,
              *, n_steps, cps):
    c = pl.program_id(1)
    Q = SSD_CHUNK

    @pl.when(c == 0)
    def _():
        ht_scr[...] = jnp.zeros_like(ht_scr)

    ht = ht_scr[...]
    for ci in range(cps):
        rows = slice(ci * Q, (ci + 1) * Q)
        gn, ht = _ssd_chunk(xc_ref[0, rows, :], slab_ref[0, rows, :], z_ref[0, rows, :], ht,
                            arow_ref[...], dsk_ref[...], sg_ref[...], e8_ref[...])
        gn_ref[0, rows, :] = gn
    ht_scr[...] = ht

    @pl.when(c == n_steps - 1)
    def _():
        st_ref[0] = ht.T


def _ssd_prompt(xc, slab, z, a_row, dsk_x, ssm_g, *, cps):
    b, L, _ = xc.shape
    rows = cps * SSD_CHUNK
    ns = L // rows
    e8 = _head_expand_mat()
    blk = lambda w: pl.BlockSpec((1, rows, w), lambda bi, ci: (bi, ci, 0))
    full = lambda a: pl.BlockSpec(a.shape, lambda bi, ci: (0, 0))
    return pl.pallas_call(
        functools.partial(_ssd_body, n_steps=ns, cps=cps),
        grid=(b, ns),
        in_specs=[blk(CONV_DIM), blk(LANES), blk(D_SSM), full(a_row), full(dsk_x), full(ssm_g), full(e8)],
        out_specs=[blk(D_SSM), pl.BlockSpec((1, D_SSM, D_STATE), lambda bi, ci: (bi, 0, 0))],
        out_shape=[jax.ShapeDtypeStruct((b, L, D_SSM), BF16),
                   jax.ShapeDtypeStruct((b, D_SSM, D_STATE), F32)],
        scratch_shapes=[pltpu.VMEM((D_STATE, D_SSM), F32)],
        compiler_params=_cparams(("arbitrary", "arbitrary")),
        name="ssd_prompt",
    )(xc, slab, z, a_row, dsk_x, ssm_g, e8)


def _ssd_step_body(xbc_ref, sc_ref, slab_ref, z_ref, st_ref, cw_ref, cb_ref, arow_ref, dsk_ref,
                   sg_ref, e8_ref, gn_ref, so_ref, xc_scr, dtx_scr, decx_scr, *, spb):
    step = pl.program_id(0)

    @pl.when(step == 0)
    def _():
        conv = (cb_ref[...] + cw_ref[0:1, :] * sc_ref[0] + cw_ref[1:2, :] * sc_ref[1]
                + cw_ref[2:3, :] * sc_ref[2] + cw_ref[3:4, :] * xbc_ref[...])
        xc_scr[...] = _silu(conv)
        slab = slab_ref[...]
        lane = lax.broadcasted_iota(jnp.int32, slab.shape, 1)
        dtm = jnp.where((lane >= DT_LANE0) & (lane < 2 * DT_LANE0), slab, 0.0)
        e8 = e8_ref[...]
        dtx_scr[...] = _dot3(dtm, e8)
        decx_scr[...] = _dot3(jnp.exp(dtm * arow_ref[...]), e8)

    gw = D_SSM // N_SSM_GROUPS
    for i in range(spb):
        b = step * spb + i
        xrow = xc_scr[pl.ds(b, 1), :]
        xs = xrow[:, :D_SSM]
        xdt = xs * dtx_scr[pl.ds(b, 1), :]
        dec = decx_scr[pl.ds(b, 1), :]
        y_rows = []
        for j in range(D_SSM // LANES):
            g = (j * LANES) // gw
            bm = xrow[:, D_SSM + g * D_STATE:D_SSM + (g + 1) * D_STATE]
            cm = xrow[:, D_SSM + (N_SSM_GROUPS + g) * D_STATE:D_SSM + (N_SSM_GROUPS + g + 1) * D_STATE]
            xcol = _row_to_col(xdt[:, j * LANES:(j + 1) * LANES])
            dcol = _row_to_col(dec[:, j * LANES:(j + 1) * LANES])
            hs = dcol * st_ref[i, j * LANES:(j + 1) * LANES, :] + xcol * bm
            so_ref[i, j * LANES:(j + 1) * LANES, :] = hs
            y_rows.append(_col_to_row(jnp.sum(hs * cm, axis=1, keepdims=True)))
        y = jnp.concatenate(y_rows, axis=1) + dsk_ref[...] * xs
        gn_ref[i] = _gate_groupnorm(y, z_ref[pl.ds(b, 1), :], sg_ref[...]).astype(BF16)


def _ssd_step(xbc, sconv_t, slab, z, state, conv_w, conv_b, a_row, dsk_x, ssm_g, *, spb):
    nb = xbc.shape[0]
    assert nb % spb == 0
    e8 = _head_expand_mat()
    full2 = lambda a: pl.BlockSpec(a.shape, lambda bi: (0,) * a.ndim)
    gn, st = pl.pallas_call(
        functools.partial(_ssd_step_body, spb=spb),
        grid=(nb // spb,),
        in_specs=[full2(xbc), full2(sconv_t), full2(slab), full2(z),
                  pl.BlockSpec((spb, D_SSM, D_STATE), lambda bi: (bi, 0, 0)),
                  full2(conv_w), full2(conv_b), full2(a_row), full2(dsk_x), full2(ssm_g), full2(e8)],
        out_specs=[pl.BlockSpec((spb, 1, D_SSM), lambda bi: (bi, 0, 0)),
                   pl.BlockSpec((spb, D_SSM, D_STATE), lambda bi: (bi, 0, 0))],
        out_shape=[jax.ShapeDtypeStruct((nb, 1, D_SSM), BF16),
                   jax.ShapeDtypeStruct((nb, D_SSM, D_STATE), F32)],
        scratch_shapes=[pltpu.VMEM((nb, CONV_DIM), F32), pltpu.VMEM((nb, D_SSM), F32),
                        pltpu.VMEM((nb, D_SSM), F32)],
        compiler_params=_cparams(("arbitrary",)),
        name="ssd_step",
    )(xbc, sconv_t, slab, z, state, conv_w, conv_b, a_row, dsk_x, ssm_g, e8)
    return gn.reshape(nb, D_SSM), st


DECODE_GROUP = 16
DECODE_SLOTS = 3


def _fox_body(pt_ref, q_ref, k_ref, v_ref, aq_ref, ak_ref, qs_ref, ks_ref, vs_ref, lfs_ref,
              ck_hbm, cv_hbm, clf_hbm, o_ref, os_ref,
              s_scr, m_scr, acc_scr, qp_scr, kbuf, vbuf, lbuf, sems, qblk_scr, dm_scr, dl_scr, dacc_scr,
              carry_scr, *, tq, nt, gps, gpseq, n_groups):
    G, NS = DECODE_GROUP, DECODE_SLOTS
    H, HD, P = N_ATT_HEADS, HEAD_DIM, PAGE_SIZE
    t = pl.program_id(2)
    sid = (pl.program_id(0) * pl.num_programs(1) + pl.program_id(1)) * pl.num_programs(2) + t
    nl = tq // LANES

    def group_copies(gg):
        slot = gg % NS
        src = jnp.minimum(gg, n_groups - 1)
        seq = src // gpseq
        base = (gpseq - 1 - src % gpseq) * G
        copies = []
        for i in range(G):
            page = pt_ref[seq, base + i]
            copies.append(pltpu.make_async_copy(ck_hbm.at[page], kbuf.at[slot, i], sems.at[slot, 0]))
            copies.append(pltpu.make_async_copy(cv_hbm.at[page], vbuf.at[slot, i], sems.at[slot, 1]))
            copies.append(pltpu.make_async_copy(clf_hbm.at[page], lbuf.at[slot, i], sems.at[slot, 2]))
        return copies

    def start_group(gg):
        for n, cp in enumerate(group_copies(gg)):
            cp.start(priority=1 if n % 3 == 1 else 0)

    def wait_group(gg):
        for cp in group_copies(gg):
            cp.wait()

    r8 = lax.broadcasted_iota(jnp.int32, (H, D_ATT), 0)
    c8 = lax.broadcasted_iota(jnp.int32, (H, D_ATT), 1)
    own_head = c8 // HD == r8

    def decode_init():
        seq = (sid * gps) // gpseq
        qs = (qs_ref[seq] * ATT_SCALE).astype(BF16).astype(F32)
        qblk = jnp.where(own_head, qs, 0.0)
        qblk_scr[...] = qblk.astype(BF16)
        k2 = ks_ref[seq].astype(BF16).astype(F32)
        dm_scr[...] = jnp.broadcast_to(jnp.sum(qblk * k2, axis=1, keepdims=True), dm_scr.shape)
        dl_scr[...] = jnp.ones_like(dl_scr)
        dacc_scr[...] = jnp.broadcast_to(vs_ref[seq].astype(BF16).astype(F32), dacc_scr.shape)
        carry_scr[...] = jnp.broadcast_to(_row_to_col(lfs_ref[seq])[0:H, :], carry_scr.shape)

    def decode_group(slot):
        kcat = jnp.concatenate([kbuf[slot, i].astype(BF16) for i in range(G)], axis=1)
        vcat = jnp.concatenate([vbuf[slot, i].astype(BF16) for i in range(G)], axis=1)
        s = _dot(qblk_scr[...], kcat)
        x = jnp.concatenate([lbuf[slot, i] for i in range(G)], axis=0)
        rr = lax.broadcasted_iota(jnp.int32, (P, P), 0)
        cc = lax.broadcasted_iota(jnp.int32, (P, P), 1)
        rev_local = _dot3(x, (rr > cc).astype(BF16))
        tot = jnp.sum(x, axis=1, keepdims=True)
        carry = carry_scr[:, 0:1]
        s_pages = [None] * G
        for i in reversed(range(G)):
            s_pages[i] = s[:, i * P:(i + 1) * P] + rev_local[i * H:(i + 1) * H, :] + carry
            carry = carry + tot[i * H:(i + 1) * H, :]
        carry_scr[...] = jnp.broadcast_to(carry, carry_scr.shape)
        st = jnp.concatenate(s_pages, axis=1)
        m_old = dm_scr[:, 0:1]
        m_new = jnp.maximum(m_old, jnp.max(st, axis=1, keepdims=True))
        alpha = jnp.exp(m_old - m_new)
        p = jnp.exp(st - m_new)
        l_new = alpha * dl_scr[:, 0:1] + jnp.sum(p, axis=1, keepdims=True)
        dacc_scr[...] = alpha * dacc_scr[...] + _dot_nt(p.astype(BF16), vcat)
        dm_scr[...] = jnp.broadcast_to(m_new, dm_scr.shape)
        dl_scr[...] = jnp.broadcast_to(l_new, dl_scr.shape)

    def decode_finish():
        o = dacc_scr[...] / dl_scr[:, 0:1]
        os_ref[0] = jnp.sum(jnp.where(own_head, o, 0.0), axis=0, keepdims=True)

    r = lax.broadcasted_iota(jnp.int32, (tq, tq), 0)
    c = lax.broadcasted_iota(jnp.int32, (tq, tq), 1)
    causal = c <= r

    def chunk_of(ci):
        first = ci <= t
        sel = jnp.where(first, 0, 1)
        q0 = pl.multiple_of(jnp.where(first, t, nt - 1 - t) * tq, tq)
        k0 = pl.multiple_of(jnp.where(first, ci, ci - t - 1) * tq, tq)
        return sel, q0, k0

    lane_q = lax.broadcasted_iota(jnp.int32, (tq, LANES), 1)
    even_dims = lane_q < HD

    def per_head(x2, a2):
        return jnp.where(even_dims, x2, a2), jnp.where(even_dims, a2, x2)

    def build_queries():
        for x, tile in enumerate((t, nt - 1 - t)):
            rows = pl.ds(pl.multiple_of(tile * tq, tq), tq)
            q_e, q_o = per_head(q_ref[0, rows, :], aq_ref[0, rows, :])
            qp_scr[x, 0] = q_e
            qp_scr[x, 1] = q_o

    def scores(ci):
        sel, q0, k0 = chunk_of(ci)
        ks = per_head(k_ref[0, pl.ds(k0, tq), :], ak_ref[0, pl.ds(k0, tq), :])
        for j in range(2):
            s = _dot_nt(qp_scr[sel, j], ks[j])
            if ci == nt:
                s = jnp.where(causal, s, NEG_BIG)
            elif ci < nt // 2:
                s = jnp.where(jnp.logical_or(causal, ci != t), s, NEG_BIG)
            s_scr[j, ci] = s
            m = m_scr[sel, j]
            for u in range(nl):
                m = jnp.maximum(m, s[:, u * LANES:(u + 1) * LANES])
            m_scr[sel, j] = m

    def row_max():
        for x in range(2):
            for j in range(2):
                m_scr[x, j] = jnp.broadcast_to(jnp.max(m_scr[x, j], axis=1, keepdims=True), (tq, LANES))

    one_at = lambda ln: jnp.where(lane_q == ln, 1.0, 0.0).astype(BF16)
    ones_e, ones_o = one_at(HD), one_at(0)

    def weighted(ci):
        sel, q0, k0 = chunk_of(ci)
        v2 = v_ref[0, pl.ds(k0, tq), :]
        vs = (jnp.where(even_dims, v2, ones_e), jnp.where(even_dims, ones_o, v2))
        for j in range(2):
            s = s_scr[j, ci]
            mrep = m_scr[sel, j]
            p = jnp.concatenate([jnp.exp2(s[:, u * LANES:(u + 1) * LANES] - mrep) for u in range(nl)], axis=1)
            acc_scr[sel, j] += _dot(p.astype(BF16), vs[j])

    def write_out():
        lane = lax.broadcasted_iota(jnp.int32, (tq, LANES), 1)
        for x, tile in enumerate((t, nt - 1 - t)):
            acc_e, acc_o = acc_scr[x, 0], acc_scr[x, 1]
            out = jnp.where(lane < HEAD_DIM, acc_e / acc_e[:, HEAD_DIM:HEAD_DIM + 1], acc_o / acc_o[:, 0:1])
            o_ref[0, pl.ds(pl.multiple_of(tile * tq, tq), tq), :] = out

    items = ([functools.partial(scores, ci) for ci in range(nt + 1)] + [row_max]
             + [functools.partial(weighted, ci) for ci in range(nt + 1)] + [write_out])
    per_part = -(-len(items) // gps)
    parts = [items[i * per_part:(i + 1) * per_part] for i in range(gps)]

    @pl.when(sid == 0)
    def _():
        for g0 in range(NS - 1):
            start_group(g0)

    pl.when((sid * gps) % gpseq == 0)(decode_init)
    m_scr[...] = jnp.full(m_scr.shape, NEG_BIG, F32)
    acc_scr[...] = jnp.zeros_like(acc_scr)
    build_queries()
    for gi in range(gps):
        gg = sid * gps + gi
        wait_group(gg)
        start_group(gg + (NS - 1))
        for item in parts[gi]:
            item()
        decode_group(gg % NS)
    pl.when((sid * gps + gps - 1) % gpseq == gpseq - 1)(decode_finish)

    @pl.when(sid == pl.num_programs(0) * pl.num_programs(1) * pl.num_programs(2) - 1)
    def _():
        for extra in range(NS - 1):
            wait_group(n_groups + extra)


def _fox_attention(q16, k16, v16, aq, ak, q_s, k_s, v_s, slab_s, cache_k, cache_v, cache_logf, page_table,
                   *, tq):
    b, L, _ = q16.shape
    nt = L // tq
    nb, n_pages = page_table.shape
    n_steps = b * (N_ATT_HEADS // 2) * (nt // 2)
    n_groups = nb * n_pages // DECODE_GROUP
    gps = n_groups // n_steps
    gpseq = n_pages // DECODE_GROUP
    assert nt % 2 == 0 and n_pages % DECODE_GROUP == 0
    assert gps * n_steps == n_groups and gpseq % gps == 0 and n_groups >= DECODE_SLOTS
    spq = gpseq // gps
    hp_n, t_n = N_ATT_HEADS // 2, nt // 2
    seq_of = lambda bi, hp, t: ((bi * hp_n + hp) * t_n + t) // spq
    pair = pl.BlockSpec((1, L, LANES), lambda bi, hp, t, pt: (bi, 0, hp), pipeline_mode=pl.Buffered(1))
    srow = lambda w: pl.BlockSpec((nb, 1, w), lambda bi, hp, t, pt: (0, 0, 0), pipeline_mode=pl.Buffered(1))
    anyspec = pl.BlockSpec(memory_space=pl.ANY)
    row3 = lambda a: a.reshape(nb, 1, a.shape[-1])
    G, NS = DECODE_GROUP, DECODE_SLOTS
    att, att_s = pl.pallas_call(
        functools.partial(_fox_body, tq=tq, nt=nt, gps=gps, gpseq=gpseq, n_groups=n_groups),
        grid_spec=pltpu.PrefetchScalarGridSpec(
            num_scalar_prefetch=1,
            grid=(b, hp_n, t_n),
            in_specs=[pair, pair, pair, pair, pair, srow(D_ATT), srow(D_ATT), srow(D_ATT), srow(LANES),
                      anyspec, anyspec, anyspec],
            out_specs=[pl.BlockSpec((1, L, LANES), lambda bi, hp, t, pt: (bi, 0, hp)),
                       pl.BlockSpec((1, 1, D_ATT), lambda bi, hp, t, pt: (seq_of(bi, hp, t), 0, 0))],
            scratch_shapes=[
                pltpu.VMEM((2, nt + 1, tq, tq), F32), pltpu.VMEM((2, 2, tq, LANES), F32),
                pltpu.VMEM((2, 2, tq, LANES), F32), pltpu.VMEM((2, 2, tq, LANES), BF16),
                pltpu.VMEM((NS, G, D_ATT, PAGE_SIZE), F32), pltpu.VMEM((NS, G, D_ATT, PAGE_SIZE), F32),
                pltpu.VMEM((NS, G, N_ATT_HEADS, PAGE_SIZE), F32), pltpu.SemaphoreType.DMA((NS, 3)),
                pltpu.VMEM((N_ATT_HEADS, D_ATT), BF16), pltpu.VMEM((8, LANES), F32),
                pltpu.VMEM((8, LANES), F32), pltpu.VMEM((8, D_ATT), F32), pltpu.VMEM((8, LANES), F32)]),
        out_shape=[jax.ShapeDtypeStruct((b, L, D_ATT), F32), jax.ShapeDtypeStruct((nb, 1, D_ATT), F32)],
        compiler_params=_cparams(("arbitrary", "arbitrary", "arbitrary")),
        name="fox_attention",
    )(page_table, q16, k16, v16, aq, ak, row3(q_s), row3(k_s), row3(v_s), row3(slab_s),
      cache_k, cache_v, cache_logf)
    return att, att_s.reshape(nb, D_ATT)


def _lane_pad(vec, lane0):
    out = jnp.zeros((1, LANES), F32)
    return out.at[0, lane0:lane0 + vec.shape[0]].set(vec.astype(F32))


def kernel(x_prompt, x_sample, cache_k, cache_v, cache_logf, page_table, state_conv, state_ssm,
           ffn1_norm, w_ffn1_in, w_ffn1_out, mix_norm, w_in, b_f, conv_w, conv_b, dt_bias, a_log,
           d_skip, att_out_norm, ssm_out_norm, w_out, ffn2_norm, w_ffn2_in, w_ffn2_out, final_norm):
    depth = w_in.shape[0]
    assert depth == 1
    B, L, _ = x_prompt.shape
    nb, T, _ = x_sample.shape
    assert T == 1
    H, HD = N_ATT_HEADS, HEAD_DIM

    l0 = 0
    w1i, w1o = w_ffn1_in[l0].astype(BF16), w_ffn1_out[l0].astype(BF16)
    w2i, w2o = w_ffn2_in[l0].astype(BF16), w_ffn2_out[l0].astype(BF16)
    wi = w_in[l0]
    o_f = 3 * D_ATT
    o_z = o_f + H
    o_x = o_z + D_SSM
    o_dt = o_x + CONV_DIM
    w_small = jnp.zeros((D_MODEL, LANES), F32)
    w_small = w_small.at[:, 0:H].set(wi[:, o_f:o_f + H])
    w_small = w_small.at[:, DT_LANE0:DT_LANE0 + N_SSM_HEADS].set(wi[:, o_dt:o_dt + N_SSM_HEADS])
    ws = tuple(w.astype(BF16) for w in (wi[:, 0:D_ATT], wi[:, D_ATT:2 * D_ATT], wi[:, 2 * D_ATT:3 * D_ATT],
                                        wi[:, o_z:o_z + D_SSM], wi[:, o_x:o_x + CONV_DIM], w_small))
    bias_s = _lane_pad(b_f[l0], 0) + _lane_pad(dt_bias[l0], DT_LANE0)
    a_row = _lane_pad(-jnp.exp(a_log[l0].astype(F32)), DT_LANE0)
    dsk_x = jnp.repeat(d_skip[l0].astype(F32), SSM_HEAD_DIM)[None, :]
    row = lambda v: v.astype(F32)[None, :]
    g1, gm, g2, gf = row(ffn1_norm[l0]), row(mix_norm[l0]), row(ffn2_norm[l0]), row(final_norm)
    ga, gs = row(att_out_norm[l0]), row(ssm_out_norm[l0])
    cw, cb = conv_w[l0].astype(F32), row(conv_b[l0])
    wm = w_out[l0].astype(BF16)

    TM, TF = FFN_ROWS, FFN_CHUNK
    xp = x_prompt.reshape(B * L, D_MODEL)
    x1 = _ffn(xp, g1, w1i, w1o, tm=TM, tf=TF)
    (k_p, v_p, q16, k16, v16, aq, ak, z_p, xc_p, tail_p, slab_p) = _inproj_aug(
        x1, gm, ws, bias_s, cw, cb, tm=INPROJ_ROWS, seq=L)
    xs = x_sample.reshape(nb, D_MODEL)
    x1s = _ffn(xs, g1, w1i, w1o, tm=nb, tf=TF)
    q_s, k_s, v_s, z_s, xbc_s, slab_s = _inproj_plain(x1s, gm, ws, bias_s)

    n_pool = cache_k.shape[1]
    ck = jnp.transpose(cache_k[l0], (0, 2, 3, 1)).reshape(n_pool, D_ATT, PAGE_SIZE)
    cv = jnp.transpose(cache_v[l0], (0, 2, 3, 1)).reshape(n_pool, D_ATT, PAGE_SIZE)
    clf = jnp.transpose(cache_logf[l0], (0, 2, 1))
    seq3 = lambda a: a.reshape(B, L, a.shape[-1])
    att, att_s = _fox_attention(seq3(q16), seq3(k16), seq3(v16), seq3(aq), seq3(ak),
                                q_s, k_s, v_s, slab_s, ck, cv, clf, page_table, tq=ATT_ROWS)

    gn_p, st_p = _ssd_prompt(seq3(xc_p), seq3(slab_p), seq3(z_p), a_row, dsk_x, gs, cps=SSD_CHUNKS_PER_STEP)
    y_p = _ffn2(x1, att.reshape(B * L, D_ATT), gn_p.reshape(B * L, D_SSM), ga, wm, g2, w2i, w2o, gf,
                tm=TM, tf=TF)

    sconv_t = jnp.swapaxes(state_conv[l0], 0, 1)
    gn_s, st_s = _ssd_step(xbc_s, sconv_t, slab_s, z_s,
                           state_ssm[l0].reshape(nb, D_SSM, D_STATE), cw, cb, a_row, dsk_x, gs,
                           spb=SSD_STEP_SEQS if nb % SSD_STEP_SEQS == 0 else 1)
    y_s = _ffn2(x1s, att_s, gn_s, ga, wm, g2, w2i, w2o, gf, tm=nb, tf=TF)

    return (
        y_p.reshape(B, L, D_MODEL),
        y_s.reshape(nb, 1, D_MODEL),
        jnp.transpose(k_p.reshape(1, B, H, HD, L), (0, 1, 4, 2, 3)),
        jnp.transpose(v_p.reshape(1, B, H, HD, L), (0, 1, 4, 2, 3)),
        slab_p[:, :H].reshape(1, B, L, H),
        tail_p[:, 8 - (CONV_W - 1):, :][None],
        st_p.reshape(1, B, N_SSM_HEADS, SSM_HEAD_DIM, D_STATE),
        k_s.reshape(1, nb, 1, H, HD),
        v_s.reshape(1, nb, 1, H, HD),
        slab_s[:, :H].reshape(1, nb, 1, H),
        jnp.concatenate([state_conv[l0][:, 1:, :], xbc_s[:, None, :]], axis=1)[None],
        st_s.reshape(1, nb, N_SSM_HEADS, SSM_HEAD_DIM, D_STATE),
    )
```

```python
import functools

import numpy as np
import jax
import jax.numpy as jnp
from jax import lax
from jax.experimental import pallas as pl
from jax.experimental.pallas import tpu as pltpu

F32 = jnp.float32
BF16 = jnp.bfloat16

D_MODEL = 1024
D_ATT = 512
D_SSM = 512
HEAD_DIM = 64
N_ATT_HEADS = 8
N_SSM_HEADS = 8
SSM_HEAD_DIM = 64
N_SSM_GROUPS = 2
D_STATE = 128
CONV_W = 4
CONV_DIM = D_SSM + 2 * N_SSM_GROUPS * D_STATE
SSD_CHUNK = 128
PAGE_SIZE = 128
D_FF = 2816
FFN_RESIDUAL = 0.5
EPS = 1e-6
ATT_SCALE = HEAD_DIM ** -0.5
LOG2E = 1.4426950408889634
LANES = 128
DT_LANE0 = 8
NEG_BIG = -1e30

VMEM_LIMIT = 56 * 1024 * 1024
FFN_ROWS = 1024
FFN_CHUNK = 256
INPROJ_ROWS = 512
ATT_ROWS = 256
SSD_CHUNKS_PER_STEP = 8
SSD_STEP_SEQS = 4


def _cparams(sem):
    return pltpu.CompilerParams(dimension_semantics=sem, vmem_limit_bytes=VMEM_LIMIT)


def _dot(a, b):
    return jnp.dot(a, b, preferred_element_type=F32)


def _dot_nt(a, b):
    return lax.dot_general(a, b, (((1,), (1,)), ((), ())), preferred_element_type=F32)


def _split3(x):
    hi = x.astype(BF16)
    r1 = x - hi.astype(F32)
    mid = r1.astype(BF16)
    lo = (r1 - mid.astype(F32)).astype(BF16)
    return hi, mid, lo


def _dot3(x, m):
    hi, mid, lo = _split3(x)
    return _dot(hi, m) + _dot(mid, m) + _dot(lo, m)


def _dot3r(m, x):
    hi, mid, lo = _split3(x)
    return _dot(m, hi) + _dot(m, mid) + _dot(m, lo)


def _rms(x, g):
    return x * lax.rsqrt(jnp.mean(x * x, axis=-1, keepdims=True) + EPS) * g


def _silu(x):
    return x / (1.0 + jnp.exp(-x))


def _tril(n, dtype=BF16):
    r = lax.broadcasted_iota(jnp.int32, (n, n), 0)
    c = lax.broadcasted_iota(jnp.int32, (n, n), 1)
    return (c <= r).astype(dtype)


def _row_to_col(row):
    n = row.shape[1]
    r = lax.broadcasted_iota(jnp.int32, (n, n), 0)
    c = lax.broadcasted_iota(jnp.int32, (n, n), 1)
    return jnp.sum(jnp.where(r == c, jnp.broadcast_to(row, (n, n)), 0.0), axis=1, keepdims=True)


def _col_to_row(col):
    n = col.shape[0]
    r = lax.broadcasted_iota(jnp.int32, (n, n), 0)
    c = lax.broadcasted_iota(jnp.int32, (n, n), 1)
    return jnp.sum(jnp.where(r == c, jnp.broadcast_to(col, (n, n)), 0.0), axis=0, keepdims=True)


def _resident(shape):
    return pl.BlockSpec(shape, lambda i: (0,) * len(shape), pipeline_mode=pl.Buffered(1))


def _swiglu_half(x, g_ref, wi_ref, wo_ref, a_scr, tf):
    h = _rms(x, g_ref[...]).astype(BF16)
    for c in range(D_FF // tf):
        gate = _dot(h, wi_ref[:, c * tf:(c + 1) * tf])
        up = _dot(h, wi_ref[:, D_FF + c * tf:D_FF + (c + 1) * tf])
        a_scr[:, c * tf:(c + 1) * tf] = (_silu(gate) * up).astype(BF16)
    return x + FFN_RESIDUAL * _dot(a_scr[...], wo_ref[...])


def _ffn_body(x_ref, g_ref, wi_ref, wo_ref, o_ref, a_scr, *, tf):
    o_ref[...] = _swiglu_half(x_ref[...], g_ref, wi_ref, wo_ref, a_scr, tf)


def _ffn(x, norm_g, w_in_b, w_out_b, *, tm, tf):
    n = x.shape[0]
    row = lambda w: pl.BlockSpec((tm, w), lambda i: (i, 0))
    return pl.pallas_call(
        functools.partial(_ffn_body, tf=tf),
        grid=(n // tm,),
        in_specs=[row(D_MODEL), _resident(norm_g.shape), _resident(w_in_b.shape), _resident(w_out_b.shape)],
        out_specs=row(D_MODEL),
        out_shape=jax.ShapeDtypeStruct((n, D_MODEL), F32),
        scratch_shapes=[pltpu.VMEM((tm, D_FF), BF16)],
        compiler_params=_cparams(("arbitrary",)),
        name="ffn1",
    )(x, norm_g, w_in_b, w_out_b)


def _ffn2_body(x_ref, att_ref, gn_ref, ag_ref, wm_ref, g_ref, wi_ref, wo_ref, fg_ref, o_ref, a_scr, *, tf):
    an = _rms(att_ref[...], ag_ref[...]).astype(BF16)
    x2 = x_ref[...] + _dot(an, wm_ref[:D_ATT, :]) + _dot(gn_ref[...], wm_ref[D_ATT:, :])
    o_ref[...] = _rms(_swiglu_half(x2, g_ref, wi_ref, wo_ref, a_scr, tf), fg_ref[...])


def _ffn2(x1, att, gn, att_g, w_mix_b, norm_g, w_in_b, w_out_b, final_g, *, tm, tf):
    n = x1.shape[0]
    row = lambda w: pl.BlockSpec((tm, w), lambda i: (i, 0))
    return pl.pallas_call(
        functools.partial(_ffn2_body, tf=tf),
        grid=(n // tm,),
        in_specs=[row(D_MODEL), row(D_ATT), row(D_SSM), _resident(att_g.shape), _resident(w_mix_b.shape),
                  _resident(norm_g.shape), _resident(w_in_b.shape), _resident(w_out_b.shape),
                  _resident(final_g.shape)],
        out_specs=row(D_MODEL),
        out_shape=jax.ShapeDtypeStruct((n, D_MODEL), F32),
        scratch_shapes=[pltpu.VMEM((tm, D_FF), BF16)],
        compiler_params=_cparams(("arbitrary",)),
        name="ffn2",
    )(x1, att, gn, att_g, w_mix_b, norm_g, w_in_b, w_out_b, final_g)


def _small_slab(raw, bias):
    v = raw + bias
    t = jnp.log1p(jnp.exp(-jnp.abs(v)))
    lane = lax.broadcasted_iota(jnp.int32, v.shape, 1)
    logf = -(jnp.maximum(-v, 0.0) + t)
    dt = jnp.maximum(v, 0.0) + t
    return jnp.where(lane < DT_LANE0, logf, jnp.where(lane < 2 * DT_LANE0, dt, 0.0))


def _inproj_aug_body(x_ref, g_ref, wq_ref, wk_ref, wv_ref, wz_ref, wx_ref, ws_ref, bs_ref,
                     eq_ref, ek_ref, cw_ref, cb_ref, wa_ref, wb_ref,
                     k_ref, v_ref, q16_ref, k16_ref, v16_ref, aq_ref, ak_ref, z_ref, xc_ref, tail_ref, slab_ref,
                     wa16_ref, wb16_ref, carry_scr, conv_scr, *, tiles_per_seq, tm):
    i = pl.program_id(0)
    wa16_ref[...] = wa_ref[...].astype(BF16)
    wb16_ref[...] = wb_ref[...].astype(BF16)

    @pl.when(i % tiles_per_seq == 0)
    def _():
        carry_scr[...] = jnp.zeros_like(carry_scr)
        conv_scr[0:8, :] = jnp.zeros((8, CONV_DIM), F32)

    h = _rms(x_ref[...], g_ref[...]).astype(BF16)
    q = _dot(h, wq_ref[...])
    k = _dot(h, wk_ref[...])
    v = _dot(h, wv_ref[...])
    k_ref[0] = k.T
    v_ref[0] = v.T
    q16_ref[...] = (q * (ATT_SCALE * LOG2E)).astype(BF16)
    k16_ref[...] = k.astype(BF16)
    v16_ref[...] = v.astype(BF16)
    z_ref[...] = _dot(h, wz_ref[...])
    slab = _small_slab(_dot(h, ws_ref[...]), bs_ref[...])
    slab_ref[...] = slab

    u = _dot(h, wx_ref[...])
    conv_scr[8:8 + tm, :] = u
    conv = (cb_ref[...] + cw_ref[0:1, :] * conv_scr[5:5 + tm, :] + cw_ref[1:2, :] * conv_scr[6:6 + tm, :]
            + cw_ref[2:3, :] * conv_scr[7:7 + tm, :] + cw_ref[3:4, :] * u)
    conv_scr[0:8, :] = u[tm - 8:tm, :]
    tail_ref[0] = u[tm - 8:tm, :]
    xc_ref[...] = _silu(conv)

    lane = lax.broadcasted_iota(jnp.int32, slab.shape, 1)
    logf = jnp.where(lane < DT_LANE0, slab, 0.0)
    c = _dot3r(_tril(tm), logf) + carry_scr[0:1, :]
    carry_scr[0:1, :] = c[tm - 1:tm, :]
    c2 = c * LOG2E
    c_hi = c2.astype(BF16).astype(F32)
    r1 = c2 - c_hi
    c_mid = r1.astype(BF16).astype(F32)
    c_lo = (r1 - c_mid).astype(BF16).astype(F32)
    aug = (c_hi + pltpu.roll(c_mid, DT_LANE0, 1) + pltpu.roll(c_lo, 2 * DT_LANE0, 1)
           + jnp.where(lane == 3 * DT_LANE0, 1.0, 0.0)).astype(BF16)
    aq_ref[...] = _dot(aug, eq_ref[...]).astype(BF16)
    ak_ref[...] = _dot(aug, ek_ref[...]).astype(BF16)


def _aug_scatter_mats():
    n_pairs = N_ATT_HEADS // 2
    eq = np.zeros((LANES, n_pairs * LANES), np.float32)
    ek = np.zeros((LANES, n_pairs * LANES), np.float32)
    one_lane = 3 * DT_LANE0
    for h in range(N_ATT_HEADS):
        base = (h // 2) * LANES + (0 if h % 2 else HEAD_DIM)
        for p in range(3):
            eq[p * DT_LANE0 + h, base + p] = 1.0
            eq[one_lane, base + 3 + p] = 1.0
            ek[one_lane, base + p] = 1.0
            ek[p * DT_LANE0 + h, base + 3 + p] = -1.0
    return jnp.asarray(eq, BF16), jnp.asarray(ek, BF16)


def _slab_rows(rows, n_steps):
    rb = -(-rows // n_steps)
    rb += -rb % 16
    while rows % rb:
        rb += 16
    return rb


def _inproj_aug(x1, norm_g, ws, bias_s, conv_w, conv_b, w_a, w_b, *, tm, seq):
    n = x1.shape[0]
    n_steps = n // tm

    def cast_spec(w):
        rb = _slab_rows(w.shape[0], n_steps)
        last = w.shape[0] // rb - 1
        return pl.BlockSpec((rb, w.shape[1]), lambda i: (jnp.minimum(i, last), 0))

    eq, ek = _aug_scatter_mats()
    row = lambda w: pl.BlockSpec((tm, w), lambda i: (i, 0))
    full = lambda a: pl.BlockSpec(a.shape, lambda i: (0, 0))
    wq, wk, wv, wz, wx, wsm = ws
    tps = seq // tm
    nseq = n // seq
    kv_t = pl.BlockSpec((1, D_ATT, tm), lambda i: (i // tps, 0, i % tps))
    tail = pl.BlockSpec((1, 8, CONV_DIM), lambda i: (i // tps, 0, 0))
    sds = jax.ShapeDtypeStruct
    return pl.pallas_call(
        functools.partial(_inproj_aug_body, tiles_per_seq=tps, tm=tm),
        grid=(n // tm,),
        in_specs=[row(D_MODEL), full(norm_g), full(wq), full(wk), full(wv), full(wz), full(wx),
                  full(wsm), full(bias_s), full(eq), full(ek), full(conv_w), full(conv_b),
                  cast_spec(w_a), cast_spec(w_b)],
        out_specs=[kv_t, kv_t, row(D_ATT), row(D_ATT), row(D_ATT), row(D_ATT), row(D_ATT),
                   row(D_SSM), row(CONV_DIM), tail, row(LANES), cast_spec(w_a), cast_spec(w_b)],
        out_shape=[
            sds((nseq, D_ATT, seq), F32), sds((nseq, D_ATT, seq), F32),
            sds((n, D_ATT), BF16), sds((n, D_ATT), BF16), sds((n, D_ATT), BF16),
            sds((n, D_ATT), BF16), sds((n, D_ATT), BF16),
            sds((n, D_SSM), F32), sds((n, CONV_DIM), F32), sds((nseq, 8, CONV_DIM), F32), sds((n, LANES), F32),
            sds(w_a.shape, BF16), sds(w_b.shape, BF16)],
        scratch_shapes=[pltpu.VMEM((8, LANES), F32), pltpu.VMEM((8 + tm, CONV_DIM), F32)],
        compiler_params=_cparams(("arbitrary",)),
        name="inproj_prompt",
    )(x1, norm_g, wq, wk, wv, wz, wx, wsm, bias_s, eq, ek, conv_w, conv_b, w_a, w_b)


def _inproj_plain_body(x_ref, g_ref, wq_ref, wk_ref, wv_ref, wz_ref, wx_ref, ws_ref, bs_ref,
                       q_ref, k_ref, v_ref, z_ref, xbc_ref, slab_ref):
    h = _rms(x_ref[...], g_ref[...]).astype(BF16)
    q_ref[...] = _dot(h, wq_ref[...])
    k_ref[...] = _dot(h, wk_ref[...])
    v_ref[...] = _dot(h, wv_ref[...])
    z_ref[...] = _dot(h, wz_ref[...])
    xbc_ref[...] = _dot(h, wx_ref[...])
    slab_ref[...] = _small_slab(_dot(h, ws_ref[...]), bs_ref[...])


def _inproj_plain(x1, norm_g, ws, bias_s):
    n = x1.shape[0]
    wq, wk, wv, wz, wx, wsm = ws
    full = lambda a: pl.BlockSpec(a.shape, lambda i: (0, 0))
    out = lambda w: pl.BlockSpec((n, w), lambda i: (0, 0))
    return pl.pallas_call(
        _inproj_plain_body,
        grid=(1,),
        in_specs=[full(x1), full(norm_g), full(wq), full(wk), full(wv), full(wz), full(wx),
                  full(wsm), full(bias_s)],
        out_specs=[out(D_ATT), out(D_ATT), out(D_ATT), out(D_SSM), out(CONV_DIM), out(LANES)],
        out_shape=[jax.ShapeDtypeStruct((n, w), F32)
                   for w in (D_ATT, D_ATT, D_ATT, D_SSM, CONV_DIM, LANES)],
        compiler_params=_cparams(("arbitrary",)),
        name="inproj_sample",
    )(x1, norm_g, wq, wk, wv, wz, wx, wsm, bias_s)


def _head_expand_mat():
    e = np.zeros((LANES, D_SSM), np.float32)
    for h in range(N_SSM_HEADS):
        e[DT_LANE0 + h, h * SSM_HEAD_DIM:(h + 1) * SSM_HEAD_DIM] = 1.0
    return jnp.asarray(e, BF16)


def _gate_groupnorm(y, z, gain):
    g = y * _silu(z)
    half = D_SSM // N_SSM_GROUPS
    parts = []
    for gi in range(N_SSM_GROUPS):
        gg = g[:, gi * half:(gi + 1) * half]
        parts.append(gg * lax.rsqrt(jnp.mean(gg * gg, axis=-1, keepdims=True) + EPS))
    return jnp.concatenate(parts, axis=1) * gain


def _ssd_chunk(xc, slab, z, ht_old, arow, dsk, sg, e8):
    Q = SSD_CHUNK
    xs = xc[:, :D_SSM]
    lane = lax.broadcasted_iota(jnp.int32, slab.shape, 1)
    dtm = jnp.where((lane >= DT_LANE0) & (lane < 2 * DT_LANE0), slab, 0.0)
    a = dtm * arow
    a_c = _dot3r(_tril(Q), a)
    a_ct = a_c.T
    a_last = a_c[Q - 1:Q, :]
    ea_x = _dot3(jnp.exp(a_c), e8)
    dec_x = _dot3(jnp.exp(a_last - a_c), e8)
    xdt = xs * _dot3(dtm, e8)
    xdt_b = xdt.astype(BF16)
    xd_b = (xdt * dec_x).astype(BF16)
    cd_x = ea_x[Q - 1:Q, :]

    r_i = lax.broadcasted_iota(jnp.int32, (Q, Q), 0)
    c_i = lax.broadcasted_iota(jnp.int32, (Q, Q), 1)
    tri = c_i <= r_i
    lane_q = lax.broadcasted_iota(jnp.int32, (Q, LANES), 1)
    gw = D_SSM // N_SSM_GROUPS
    hpg = N_SSM_HEADS // N_SSM_GROUPS
    y_parts, ht_parts = [], []
    for g in range(N_SSM_GROUPS):
        bm = xc[:, D_SSM + g * D_STATE:D_SSM + (g + 1) * D_STATE]
        cm = xc[:, D_SSM + (N_SSM_GROUPS + g) * D_STATE:D_SSM + (N_SSM_GROUPS + g + 1) * D_STATE]
        cb16 = cm.astype(BF16)
        cbm = _dot_nt(cb16, bm.astype(BF16))
        y_off = _dot(cb16, ht_old[:, g * gw:(g + 1) * gw].astype(BF16))
        y_diag = []
        for pr in range(hpg // 2):
            halves = []
            for hh in range(2):
                h = g * hpg + pr * 2 + hh
                col = a_c[:, DT_LANE0 + h:DT_LANE0 + h + 1]
                row = a_ct[DT_LANE0 + h:DT_LANE0 + h + 1, :]
                lm = jnp.where(tri, jnp.exp(col - row), 0.0)
                sc = (cbm * lm).astype(BF16)
                lo = g * gw + pr * LANES
                halves.append(_dot(sc, xdt_b[:, lo:lo + LANES]))
            y_diag.append(jnp.where(lane_q < SSM_HEAD_DIM, halves[0], halves[1]))
        y_parts.append(jnp.concatenate(y_diag, axis=1) + y_off * ea_x[:, g * gw:(g + 1) * gw])
        new = _dot(bm.T.astype(BF16), xd_b[:, g * gw:(g + 1) * gw])
        ht_parts.append(ht_old[:, g * gw:(g + 1) * gw] * cd_x[:, g * gw:(g + 1) * gw] + new)
    y = jnp.concatenate(y_parts, axis=1) + dsk * xs
    return _gate_groupnorm(y, z, sg).astype(BF16), jnp.concatenate(ht_parts, axis=1)


def _ssd_body(xc_ref, slab_ref, z_ref, arow_ref, dsk_ref, sg_ref, e8_ref, gn_ref, st_ref, ht_scr,
              *, n_steps, cps):
    c = pl.program_id(1)
    Q = SSD_CHUNK

    @pl.when(c == 0)
    def _():
        ht_scr[...] = jnp.zeros_like(ht_scr)

    ht = ht_scr[...]
    for ci in range(cps):
        rows = slice(ci * Q, (ci + 1) * Q)
        gn, ht = _ssd_chunk(xc_ref[0, rows, :], slab_ref[0, rows, :], z_ref[0, rows, :], ht,
                            arow_ref[...], dsk_ref[...], sg_ref[...], e8_ref[...])
        gn_ref[0, rows, :] = gn
    ht_scr[...] = ht

    @pl.when(c == n_steps - 1)
    def _():
        st_ref[0] = ht.T


def _ssd_prompt(xc, slab, z, a_row, dsk_x, ssm_g, *, cps):
    b, L, _ = xc.shape
    rows = cps * SSD_CHUNK
    ns = L // rows
    e8 = _head_expand_mat()
    blk = lambda w: pl.BlockSpec((1, rows, w), lambda bi, ci: (bi, ci, 0))
    full = lambda a: pl.BlockSpec(a.shape, lambda bi, ci: (0, 0))
    return pl.pallas_call(
        functools.partial(_ssd_body, n_steps=ns, cps=cps),
        grid=(b, ns),
        in_specs=[blk(CONV_DIM), blk(LANES), blk(D_SSM), full(a_row), full(dsk_x), full(ssm_g), full(e8)],
        out_specs=[blk(D_SSM), pl.BlockSpec((1, D_SSM, D_STATE), lambda bi, ci: (bi, 0, 0))],
        out_shape=[jax.ShapeDtypeStruct((b, L, D_SSM), BF16),
                   jax.ShapeDtypeStruct((b, D_SSM, D_STATE), F32)],
        scratch_shapes=[pltpu.VMEM((D_STATE, D_SSM), F32)],
        compiler_params=_cparams(("arbitrary", "arbitrary")),
        name="ssd_prompt",
    )(xc, slab, z, a_row, dsk_x, ssm_g, e8)


def _ssd_step_body(xbc_ref, sc_ref, slab_ref, z_ref, st_ref, cw_ref, cb_ref, arow_ref, dsk_ref,
                   sg_ref, e8_ref, gn_ref, so_ref, xc_scr, dtx_scr, decx_scr, *, spb):
    step = pl.program_id(0)

    @pl.when(step == 0)
    def _():
        conv = (cb_ref[...] + cw_ref[0:1, :] * sc_ref[0] + cw_ref[1:2, :] * sc_ref[1]
                + cw_ref[2:3, :] * sc_ref[2] + cw_ref[3:4, :] * xbc_ref[...])
        xc_scr[...] = _silu(conv)
        slab = slab_ref[...]
        lane = lax.broadcasted_iota(jnp.int32, slab.shape, 1)
        dtm = jnp.where((lane >= DT_LANE0) & (lane < 2 * DT_LANE0), slab, 0.0)
        e8 = e8_ref[...]
        dtx_scr[...] = _dot3(dtm, e8)
        decx_scr[...] = _dot3(jnp.exp(dtm * arow_ref[...]), e8)

    gw = D_SSM // N_SSM_GROUPS
    for i in range(spb):
        b = step * spb + i
        xrow = xc_scr[pl.ds(b, 1), :]
        xs = xrow[:, :D_SSM]
        xdt = xs * dtx_scr[pl.ds(b, 1), :]
        dec = decx_scr[pl.ds(b, 1), :]
        y_rows = []
        for j in range(D_SSM // LANES):
            g = (j * LANES) // gw
            bm = xrow[:, D_SSM + g * D_STATE:D_SSM + (g + 1) * D_STATE]
            cm = xrow[:, D_SSM + (N_SSM_GROUPS + g) * D_STATE:D_SSM + (N_SSM_GROUPS + g + 1) * D_STATE]
            xcol = _row_to_col(xdt[:, j * LANES:(j + 1) * LANES])
            dcol = _row_to_col(dec[:, j * LANES:(j + 1) * LANES])
            hs = dcol * st_ref[i, j * LANES:(j + 1) * LANES, :] + xcol * bm
            so_ref[i, j * LANES:(j + 1) * LANES, :] = hs
            y_rows.append(_col_to_row(jnp.sum(hs * cm, axis=1, keepdims=True)))
        y = jnp.concatenate(y_rows, axis=1) + dsk_ref[...] * xs
        gn_ref[i] = _gate_groupnorm(y, z_ref[pl.ds(b, 1), :], sg_ref[...]).astype(BF16)


def _ssd_step(xbc, sconv_t, slab, z, state, conv_w, conv_b, a_row, dsk_x, ssm_g, *, spb):
    nb = xbc.shape[0]
    assert nb % spb == 0
    e8 = _head_expand_mat()
    full2 = lambda a: pl.BlockSpec(a.shape, lambda bi: (0,) * a.ndim)
    gn, st = pl.pallas_call(
        functools.partial(_ssd_step_body, spb=spb),
        grid=(nb // spb,),
        in_specs=[full2(xbc), full2(sconv_t), full2(slab), full2(z),
                  pl.BlockSpec((spb, D_SSM, D_STATE), lambda bi: (bi, 0, 0)),
                  full2(conv_w), full2(conv_b), full2(a_row), full2(dsk_x), full2(ssm_g), full2(e8)],
        out_specs=[pl.BlockSpec((spb, 1, D_SSM), lambda bi: (bi, 0, 0)),
                   pl.BlockSpec((spb, D_SSM, D_STATE), lambda bi: (bi, 0, 0))],
        out_shape=[jax.ShapeDtypeStruct((nb, 1, D_SSM), BF16),
                   jax.ShapeDtypeStruct((nb, D_SSM, D_STATE), F32)],
        scratch_shapes=[pltpu.VMEM((nb, CONV_DIM), F32), pltpu.VMEM((nb, D_SSM), F32),
                        pltpu.VMEM((nb, D_SSM), F32)],
        compiler_params=_cparams(("arbitrary",)),
        name="ssd_step",
    )(xbc, sconv_t, slab, z, state, conv_w, conv_b, a_row, dsk_x, ssm_g, e8)
    return gn.reshape(nb, D_SSM), st


DECODE_GROUP = 16
DECODE_SLOTS = 3


def _fox_body(pt_ref, q_ref, k_ref, v_ref, aq_ref, ak_ref, qs_ref, ks_ref, vs_ref, lfs_ref,
              ck_hbm, cv_hbm, clf_hbm, o_ref, os_ref,
              s_scr, m_scr, acc_scr, qp_scr, kbuf, vbuf, lbuf, sems, qblk_scr, dm_scr, dl_scr, dacc_scr,
              carry_scr, *, tq, nt, gps, gpseq, n_groups):
    G, NS = DECODE_GROUP, DECODE_SLOTS
    H, HD, P = N_ATT_HEADS, HEAD_DIM, PAGE_SIZE
    t = pl.program_id(2)
    sid = (pl.program_id(0) * pl.num_programs(1) + pl.program_id(1)) * pl.num_programs(2) + t
    nl = tq // LANES

    def group_copies(gg):
        slot = gg % NS
        src = jnp.minimum(gg, n_groups - 1)
        seq = src // gpseq
        base = (gpseq - 1 - src % gpseq) * G
        copies = []
        for i in range(G):
            page = pt_ref[seq, base + i]
            copies.append(pltpu.make_async_copy(ck_hbm.at[page], kbuf.at[slot, i], sems.at[slot, 0]))
            copies.append(pltpu.make_async_copy(cv_hbm.at[page], vbuf.at[slot, i], sems.at[slot, 1]))
            copies.append(pltpu.make_async_copy(clf_hbm.at[page], lbuf.at[slot, i], sems.at[slot, 2]))
        return copies

    def start_group(gg):
        for n, cp in enumerate(group_copies(gg)):
            cp.start(priority=1 if n % 3 == 1 else 0)

    def wait_group(gg):
        for cp in group_copies(gg):
            cp.wait()

    r8 = lax.broadcasted_iota(jnp.int32, (H, D_ATT), 0)
    c8 = lax.broadcasted_iota(jnp.int32, (H, D_ATT), 1)
    own_head = c8 // HD == r8

    def decode_init():
        seq = (sid * gps) // gpseq
        qs = (qs_ref[seq] * ATT_SCALE).astype(BF16).astype(F32)
        qblk = jnp.where(own_head, qs, 0.0)
        qblk_scr[...] = qblk.astype(BF16)
        k2 = ks_ref[seq].astype(BF16).astype(F32)
        dm_scr[...] = jnp.broadcast_to(jnp.sum(qblk * k2, axis=1, keepdims=True), dm_scr.shape)
        dl_scr[...] = jnp.ones_like(dl_scr)
        dacc_scr[...] = jnp.broadcast_to(vs_ref[seq].astype(BF16).astype(F32), dacc_scr.shape)
        carry_scr[...] = jnp.broadcast_to(_row_to_col(lfs_ref[seq])[0:H, :], carry_scr.shape)

    def decode_group(slot):
        kcat = jnp.concatenate([kbuf[slot, i].astype(BF16) for i in range(G)], axis=1)
        vcat = jnp.concatenate([vbuf[slot, i].astype(BF16) for i in range(G)], axis=1)
        s = _dot(qblk_scr[...], kcat)
        x = jnp.concatenate([lbuf[slot, i] for i in range(G)], axis=0)
        rr = lax.broadcasted_iota(jnp.int32, (P, P), 0)
        cc = lax.broadcasted_iota(jnp.int32, (P, P), 1)
        rev_local = _dot3(x, (rr > cc).astype(BF16))
        tot = jnp.sum(x, axis=1, keepdims=True)
        carry = carry_scr[:, 0:1]
        s_pages = [None] * G
        for i in reversed(range(G)):
            s_pages[i] = s[:, i * P:(i + 1) * P] + rev_local[i * H:(i + 1) * H, :] + carry
            carry = carry + tot[i * H:(i + 1) * H, :]
        carry_scr[...] = jnp.broadcast_to(carry, carry_scr.shape)
        st = jnp.concatenate(s_pages, axis=1)
        m_old = dm_scr[:, 0:1]
        m_new = jnp.maximum(m_old, jnp.max(st, axis=1, keepdims=True))
        alpha = jnp.exp(m_old - m_new)
        p = jnp.exp(st - m_new)
        l_new = alpha * dl_scr[:, 0:1] + jnp.sum(p, axis=1, keepdims=True)
        dacc_scr[...] = alpha * dacc_scr[...] + _dot_nt(p.astype(BF16), vcat)
        dm_scr[...] = jnp.broadcast_to(m_new, dm_scr.shape)
        dl_scr[...] = jnp.broadcast_to(l_new, dl_scr.shape)

    def decode_finish():
        o = dacc_scr[...] / dl_scr[:, 0:1]
        os_ref[0] = jnp.sum(jnp.where(own_head, o, 0.0), axis=0, keepdims=True)

    r = lax.broadcasted_iota(jnp.int32, (tq, tq), 0)
    c = lax.broadcasted_iota(jnp.int32, (tq, tq), 1)
    causal = c <= r

    def chunk_of(ci):
        first = ci <= t
        sel = jnp.where(first, 0, 1)
        q0 = pl.multiple_of(jnp.where(first, t, nt - 1 - t) * tq, tq)
        k0 = pl.multiple_of(jnp.where(first, ci, ci - t - 1) * tq, tq)
        return sel, q0, k0

    lane_q = lax.broadcasted_iota(jnp.int32, (tq, LANES), 1)
    even_dims = lane_q < HD

    def per_head(x2, a2):
        return jnp.where(even_dims, x2, a2), jnp.where(even_dims, a2, x2)

    def build_queries():
        for x, tile in enumerate((t, nt - 1 - t)):
            rows = pl.ds(pl.multiple_of(tile * tq, tq), tq)
            q_e, q_o = per_head(q_ref[0, rows, :], aq_ref[0, rows, :])
            qp_scr[x, 0] = q_e
            qp_scr[x, 1] = q_o

    def scores(ci):
        sel, q0, k0 = chunk_of(ci)
        ks = per_head(k_ref[0, pl.ds(k0, tq), :], ak_ref[0, pl.ds(k0, tq), :])
        for j in range(2):
            s = _dot_nt(qp_scr[sel, j], ks[j])
            if ci == nt:
                s = jnp.where(causal, s, NEG_BIG)
            elif ci < nt // 2:
                s = jnp.where(jnp.logical_or(causal, ci != t), s, NEG_BIG)
            s_scr[j, ci] = s
            m = m_scr[sel, j]
            for u in range(nl):
                m = jnp.maximum(m, s[:, u * LANES:(u + 1) * LANES])
            m_scr[sel, j] = m

    def row_max():
        for x in range(2):
            for j in range(2):
                m_scr[x, j] = jnp.broadcast_to(jnp.max(m_scr[x, j], axis=1, keepdims=True), (tq, LANES))

    one_at = lambda ln: jnp.where(lane_q == ln, 1.0, 0.0).astype(BF16)
    ones_e, ones_o = one_at(HD), one_at(0)

    def weighted(ci):
        sel, q0, k0 = chunk_of(ci)
        v2 = v_ref[0, pl.ds(k0, tq), :]
        vs = (jnp.where(even_dims, v2, ones_e), jnp.where(even_dims, ones_o, v2))
        for j in range(2):
            s = s_scr[j, ci]
            mrep = m_scr[sel, j]
            p = jnp.concatenate([jnp.exp2(s[:, u * LANES:(u + 1) * LANES] - mrep) for u in range(nl)], axis=1)
            acc_scr[sel, j] += _dot(p.astype(BF16), vs[j])

    def write_out():
        lane = lax.broadcasted_iota(jnp.int32, (tq, LANES), 1)
        for x, tile in enumerate((t, nt - 1 - t)):
            acc_e, acc_o = acc_scr[x, 0], acc_scr[x, 1]
            out = jnp.where(lane < HEAD_DIM, acc_e / acc_e[:, HEAD_DIM:HEAD_DIM + 1], acc_o / acc_o[:, 0:1])
            o_ref[0, pl.ds(pl.multiple_of(tile * tq, tq), tq), :] = out

    items = ([functools.partial(scores, ci) for ci in range(nt + 1)] + [row_max]
             + [functools.partial(weighted, ci) for ci in range(nt + 1)] + [write_out])
    per_part = -(-len(items) // gps)
    parts = [items[i * per_part:(i + 1) * per_part] for i in range(gps)]

    @pl.when(sid == 0)
    def _():
        for g0 in range(NS - 1):
            start_group(g0)

    pl.when((sid * gps) % gpseq == 0)(decode_init)
    m_scr[...] = jnp.full(m_scr.shape, NEG_BIG, F32)
    acc_scr[...] = jnp.zeros_like(acc_scr)
    build_queries()
    for gi in range(gps):
        gg = sid * gps + gi
        wait_group(gg)
        start_group(gg + (NS - 1))
        for item in parts[gi]:
            item()
        decode_group(gg % NS)
    pl.when((sid * gps + gps - 1) % gpseq == gpseq - 1)(decode_finish)

    @pl.when(sid == pl.num_programs(0) * pl.num_programs(1) * pl.num_programs(2) - 1)
    def _():
        for extra in range(NS - 1):
            wait_group(n_groups + extra)


def _fox_attention(q16, k16, v16, aq, ak, q_s, k_s, v_s, slab_s, cache_k, cache_v, cache_logf, page_table,
                   *, tq):
    b, L, _ = q16.shape
    nt = L // tq
    nb, n_pages = page_table.shape
    n_steps = b * (N_ATT_HEADS // 2) * (nt // 2)
    n_groups = nb * n_pages // DECODE_GROUP
    gps = n_groups // n_steps
    gpseq = n_pages // DECODE_GROUP
    assert nt % 2 == 0 and n_pages % DECODE_GROUP == 0
    assert gps * n_steps == n_groups and gpseq % gps == 0 and n_groups >= DECODE_SLOTS
    spq = gpseq // gps
    hp_n, t_n = N_ATT_HEADS // 2, nt // 2
    seq_of = lambda bi, hp, t: ((bi * hp_n + hp) * t_n + t) // spq
    pair = pl.BlockSpec((1, L, LANES), lambda bi, hp, t, pt: (bi, 0, hp), pipeline_mode=pl.Buffered(1))
    srow = lambda w: pl.BlockSpec((nb, 1, w), lambda bi, hp, t, pt: (0, 0, 0), pipeline_mode=pl.Buffered(1))
    anyspec = pl.BlockSpec(memory_space=pl.ANY)
    row3 = lambda a: a.reshape(nb, 1, a.shape[-1])
    G, NS = DECODE_GROUP, DECODE_SLOTS
    att, att_s = pl.pallas_call(
        functools.partial(_fox_body, tq=tq, nt=nt, gps=gps, gpseq=gpseq, n_groups=n_groups),
        grid_spec=pltpu.PrefetchScalarGridSpec(
            num_scalar_prefetch=1,
            grid=(b, hp_n, t_n),
            in_specs=[pair, pair, pair, pair, pair, srow(D_ATT), srow(D_ATT), srow(D_ATT), srow(LANES),
                      anyspec, anyspec, anyspec],
            out_specs=[pl.BlockSpec((1, L, LANES), lambda bi, hp, t, pt: (bi, 0, hp)),
                       pl.BlockSpec((1, 1, D_ATT), lambda bi, hp, t, pt: (seq_of(bi, hp, t), 0, 0))],
            scratch_shapes=[
                pltpu.VMEM((2, nt + 1, tq, tq), F32), pltpu.VMEM((2, 2, tq, LANES), F32),
                pltpu.VMEM((2, 2, tq, LANES), F32), pltpu.VMEM((2, 2, tq, LANES), BF16),
                pltpu.VMEM((NS, G, D_ATT, PAGE_SIZE), F32), pltpu.VMEM((NS, G, D_ATT, PAGE_SIZE), F32),
                pltpu.VMEM((NS, G, N_ATT_HEADS, PAGE_SIZE), F32), pltpu.SemaphoreType.DMA((NS, 3)),
                pltpu.VMEM((N_ATT_HEADS, D_ATT), BF16), pltpu.VMEM((8, LANES), F32),
                pltpu.VMEM((8, LANES), F32), pltpu.VMEM((8, D_ATT), F32), pltpu.VMEM((8, LANES), F32)]),
        out_shape=[jax.ShapeDtypeStruct((b, L, D_ATT), F32), jax.ShapeDtypeStruct((nb, 1, D_ATT), F32)],
        compiler_params=_cparams(("arbitrary", "arbitrary", "arbitrary")),
        name="fox_attention",
    )(page_table, q16, k16, v16, aq, ak, row3(q_s), row3(k_s), row3(v_s), row3(slab_s),
      cache_k, cache_v, cache_logf)
    return att, att_s.reshape(nb, D_ATT)


def _lane_pad(vec, lane0):
    out = jnp.zeros((1, LANES), F32)
    return out.at[0, lane0:lane0 + vec.shape[0]].set(vec.astype(F32))


def kernel(x_prompt, x_sample, cache_k, cache_v, cache_logf, page_table, state_conv, state_ssm,
           ffn1_norm, w_ffn1_in, w_ffn1_out, mix_norm, w_in, b_f, conv_w, conv_b, dt_bias, a_log,
           d_skip, att_out_norm, ssm_out_norm, w_out, ffn2_norm, w_ffn2_in, w_ffn2_out, final_norm):
    depth = w_in.shape[0]
    assert depth == 1
    B, L, _ = x_prompt.shape
    nb, T, _ = x_sample.shape
    assert T == 1
    H, HD = N_ATT_HEADS, HEAD_DIM

    l0 = 0
    w1i, w1o = w_ffn1_in[l0].astype(BF16), w_ffn1_out[l0].astype(BF16)
    wi = w_in[l0]
    o_f = 3 * D_ATT
    o_z = o_f + H
    o_x = o_z + D_SSM
    o_dt = o_x + CONV_DIM
    w_small = jnp.zeros((D_MODEL, LANES), F32)
    w_small = w_small.at[:, 0:H].set(wi[:, o_f:o_f + H])
    w_small = w_small.at[:, DT_LANE0:DT_LANE0 + N_SSM_HEADS].set(wi[:, o_dt:o_dt + N_SSM_HEADS])
    ws = tuple(w.astype(BF16) for w in (wi[:, 0:D_ATT], wi[:, D_ATT:2 * D_ATT], wi[:, 2 * D_ATT:3 * D_ATT],
                                        wi[:, o_z:o_z + D_SSM], wi[:, o_x:o_x + CONV_DIM], w_small))
    bias_s = _lane_pad(b_f[l0], 0) + _lane_pad(dt_bias[l0], DT_LANE0)
    a_row = _lane_pad(-jnp.exp(a_log[l0].astype(F32)), DT_LANE0)
    dsk_x = jnp.repeat(d_skip[l0].astype(F32), SSM_HEAD_DIM)[None, :]
    row = lambda v: v.astype(F32)[None, :]
    g1, gm, g2, gf = row(ffn1_norm[l0]), row(mix_norm[l0]), row(ffn2_norm[l0]), row(final_norm)
    ga, gs = row(att_out_norm[l0]), row(ssm_out_norm[l0])
    cw, cb = conv_w[l0].astype(F32), row(conv_b[l0])
    wm = w_out[l0].astype(BF16)

    TM, TF = FFN_ROWS, FFN_CHUNK
    xp = x_prompt.reshape(B * L, D_MODEL)
    x1 = _ffn(xp, g1, w1i, w1o, tm=TM, tf=TF)
    (k_p, v_p, q16, k16, v16, aq, ak, z_p, xc_p, tail_p, slab_p, w2i, w2o) = _inproj_aug(
        x1, gm, ws, bias_s, cw, cb, w_ffn2_in[l0].astype(F32), w_ffn2_out[l0].astype(F32),
        tm=INPROJ_ROWS, seq=L)
    xs = x_sample.reshape(nb, D_MODEL)
    x1s = _ffn(xs, g1, w1i, w1o, tm=nb, tf=TF)
    q_s, k_s, v_s, z_s, xbc_s, slab_s = _inproj_plain(x1s, gm, ws, bias_s)

    n_pool = cache_k.shape[1]
    ck = jnp.transpose(cache_k[l0], (0, 2, 3, 1)).reshape(n_pool, D_ATT, PAGE_SIZE)
    cv = jnp.transpose(cache_v[l0], (0, 2, 3, 1)).reshape(n_pool, D_ATT, PAGE_SIZE)
    clf = jnp.transpose(cache_logf[l0], (0, 2, 1))
    seq3 = lambda a: a.reshape(B, L, a.shape[-1])
    att, att_s = _fox_attention(seq3(q16), seq3(k16), seq3(v16), seq3(aq), seq3(ak),
                                q_s, k_s, v_s, slab_s, ck, cv, clf, page_table, tq=ATT_ROWS)

    gn_p, st_p = _ssd_prompt(seq3(xc_p), seq3(slab_p), seq3(z_p), a_row, dsk_x, gs, cps=SSD_CHUNKS_PER_STEP)
    y_p = _ffn2(x1, att.reshape(B * L, D_ATT), gn_p.reshape(B * L, D_SSM), ga, wm, g2, w2i, w2o, gf,
                tm=TM, tf=TF)

    sconv_t = jnp.swapaxes(state_conv[l0], 0, 1)
    gn_s, st_s = _ssd_step(xbc_s, sconv_t, slab_s, z_s,
                           state_ssm[l0].reshape(nb, D_SSM, D_STATE), cw, cb, a_row, dsk_x, gs,
                           spb=SSD_STEP_SEQS if nb % SSD_STEP_SEQS == 0 else 1)
    y_s = _ffn2(x1s, att_s, gn_s, ga, wm, g2, w2i, w2o, gf, tm=nb, tf=TF)

    return (
        y_p.reshape(B, L, D_MODEL),
        y_s.reshape(nb, 1, D_MODEL),
        jnp.transpose(k_p.reshape(1, B, H, HD, L), (0, 1, 4, 2, 3)),
        jnp.transpose(v_p.reshape(1, B, H, HD, L), (0, 1, 4, 2, 3)),
        slab_p[:, :H].reshape(1, B, L, H),
        tail_p[:, 8 - (CONV_W - 1):, :][None],
        st_p.reshape(1, B, N_SSM_HEADS, SSM_HEAD_DIM, D_STATE),
        k_s.reshape(1, nb, 1, H, HD),
        v_s.reshape(1, nb, 1, H, HD),
        slab_s[:, :H].reshape(1, nb, 1, H),
        jnp.concatenate([state_conv[l0][:, 1:, :], xbc_s[:, None, :]], axis=1)[None],
        st_s.reshape(1, nb, N_SSM_HEADS, SSM_HEAD_DIM, D_STATE),
    )
```

```python
import functools

import numpy as np
import jax
import jax.numpy as jnp
from jax import lax
from jax.experimental import pallas as pl
from jax.experimental.pallas import tpu as pltpu

F32 = jnp.float32
BF16 = jnp.bfloat16

D_MODEL = 1024
D_ATT = 512
D_SSM = 512
HEAD_DIM = 64
N_ATT_HEADS = 8
N_SSM_HEADS = 8
SSM_HEAD_DIM = 64
N_SSM_GROUPS = 2
D_STATE = 128
CONV_W = 4
CONV_DIM = D_SSM + 2 * N_SSM_GROUPS * D_STATE
SSD_CHUNK = 128
PAGE_SIZE = 128
D_FF = 2816
FFN_RESIDUAL = 0.5
EPS = 1e-6
ATT_SCALE = HEAD_DIM ** -0.5
LOG2E = 1.4426950408889634
LANES = 128
DT_LANE0 = 8
NEG_BIG = -1e30

VMEM_LIMIT = 56 * 1024 * 1024
FFN_ROWS = 1024
FFN_CHUNK = 256
INPROJ_ROWS = 512
ATT_ROWS = 256
SSD_CHUNKS_PER_STEP = 8
SSD_STEP_SEQS = 4


def _cparams(sem):
    return pltpu.CompilerParams(dimension_semantics=sem, vmem_limit_bytes=VMEM_LIMIT)


def _dot(a, b):
    return jnp.dot(a, b, preferred_element_type=F32)


def _dot_nt(a, b):
    return lax.dot_general(a, b, (((1,), (1,)), ((), ())), preferred_element_type=F32)


def _split3(x):
    hi = x.astype(BF16)
    r1 = x - hi.astype(F32)
    mid = r1.astype(BF16)
    lo = (r1 - mid.astype(F32)).astype(BF16)
    return hi, mid, lo


def _dot3(x, m):
    hi, mid, lo = _split3(x)
    return _dot(hi, m) + _dot(mid, m) + _dot(lo, m)


def _dot3r(m, x):
    hi, mid, lo = _split3(x)
    return _dot(m, hi) + _dot(m, mid) + _dot(m, lo)


def _rms(x, g):
    return x * lax.rsqrt(jnp.mean(x * x, axis=-1, keepdims=True) + EPS) * g


def _silu(x):
    return x / (1.0 + jnp.exp(-x))


def _tril(n, dtype=BF16):
    r = lax.broadcasted_iota(jnp.int32, (n, n), 0)
    c = lax.broadcasted_iota(jnp.int32, (n, n), 1)
    return (c <= r).astype(dtype)


def _row_to_col(row):
    n = row.shape[1]
    r = lax.broadcasted_iota(jnp.int32, (n, n), 0)
    c = lax.broadcasted_iota(jnp.int32, (n, n), 1)
    return jnp.sum(jnp.where(r == c, jnp.broadcast_to(row, (n, n)), 0.0), axis=1, keepdims=True)


def _col_to_row(col):
    n = col.shape[0]
    r = lax.broadcasted_iota(jnp.int32, (n, n), 0)
    c = lax.broadcasted_iota(jnp.int32, (n, n), 1)
    return jnp.sum(jnp.where(r == c, jnp.broadcast_to(col, (n, n)), 0.0), axis=0, keepdims=True)


def _resident(shape):
    return pl.BlockSpec(shape, lambda i: (0,) * len(shape), pipeline_mode=pl.Buffered(1))


def _swiglu_half(x, g_ref, wi_ref, wo_ref, a_scr, tf):
    h = _rms(x, g_ref[...]).astype(BF16)
    for c in range(D_FF // tf):
        gate = _dot(h, wi_ref[:, c * tf:(c + 1) * tf])
        up = _dot(h, wi_ref[:, D_FF + c * tf:D_FF + (c + 1) * tf])
        a_scr[:, c * tf:(c + 1) * tf] = (_silu(gate) * up).astype(BF16)
    return x + FFN_RESIDUAL * _dot(a_scr[...], wo_ref[...])


def _ffn_body(x_ref, g_ref, wi_ref, wo_ref, o_ref, a_scr, *, tf):
    o_ref[...] = _swiglu_half(x_ref[...], g_ref, wi_ref, wo_ref, a_scr, tf)


def _ffn(x, norm_g, w_in_b, w_out_b, *, tm, tf):
    n = x.shape[0]
    row = lambda w: pl.BlockSpec((tm, w), lambda i: (i, 0))
    return pl.pallas_call(
        functools.partial(_ffn_body, tf=tf),
        grid=(n // tm,),
        in_specs=[row(D_MODEL), _resident(norm_g.shape), _resident(w_in_b.shape), _resident(w_out_b.shape)],
        out_specs=row(D_MODEL),
        out_shape=jax.ShapeDtypeStruct((n, D_MODEL), F32),
        scratch_shapes=[pltpu.VMEM((tm, D_FF), BF16)],
        compiler_params=_cparams(("arbitrary",)),
        name="ffn1",
    )(x, norm_g, w_in_b, w_out_b)


def _ffn2_body(x_ref, att_ref, gn_ref, ag_ref, wm_ref, g_ref, wi_ref, wo_ref, fg_ref, o_ref, a_scr, *, tf):
    an = _rms(att_ref[...], ag_ref[...]).astype(BF16)
    x2 = x_ref[...] + _dot(an, wm_ref[:D_ATT, :]) + _dot(gn_ref[...], wm_ref[D_ATT:, :])
    o_ref[...] = _rms(_swiglu_half(x2, g_ref, wi_ref, wo_ref, a_scr, tf), fg_ref[...])


def _ffn2(x1, att, gn, att_g, w_mix_b, norm_g, w_in_b, w_out_b, final_g, *, tm, tf):
    n = x1.shape[0]
    row = lambda w: pl.BlockSpec((tm, w), lambda i: (i, 0))
    return pl.pallas_call(
        functools.partial(_ffn2_body, tf=tf),
        grid=(n // tm,),
        in_specs=[row(D_MODEL), row(D_ATT), row(D_SSM), _resident(att_g.shape), _resident(w_mix_b.shape),
                  _resident(norm_g.shape), _resident(w_in_b.shape), _resident(w_out_b.shape),
                  _resident(final_g.shape)],
        out_specs=row(D_MODEL),
        out_shape=jax.ShapeDtypeStruct((n, D_MODEL), F32),
        scratch_shapes=[pltpu.VMEM((tm, D_FF), BF16)],
        compiler_params=_cparams(("arbitrary",)),
        name="ffn2",
    )(x1, att, gn, att_g, w_mix_b, norm_g, w_in_b, w_out_b, final_g)


def _small_slab(raw, bias):
    v = raw + bias
    t = jnp.log1p(jnp.exp(-jnp.abs(v)))
    lane = lax.broadcasted_iota(jnp.int32, v.shape, 1)
    logf = -(jnp.maximum(-v, 0.0) + t)
    dt = jnp.maximum(v, 0.0) + t
    return jnp.where(lane < DT_LANE0, logf, jnp.where(lane < 2 * DT_LANE0, dt, 0.0))


def _inproj_aug_body(x_ref, g_ref, wq_ref, wk_ref, wv_ref, wz_ref, wx_ref, ws_ref, bs_ref,
                     eq_ref, ek_ref, cw_ref, cb_ref, wa_ref, wb_ref,
                     k_ref, v_ref, q16_ref, k16_ref, v16_ref, aq_ref, ak_ref, z_ref, xc_ref, tail_ref, slab_ref,
                     lft_ref, wa16_ref, wb16_ref, carry_scr, conv_scr, *, tiles_per_seq, tm):
    i = pl.program_id(0)
    wa16_ref[...] = wa_ref[...].astype(BF16)
    wb16_ref[...] = wb_ref[...].astype(BF16)

    @pl.when(i % tiles_per_seq == 0)
    def _():
        carry_scr[...] = jnp.zeros_like(carry_scr)
        conv_scr[0:8, :] = jnp.zeros((8, CONV_DIM), F32)

    h = _rms(x_ref[...], g_ref[...]).astype(BF16)
    q = _dot(h, wq_ref[...])
    k = _dot(h, wk_ref[...])
    v = _dot(h, wv_ref[...])
    k_ref[0] = k.T
    v_ref[0] = v.T
    q16_ref[...] = (q * (ATT_SCALE * LOG2E)).astype(BF16)
    k16_ref[...] = k.astype(BF16)
    v16_ref[...] = v.astype(BF16)
    z_ref[...] = _dot(h, wz_ref[...])
    slab = _small_slab(_dot(h, ws_ref[...]), bs_ref[...])
    slab_ref[...] = slab
    lft_ref[0] = slab.T[0:DT_LANE0, :]

    u = _dot(h, wx_ref[...])
    conv_scr[8:8 + tm, :] = u
    conv = (cb_ref[...] + cw_ref[0:1, :] * conv_scr[5:5 + tm, :] + cw_ref[1:2, :] * conv_scr[6:6 + tm, :]
            + cw_ref[2:3, :] * conv_scr[7:7 + tm, :] + cw_ref[3:4, :] * u)
    conv_scr[0:8, :] = u[tm - 8:tm, :]
    tail_ref[0] = u[tm - 8:tm, :]
    xc_ref[...] = _silu(conv)

    lane = lax.broadcasted_iota(jnp.int32, slab.shape, 1)
    logf = jnp.where(lane < DT_LANE0, slab, 0.0)
    c = _dot3r(_tril(tm), logf) + carry_scr[0:1, :]
    carry_scr[0:1, :] = c[tm - 1:tm, :]
    c2 = c * LOG2E
    c_hi = c2.astype(BF16).astype(F32)
    r1 = c2 - c_hi
    c_mid = r1.astype(BF16).astype(F32)
    c_lo = (r1 - c_mid).astype(BF16).astype(F32)
    aug = (c_hi + pltpu.roll(c_mid, DT_LANE0, 1) + pltpu.roll(c_lo, 2 * DT_LANE0, 1)
           + jnp.where(lane == 3 * DT_LANE0, 1.0, 0.0)).astype(BF16)
    aq_ref[...] = _dot(aug, eq_ref[...]).astype(BF16)
    ak_ref[...] = _dot(aug, ek_ref[...]).astype(BF16)


def _aug_scatter_mats():
    n_pairs = N_ATT_HEADS // 2
    eq = np.zeros((LANES, n_pairs * LANES), np.float32)
    ek = np.zeros((LANES, n_pairs * LANES), np.float32)
    one_lane = 3 * DT_LANE0
    for h in range(N_ATT_HEADS):
        base = (h // 2) * LANES + (0 if h % 2 else HEAD_DIM)
        for p in range(3):
            eq[p * DT_LANE0 + h, base + p] = 1.0
            eq[one_lane, base + 3 + p] = 1.0
            ek[one_lane, base + p] = 1.0
            ek[p * DT_LANE0 + h, base + 3 + p] = -1.0
    return jnp.asarray(eq, BF16), jnp.asarray(ek, BF16)


def _slab_rows(rows, n_steps):
    rb = -(-rows // n_steps)
    rb += -rb % 16
    while rows % rb:
        rb += 16
    return rb


def _inproj_aug(x1, norm_g, ws, bias_s, conv_w, conv_b, w_a, w_b, *, tm, seq):
    n = x1.shape[0]
    n_steps = n // tm

    def cast_spec(w):
        rb = _slab_rows(w.shape[0], n_steps)
        last = w.shape[0] // rb - 1
        return pl.BlockSpec((rb, w.shape[1]), lambda i: (jnp.minimum(i, last), 0))

    eq, ek = _aug_scatter_mats()
    row = lambda w: pl.BlockSpec((tm, w), lambda i: (i, 0))
    full = lambda a: pl.BlockSpec(a.shape, lambda i: (0, 0))
    wq, wk, wv, wz, wx, wsm = ws
    tps = seq // tm
    nseq = n // seq
    kv_t = pl.BlockSpec((1, D_ATT, tm), lambda i: (i // tps, 0, i % tps))
    tail = pl.BlockSpec((1, 8, CONV_DIM), lambda i: (i // tps, 0, 0))
    sds = jax.ShapeDtypeStruct
    return pl.pallas_call(
        functools.partial(_inproj_aug_body, tiles_per_seq=tps, tm=tm),
        grid=(n // tm,),
        in_specs=[row(D_MODEL), full(norm_g), full(wq), full(wk), full(wv), full(wz), full(wx),
                  full(wsm), full(bias_s), full(eq), full(ek), full(conv_w), full(conv_b),
                  cast_spec(w_a), cast_spec(w_b)],
        out_specs=[kv_t, kv_t, row(D_ATT), row(D_ATT), row(D_ATT), row(D_ATT), row(D_ATT),
                   row(D_SSM), row(CONV_DIM), tail, row(LANES),
                   pl.BlockSpec((1, DT_LANE0, tm), lambda i: (i // tps, 0, i % tps)),
                   cast_spec(w_a), cast_spec(w_b)],
        out_shape=[
            sds((nseq, D_ATT, seq), F32), sds((nseq, D_ATT, seq), F32),
            sds((n, D_ATT), BF16), sds((n, D_ATT), BF16), sds((n, D_ATT), BF16),
            sds((n, D_ATT), BF16), sds((n, D_ATT), BF16),
            sds((n, D_SSM), F32), sds((n, CONV_DIM), F32), sds((nseq, 8, CONV_DIM), F32), sds((n, LANES), F32),
            sds((nseq, DT_LANE0, seq), F32),
            sds(w_a.shape, BF16), sds(w_b.shape, BF16)],
        scratch_shapes=[pltpu.VMEM((8, LANES), F32), pltpu.VMEM((8 + tm, CONV_DIM), F32)],
        compiler_params=_cparams(("arbitrary",)),
        name="inproj_prompt",
    )(x1, norm_g, wq, wk, wv, wz, wx, wsm, bias_s, eq, ek, conv_w, conv_b, w_a, w_b)


def _inproj_plain_body(x_ref, g_ref, wq_ref, wk_ref, wv_ref, wz_ref, wx_ref, ws_ref, bs_ref,
                       q_ref, k_ref, v_ref, z_ref, xbc_ref, slab_ref):
    h = _rms(x_ref[...], g_ref[...]).astype(BF16)
    q_ref[...] = _dot(h, wq_ref[...])
    k_ref[...] = _dot(h, wk_ref[...])
    v_ref[...] = _dot(h, wv_ref[...])
    z_ref[...] = _dot(h, wz_ref[...])
    xbc_ref[...] = _dot(h, wx_ref[...])
    slab_ref[...] = _small_slab(_dot(h, ws_ref[...]), bs_ref[...])


def _inproj_plain(x1, norm_g, ws, bias_s):
    n = x1.shape[0]
    wq, wk, wv, wz, wx, wsm = ws
    full = lambda a: pl.BlockSpec(a.shape, lambda i: (0, 0))
    out = lambda w: pl.BlockSpec((n, w), lambda i: (0, 0))
    return pl.pallas_call(
        _inproj_plain_body,
        grid=(1,),
        in_specs=[full(x1), full(norm_g), full(wq), full(wk), full(wv), full(wz), full(wx),
                  full(wsm), full(bias_s)],
        out_specs=[out(D_ATT), out(D_ATT), out(D_ATT), out(D_SSM), out(CONV_DIM), out(LANES)],
        out_shape=[jax.ShapeDtypeStruct((n, w), F32)
                   for w in (D_ATT, D_ATT, D_ATT, D_SSM, CONV_DIM, LANES)],
        compiler_params=_cparams(("arbitrary",)),
        name="inproj_sample",
    )(x1, norm_g, wq, wk, wv, wz, wx, wsm, bias_s)


def _head_expand_mat():
    e = np.zeros((LANES, D_SSM), np.float32)
    for h in range(N_SSM_HEADS):
        e[DT_LANE0 + h, h * SSM_HEAD_DIM:(h + 1) * SSM_HEAD_DIM] = 1.0
    return jnp.asarray(e, BF16)


def _gate_groupnorm(y, z, gain):
    g = y * _silu(z)
    half = D_SSM // N_SSM_GROUPS
    parts = []
    for gi in range(N_SSM_GROUPS):
        gg = g[:, gi * half:(gi + 1) * half]
        parts.append(gg * lax.rsqrt(jnp.mean(gg * gg, axis=-1, keepdims=True) + EPS))
    return jnp.concatenate(parts, axis=1) * gain


def _ssd_chunk(xc, slab, z, ht_old, arow, dsk, sg, e8):
    Q = SSD_CHUNK
    xs = xc[:, :D_SSM]
    lane = lax.broadcasted_iota(jnp.int32, slab.shape, 1)
    dtm = jnp.where((lane >= DT_LANE0) & (lane < 2 * DT_LANE0), slab, 0.0)
    a = dtm * arow
    a_c = _dot3r(_tril(Q), a)
    a_ct = a_c.T
    a_last = a_c[Q - 1:Q, :]
    ea_x = _dot3(jnp.exp(a_c), e8)
    dec_x = _dot3(jnp.exp(a_last - a_c), e8)
    xdt = xs * _dot3(dtm, e8)
    xdt_b = xdt.astype(BF16)
    xd_b = (xdt * dec_x).astype(BF16)
    cd_x = ea_x[Q - 1:Q, :]

    r_i = lax.broadcasted_iota(jnp.int32, (Q, Q), 0)
    c_i = lax.broadcasted_iota(jnp.int32, (Q, Q), 1)
    tri = c_i <= r_i
    lane_q = lax.broadcasted_iota(jnp.int32, (Q, LANES), 1)
    gw = D_SSM // N_SSM_GROUPS
    hpg = N_SSM_HEADS // N_SSM_GROUPS
    y_parts, ht_parts = [], []
    for g in range(N_SSM_GROUPS):
        bm = xc[:, D_SSM + g * D_STATE:D_SSM + (g + 1) * D_STATE]
        cm = xc[:, D_SSM + (N_SSM_GROUPS + g) * D_STATE:D_SSM + (N_SSM_GROUPS + g + 1) * D_STATE]
        cb16 = cm.astype(BF16)
        cbm = _dot_nt(cb16, bm.astype(BF16))
        y_off = _dot(cb16, ht_old[:, g * gw:(g + 1) * gw].astype(BF16))
        y_diag = []
        for pr in range(hpg // 2):
            halves = []
            for hh in range(2):
                h = g * hpg + pr * 2 + hh
                col = a_c[:, DT_LANE0 + h:DT_LANE0 + h + 1]
                row = a_ct[DT_LANE0 + h:DT_LANE0 + h + 1, :]
                lm = jnp.where(tri, jnp.exp(col - row), 0.0)
                sc = (cbm * lm).astype(BF16)
                lo = g * gw + pr * LANES
                halves.append(_dot(sc, xdt_b[:, lo:lo + LANES]))
            y_diag.append(jnp.where(lane_q < SSM_HEAD_DIM, halves[0], halves[1]))
        y_parts.append(jnp.concatenate(y_diag, axis=1) + y_off * ea_x[:, g * gw:(g + 1) * gw])
        new = _dot(bm.T.astype(BF16), xd_b[:, g * gw:(g + 1) * gw])
        ht_parts.append(ht_old[:, g * gw:(g + 1) * gw] * cd_x[:, g * gw:(g + 1) * gw] + new)
    y = jnp.concatenate(y_parts, axis=1) + dsk * xs
    return _gate_groupnorm(y, z, sg).astype(BF16), jnp.concatenate(ht_parts, axis=1)


def _ssd_body(xc_ref, slab_ref, z_ref, arow_ref, dsk_ref, sg_ref, e8_ref, gn_ref, st_ref, ht_scr,
              *, n_steps, cps):
    c = pl.program_id(1)
    Q = SSD_CHUNK

    @pl.when(c == 0)
    def _():
        ht_scr[...] = jnp.zeros_like(ht_scr)

    ht = ht_scr[...]
    for ci in range(cps):
        rows = slice(ci * Q, (ci + 1) * Q)
        gn, ht = _ssd_chunk(xc_ref[0, rows, :], slab_ref[0, rows, :], z_ref[0, rows, :], ht,
                            arow_ref[...], dsk_ref[...], sg_ref[...], e8_ref[...])
        gn_ref[0, rows, :] = gn
    ht_scr[...] = ht

    @pl.when(c == n_steps - 1)
    def _():
        st_ref[0] = ht.T


def _ssd_prompt(xc, slab, z, a_row, dsk_x, ssm_g, *, cps):
    b, L, _ = xc.shape
    rows = cps * SSD_CHUNK
    ns = L // rows
    e8 = _head_expand_mat()
    blk = lambda w: pl.BlockSpec((1, rows, w), lambda bi, ci: (bi, ci, 0))
    full = lambda a: pl.BlockSpec(a.shape, lambda bi, ci: (0, 0))
    return pl.pallas_call(
        functools.partial(_ssd_body, n_steps=ns, cps=cps),
        grid=(b, ns),
        in_specs=[blk(CONV_DIM), blk(LANES), blk(D_SSM), full(a_row), full(dsk_x), full(ssm_g), full(e8)],
        out_specs=[blk(D_SSM), pl.BlockSpec((1, D_SSM, D_STATE), lambda bi, ci: (bi, 0, 0))],
        out_shape=[jax.ShapeDtypeStruct((b, L, D_SSM), BF16),
                   jax.ShapeDtypeStruct((b, D_SSM, D_STATE), F32)],
        scratch_shapes=[pltpu.VMEM((D_STATE, D_SSM), F32)],
        compiler_params=_cparams(("arbitrary", "arbitrary")),
        name="ssd_prompt",
    )(xc, slab, z, a_row, dsk_x, ssm_g, e8)


def _ssd_step_body(xbc_ref, sc_ref, slab_ref, z_ref, st_ref, cw_ref, cb_ref, arow_ref, dsk_ref,
                   sg_ref, e8_ref, gn_ref, so_ref, xc_scr, dtx_scr, decx_scr, *, spb):
    step = pl.program_id(0)

    @pl.when(step == 0)
    def _():
        conv = (cb_ref[...] + cw_ref[0:1, :] * sc_ref[0] + cw_ref[1:2, :] * sc_ref[1]
                + cw_ref[2:3, :] * sc_ref[2] + cw_ref[3:4, :] * xbc_ref[...])
        xc_scr[...] = _silu(conv)
        slab = slab_ref[...]
        lane = lax.broadcasted_iota(jnp.int32, slab.shape, 1)
        dtm = jnp.where((lane >= DT_LANE0) & (lane < 2 * DT_LANE0), slab, 0.0)
        e8 = e8_ref[...]
        dtx_scr[...] = _dot3(dtm, e8)
        decx_scr[...] = _dot3(jnp.exp(dtm * arow_ref[...]), e8)

    gw = D_SSM // N_SSM_GROUPS
    for i in range(spb):
        b = step * spb + i
        xrow = xc_scr[pl.ds(b, 1), :]
        xs = xrow[:, :D_SSM]
        xdt = xs * dtx_scr[pl.ds(b, 1), :]
        dec = decx_scr[pl.ds(b, 1), :]
        y_rows = []
        for j in range(D_SSM // LANES):
            g = (j * LANES) // gw
            bm = xrow[:, D_SSM + g * D_STATE:D_SSM + (g + 1) * D_STATE]
            cm = xrow[:, D_SSM + (N_SSM_GROUPS + g) * D_STATE:D_SSM + (N_SSM_GROUPS + g + 1) * D_STATE]
            xcol = _row_to_col(xdt[:, j * LANES:(j + 1) * LANES])
            dcol = _row_to_col(dec[:, j * LANES:(j + 1) * LANES])
            hs = dcol * st_ref[i, j * LANES:(j + 1) * LANES, :] + xcol * bm
            so_ref[i, j * LANES:(j + 1) * LANES, :] = hs
            y_rows.append(_col_to_row(jnp.sum(hs * cm, axis=1, keepdims=True)))
        y = jnp.concatenate(y_rows, axis=1) + dsk_ref[...] * xs
        gn_ref[i] = _gate_groupnorm(y, z_ref[pl.ds(b, 1), :], sg_ref[...]).astype(BF16)


def _ssd_step(xbc, sconv_t, slab, z, state, conv_w, conv_b, a_row, dsk_x, ssm_g, *, spb):
    nb = xbc.shape[0]
    assert nb % spb == 0
    e8 = _head_expand_mat()
    full2 = lambda a: pl.BlockSpec(a.shape, lambda bi: (0,) * a.ndim)
    gn, st = pl.pallas_call(
        functools.partial(_ssd_step_body, spb=spb),
        grid=(nb // spb,),
        in_specs=[full2(xbc), full2(sconv_t), full2(slab), full2(z),
                  pl.BlockSpec((spb, D_SSM, D_STATE), lambda bi: (bi, 0, 0)),
                  full2(conv_w), full2(conv_b), full2(a_row), full2(dsk_x), full2(ssm_g), full2(e8)],
        out_specs=[pl.BlockSpec((spb, 1, D_SSM), lambda bi: (bi, 0, 0)),
                   pl.BlockSpec((spb, D_SSM, D_STATE), lambda bi: (bi, 0, 0))],
        out_shape=[jax.ShapeDtypeStruct((nb, 1, D_SSM), BF16),
                   jax.ShapeDtypeStruct((nb, D_SSM, D_STATE), F32)],
        scratch_shapes=[pltpu.VMEM((nb, CONV_DIM), F32), pltpu.VMEM((nb, D_SSM), F32),
                        pltpu.VMEM((nb, D_SSM), F32)],
        compiler_params=_cparams(("arbitrary",)),
        name="ssd_step",
    )(xbc, sconv_t, slab, z, state, conv_w, conv_b, a_row, dsk_x, ssm_g, e8)
    return gn.reshape(nb, D_SSM), st


DECODE_GROUP = 16
DECODE_SLOTS = 3


def _fox_body(pt_ref, q_ref, k_ref, v_ref, aq_ref, ak_ref, qs_ref, ks_ref, vs_ref, lfs_ref,
              ck_hbm, cv_hbm, clf_hbm, o_ref, os_ref,
              s_scr, m_scr, acc_scr, qp_scr, kbuf, vbuf, lbuf, sems, qblk_scr, dm_scr, dl_scr, dacc_scr,
              carry_scr, *, tq, nt, gps, gpseq, n_groups):
    G, NS = DECODE_GROUP, DECODE_SLOTS
    H, HD, P = N_ATT_HEADS, HEAD_DIM, PAGE_SIZE
    t = pl.program_id(2)
    sid = (pl.program_id(0) * pl.num_programs(1) + pl.program_id(1)) * pl.num_programs(2) + t
    nl = tq // LANES

    def group_copies(gg):
        slot = gg % NS
        src = jnp.minimum(gg, n_groups - 1)
        seq = src // gpseq
        base = (gpseq - 1 - src % gpseq) * G
        copies = []
        for i in range(G):
            page = pt_ref[seq, base + i]
            copies.append(pltpu.make_async_copy(ck_hbm.at[page], kbuf.at[slot, i], sems.at[slot, 0]))
            copies.append(pltpu.make_async_copy(cv_hbm.at[page], vbuf.at[slot, i], sems.at[slot, 1]))
            copies.append(pltpu.make_async_copy(clf_hbm.at[page], lbuf.at[slot, i], sems.at[slot, 2]))
        return copies

    def start_group(gg):
        for n, cp in enumerate(group_copies(gg)):
            cp.start(priority=1 if n % 3 == 1 else 0)

    def wait_group(gg):
        for cp in group_copies(gg):
            cp.wait()

    r8 = lax.broadcasted_iota(jnp.int32, (H, D_ATT), 0)
    c8 = lax.broadcasted_iota(jnp.int32, (H, D_ATT), 1)
    own_head = c8 // HD == r8

    def decode_init():
        seq = (sid * gps) // gpseq
        qs = (qs_ref[seq] * ATT_SCALE).astype(BF16).astype(F32)
        qblk = jnp.where(own_head, qs, 0.0)
        qblk_scr[...] = qblk.astype(BF16)
        k2 = ks_ref[seq].astype(BF16).astype(F32)
        dm_scr[...] = jnp.broadcast_to(jnp.sum(qblk * k2, axis=1, keepdims=True), dm_scr.shape)
        dl_scr[...] = jnp.ones_like(dl_scr)
        dacc_scr[...] = jnp.broadcast_to(vs_ref[seq].astype(BF16).astype(F32), dacc_scr.shape)
        carry_scr[...] = jnp.broadcast_to(_row_to_col(lfs_ref[seq])[0:H, :], carry_scr.shape)

    def decode_group(slot):
        kcat = jnp.concatenate([kbuf[slot, i].astype(BF16) for i in range(G)], axis=1)
        vcat = jnp.concatenate([vbuf[slot, i].astype(BF16) for i in range(G)], axis=1)
        s = _dot(qblk_scr[...], kcat)
        x = jnp.concatenate([lbuf[slot, i] for i in range(G)], axis=0)
        rr = lax.broadcasted_iota(jnp.int32, (P, P), 0)
        cc = lax.broadcasted_iota(jnp.int32, (P, P), 1)
        rev_local = _dot3(x, (rr > cc).astype(BF16))
        tot = jnp.sum(x, axis=1, keepdims=True)
        carry = carry_scr[:, 0:1]
        s_pages = [None] * G
        for i in reversed(range(G)):
            s_pages[i] = s[:, i * P:(i + 1) * P] + rev_local[i * H:(i + 1) * H, :] + carry
            carry = carry + tot[i * H:(i + 1) * H, :]
        carry_scr[...] = jnp.broadcast_to(carry, carry_scr.shape)
        st = jnp.concatenate(s_pages, axis=1)
        m_old = dm_scr[:, 0:1]
        m_new = jnp.maximum(m_old, jnp.max(st, axis=1, keepdims=True))
        alpha = jnp.exp(m_old - m_new)
        p = jnp.exp(st - m_new)
        l_new = alpha * dl_scr[:, 0:1] + jnp.sum(p, axis=1, keepdims=True)
        dacc_scr[...] = alpha * dacc_scr[...] + _dot_nt(p.astype(BF16), vcat)
        dm_scr[...] = jnp.broadcast_to(m_new, dm_scr.shape)
        dl_scr[...] = jnp.broadcast_to(l_new, dl_scr.shape)

    def decode_finish():
        o = dacc_scr[...] / dl_scr[:, 0:1]
        os_ref[0] = jnp.sum(jnp.where(own_head, o, 0.0), axis=0, keepdims=True)

    r = lax.broadcasted_iota(jnp.int32, (tq, tq), 0)
    c = lax.broadcasted_iota(jnp.int32, (tq, tq), 1)
    causal = c <= r

    def chunk_of(ci):
        first = ci <= t
        sel = jnp.where(first, 0, 1)
        q0 = pl.multiple_of(jnp.where(first, t, nt - 1 - t) * tq, tq)
        k0 = pl.multiple_of(jnp.where(first, ci, ci - t - 1) * tq, tq)
        return sel, q0, k0

    lane_q = lax.broadcasted_iota(jnp.int32, (tq, LANES), 1)
    even_dims = lane_q < HD

    def per_head(x2, a2):
        return jnp.where(even_dims, x2, a2), jnp.where(even_dims, a2, x2)

    def build_queries():
        for x, tile in enumerate((t, nt - 1 - t)):
            rows = pl.ds(pl.multiple_of(tile * tq, tq), tq)
            q_e, q_o = per_head(q_ref[0, rows, :], aq_ref[0, rows, :])
            qp_scr[x, 0] = q_e
            qp_scr[x, 1] = q_o

    def scores(ci):
        sel, q0, k0 = chunk_of(ci)
        ks = per_head(k_ref[0, pl.ds(k0, tq), :], ak_ref[0, pl.ds(k0, tq), :])
        for j in range(2):
            s = _dot_nt(qp_scr[sel, j], ks[j])
            if ci == nt:
                s = jnp.where(causal, s, NEG_BIG)
            elif ci < nt // 2:
                s = jnp.where(jnp.logical_or(causal, ci != t), s, NEG_BIG)
            s_scr[j, ci] = s
            m = m_scr[sel, j]
            for u in range(nl):
                m = jnp.maximum(m, s[:, u * LANES:(u + 1) * LANES])
            m_scr[sel, j] = m

    def row_max():
        for x in range(2):
            for j in range(2):
                m_scr[x, j] = jnp.broadcast_to(jnp.max(m_scr[x, j], axis=1, keepdims=True), (tq, LANES))

    one_at = lambda ln: jnp.where(lane_q == ln, 1.0, 0.0).astype(BF16)
    ones_e, ones_o = one_at(HD), one_at(0)

    def weighted(ci):
        sel, q0, k0 = chunk_of(ci)
        v2 = v_ref[0, pl.ds(k0, tq), :]
        vs = (jnp.where(even_dims, v2, ones_e), jnp.where(even_dims, ones_o, v2))
        for j in range(2):
            s = s_scr[j, ci]
            mrep = m_scr[sel, j]
            p = jnp.concatenate([jnp.exp2(s[:, u * LANES:(u + 1) * LANES] - mrep) for u in range(nl)], axis=1)
            acc_scr[sel, j] += _dot(p.astype(BF16), vs[j])

    def write_out():
        lane = lax.broadcasted_iota(jnp.int32, (tq, LANES), 1)
        for x, tile in enumerate((t, nt - 1 - t)):
            acc_e, acc_o = acc_scr[x, 0], acc_scr[x, 1]
            out = jnp.where(lane < HEAD_DIM, acc_e / acc_e[:, HEAD_DIM:HEAD_DIM + 1], acc_o / acc_o[:, 0:1])
            o_ref[0, pl.ds(pl.multiple_of(tile * tq, tq), tq), :] = out

    items = ([functools.partial(scores, ci) for ci in range(nt + 1)] + [row_max]
             + [functools.partial(weighted, ci) for ci in range(nt + 1)] + [write_out])
    per_part = -(-len(items) // gps)
    parts = [items[i * per_part:(i + 1) * per_part] for i in range(gps)]

    @pl.when(sid == 0)
    def _():
        for g0 in range(NS - 1):
            start_group(g0)

    pl.when((sid * gps) % gpseq == 0)(decode_init)
    m_scr[...] = jnp.full(m_scr.shape, NEG_BIG, F32)
    acc_scr[...] = jnp.zeros_like(acc_scr)
    build_queries()
    for gi in range(gps):
        gg = sid * gps + gi
        wait_group(gg)
        start_group(gg + (NS - 1))
        for item in parts[gi]:
            item()
        decode_group(gg % NS)
    pl.when((sid * gps + gps - 1) % gpseq == gpseq - 1)(decode_finish)

    @pl.when(sid == pl.num_programs(0) * pl.num_programs(1) * pl.num_programs(2) - 1)
    def _():
        for extra in range(NS - 1):
            wait_group(n_groups + extra)


def _fox_attention(q16, k16, v16, aq, ak, q_s, k_s, v_s, slab_s, cache_k, cache_v, cache_logf, page_table,
                   *, tq):
    b, L, _ = q16.shape
    nt = L // tq
    nb, n_pages = page_table.shape
    n_steps = b * (N_ATT_HEADS // 2) * (nt // 2)
    n_groups = nb * n_pages // DECODE_GROUP
    gps = n_groups // n_steps
    gpseq = n_pages // DECODE_GROUP
    assert nt % 2 == 0 and n_pages % DECODE_GROUP == 0
    assert gps * n_steps == n_groups and gpseq % gps == 0 and n_groups >= DECODE_SLOTS
    spq = gpseq // gps
    hp_n, t_n = N_ATT_HEADS // 2, nt // 2
    seq_of = lambda bi, hp, t: ((bi * hp_n + hp) * t_n + t) // spq
    pair = pl.BlockSpec((1, L, LANES), lambda bi, hp, t, pt: (bi, 0, hp), pipeline_mode=pl.Buffered(1))
    srow = lambda w: pl.BlockSpec((nb, 1, w), lambda bi, hp, t, pt: (0, 0, 0), pipeline_mode=pl.Buffered(1))
    anyspec = pl.BlockSpec(memory_space=pl.ANY)
    row3 = lambda a: a.reshape(nb, 1, a.shape[-1])
    G, NS = DECODE_GROUP, DECODE_SLOTS
    att, att_s = pl.pallas_call(
        functools.partial(_fox_body, tq=tq, nt=nt, gps=gps, gpseq=gpseq, n_groups=n_groups),
        grid_spec=pltpu.PrefetchScalarGridSpec(
            num_scalar_prefetch=1,
            grid=(b, hp_n, t_n),
            in_specs=[pair, pair, pair, pair, pair, srow(D_ATT), srow(D_ATT), srow(D_ATT), srow(LANES),
                      anyspec, anyspec, anyspec],
            out_specs=[pl.BlockSpec((1, L, LANES), lambda bi, hp, t, pt: (bi, 0, hp)),
                       pl.BlockSpec((1, 1, D_ATT), lambda bi, hp, t, pt: (seq_of(bi, hp, t), 0, 0))],
            scratch_shapes=[
                pltpu.VMEM((2, nt + 1, tq, tq), F32), pltpu.VMEM((2, 2, tq, LANES), F32),
                pltpu.VMEM((2, 2, tq, LANES), F32), pltpu.VMEM((2, 2, tq, LANES), BF16),
                pltpu.VMEM((NS, G, D_ATT, PAGE_SIZE), F32), pltpu.VMEM((NS, G, D_ATT, PAGE_SIZE), F32),
                pltpu.VMEM((NS, G, N_ATT_HEADS, PAGE_SIZE), F32), pltpu.SemaphoreType.DMA((NS, 3)),
                pltpu.VMEM((N_ATT_HEADS, D_ATT), BF16), pltpu.VMEM((8, LANES), F32),
                pltpu.VMEM((8, LANES), F32), pltpu.VMEM((8, D_ATT), F32), pltpu.VMEM((8, LANES), F32)]),
        out_shape=[jax.ShapeDtypeStruct((b, L, D_ATT), F32), jax.ShapeDtypeStruct((nb, 1, D_ATT), F32)],
        compiler_params=_cparams(("arbitrary", "arbitrary", "arbitrary")),
        name="fox_attention",
    )(page_table, q16, k16, v16, aq, ak, row3(q_s), row3(k_s), row3(v_s), row3(slab_s),
      cache_k, cache_v, cache_logf)
    return att, att_s.reshape(nb, D_ATT)


def _lane_pad(vec, lane0):
    return jnp.pad(vec.astype(F32)[None, :], ((0, 0), (lane0, LANES - lane0 - vec.shape[0])))


def kernel(x_prompt, x_sample, cache_k, cache_v, cache_logf, page_table, state_conv, state_ssm,
           ffn1_norm, w_ffn1_in, w_ffn1_out, mix_norm, w_in, b_f, conv_w, conv_b, dt_bias, a_log,
           d_skip, att_out_norm, ssm_out_norm, w_out, ffn2_norm, w_ffn2_in, w_ffn2_out, final_norm):
    depth = w_in.shape[0]
    assert depth == 1
    B, L, _ = x_prompt.shape
    nb, T, _ = x_sample.shape
    assert T == 1
    H, HD = N_ATT_HEADS, HEAD_DIM
    assert H == DT_LANE0

    l0 = 0
    w1i, w1o = w_ffn1_in[l0].astype(BF16), w_ffn1_out[l0].astype(BF16)
    wi = w_in[l0]
    o_f = 3 * D_ATT
    o_z = o_f + H
    o_x = o_z + D_SSM
    o_dt = o_x + CONV_DIM
    w_small = jnp.concatenate(
        [wi[:, o_f:o_f + H], wi[:, o_dt:o_dt + N_SSM_HEADS],
         jnp.zeros((D_MODEL, LANES - DT_LANE0 - N_SSM_HEADS), wi.dtype)], axis=1)
    ws = tuple(w.astype(BF16) for w in (wi[:, 0:D_ATT], wi[:, D_ATT:2 * D_ATT], wi[:, 2 * D_ATT:3 * D_ATT],
                                        wi[:, o_z:o_z + D_SSM], wi[:, o_x:o_x + CONV_DIM], w_small))
    bias_s = _lane_pad(b_f[l0], 0) + _lane_pad(dt_bias[l0], DT_LANE0)
    a_row = _lane_pad(-jnp.exp(a_log[l0].astype(F32)), DT_LANE0)
    dsk_x = jnp.repeat(d_skip[l0].astype(F32), SSM_HEAD_DIM)[None, :]
    row = lambda v: v.astype(F32)[None, :]
    g1, gm, g2, gf = row(ffn1_norm[l0]), row(mix_norm[l0]), row(ffn2_norm[l0]), row(final_norm)
    ga, gs = row(att_out_norm[l0]), row(ssm_out_norm[l0])
    cw, cb = conv_w[l0].astype(F32), row(conv_b[l0])
    wm = w_out[l0].astype(BF16)

    TM, TF = FFN_ROWS, FFN_CHUNK
    xp = x_prompt.reshape(B * L, D_MODEL)
    x1 = _ffn(xp, g1, w1i, w1o, tm=TM, tf=TF)
    (k_p, v_p, q16, k16, v16, aq, ak, z_p, xc_p, tail_p, slab_p, lf_p, w2i, w2o) = _inproj_aug(
        x1, gm, ws, bias_s, cw, cb, w_ffn2_in[l0].astype(F32), w_ffn2_out[l0].astype(F32),
        tm=INPROJ_ROWS, seq=L)
    xs = x_sample.reshape(nb, D_MODEL)
    x1s = _ffn(xs, g1, w1i, w1o, tm=nb, tf=TF)
    q_s, k_s, v_s, z_s, xbc_s, slab_s = _inproj_plain(x1s, gm, ws, bias_s)

    n_pool = cache_k.shape[1]
    ck = jnp.transpose(cache_k[l0], (0, 2, 3, 1)).reshape(n_pool, D_ATT, PAGE_SIZE)
    cv = jnp.transpose(cache_v[l0], (0, 2, 3, 1)).reshape(n_pool, D_ATT, PAGE_SIZE)
    clf = jnp.transpose(cache_logf[l0], (0, 2, 1))
    seq3 = lambda a: a.reshape(B, L, a.shape[-1])
    att, att_s = _fox_attention(seq3(q16), seq3(k16), seq3(v16), seq3(aq), seq3(ak),
                                q_s, k_s, v_s, slab_s, ck, cv, clf, page_table, tq=ATT_ROWS)

    gn_p, st_p = _ssd_prompt(seq3(xc_p), seq3(slab_p), seq3(z_p), a_row, dsk_x, gs, cps=SSD_CHUNKS_PER_STEP)
    y_p = _ffn2(x1, att.reshape(B * L, D_ATT), gn_p.reshape(B * L, D_SSM), ga, wm, g2, w2i, w2o, gf,
                tm=TM, tf=TF)

    sconv_t = jnp.swapaxes(state_conv[l0], 0, 1)
    gn_s, st_s = _ssd_step(xbc_s, sconv_t, slab_s, z_s,
                           state_ssm[l0].reshape(nb, D_SSM, D_STATE), cw, cb, a_row, dsk_x, gs,
                           spb=SSD_STEP_SEQS if nb % SSD_STEP_SEQS == 0 else 1)
    y_s = _ffn2(x1s, att_s, gn_s, ga, wm, g2, w2i, w2o, gf, tm=nb, tf=TF)

    return (
        y_p.reshape(B, L, D_MODEL),
        y_s.reshape(nb, 1, D_MODEL),
        jnp.transpose(k_p.reshape(1, B, H, HD, L), (0, 1, 4, 2, 3)),
        jnp.transpose(v_p.reshape(1, B, H, HD, L), (0, 1, 4, 2, 3)),
        jnp.transpose(lf_p.reshape(1, B, H, L), (0, 1, 3, 2)),
        tail_p[:, 8 - (CONV_W - 1):, :][None],
        st_p.reshape(1, B, N_SSM_HEADS, SSM_HEAD_DIM, D_STATE),
        k_s.reshape(1, nb, 1, H, HD),
        v_s.reshape(1, nb, 1, H, HD),
        slab_s[:, :H].reshape(1, nb, 1, H),
        jnp.concatenate([state_conv[l0][:, 1:, :], xbc_s[:, None, :]], axis=1)[None],
        st_s.reshape(1, nb, N_SSM_HEADS, SSM_HEAD_DIM, D_STATE),
    )
```

```python
import functools

import numpy as np
import jax
import jax.numpy as jnp
from jax import lax
from jax.experimental import pallas as pl
from jax.experimental.pallas import tpu as pltpu

F32 = jnp.float32
BF16 = jnp.bfloat16

D_MODEL = 1024
D_ATT = 512
D_SSM = 512
HEAD_DIM = 64
N_ATT_HEADS = 8
N_SSM_HEADS = 8
SSM_HEAD_DIM = 64
N_SSM_GROUPS = 2
D_STATE = 128
CONV_W = 4
CONV_DIM = D_SSM + 2 * N_SSM_GROUPS * D_STATE
SSD_CHUNK = 128
PAGE_SIZE = 128
D_FF = 2816
FFN_RESIDUAL = 0.5
EPS = 1e-6
ATT_SCALE = HEAD_DIM ** -0.5
LOG2E = 1.4426950408889634
LANES = 128
DT_LANE0 = 8
NEG_BIG = -1e30

VMEM_LIMIT = 56 * 1024 * 1024
FFN_ROWS = 1024
FFN_CHUNK = 256
INPROJ_ROWS = 512
ATT_ROWS = 256
SSD_CHUNKS_PER_STEP = 16
SSD_STEP_SEQS = 8


def _cparams(sem):
    return pltpu.CompilerParams(dimension_semantics=sem, vmem_limit_bytes=VMEM_LIMIT)


def _dot(a, b):
    return jnp.dot(a, b, preferred_element_type=F32)


def _dot_nt(a, b):
    return lax.dot_general(a, b, (((1,), (1,)), ((), ())), preferred_element_type=F32)


def _split3(x):
    hi = x.astype(BF16)
    r1 = x - hi.astype(F32)
    mid = r1.astype(BF16)
    lo = (r1 - mid.astype(F32)).astype(BF16)
    return hi, mid, lo


def _dot3(x, m):
    hi, mid, lo = _split3(x)
    return _dot(hi, m) + _dot(mid, m) + _dot(lo, m)


def _dot3r(m, x):
    hi, mid, lo = _split3(x)
    return _dot(m, hi) + _dot(m, mid) + _dot(m, lo)


def _rms(x, g):
    return x * lax.rsqrt(jnp.mean(x * x, axis=-1, keepdims=True) + EPS) * g


def _silu(x):
    return x / (1.0 + jnp.exp(-x))


def _tril(n, dtype=BF16):
    r = lax.broadcasted_iota(jnp.int32, (n, n), 0)
    c = lax.broadcasted_iota(jnp.int32, (n, n), 1)
    return (c <= r).astype(dtype)


def _row_to_col(row):
    n = row.shape[1]
    r = lax.broadcasted_iota(jnp.int32, (n, n), 0)
    c = lax.broadcasted_iota(jnp.int32, (n, n), 1)
    return jnp.sum(jnp.where(r == c, jnp.broadcast_to(row, (n, n)), 0.0), axis=1, keepdims=True)


def _col_to_row(col):
    n = col.shape[0]
    r = lax.broadcasted_iota(jnp.int32, (n, n), 0)
    c = lax.broadcasted_iota(jnp.int32, (n, n), 1)
    return jnp.sum(jnp.where(r == c, jnp.broadcast_to(col, (n, n)), 0.0), axis=0, keepdims=True)


def _resident(shape):
    return pl.BlockSpec(shape, lambda i: (0,) * len(shape), pipeline_mode=pl.Buffered(1))


def _swiglu_half(x, g_ref, wi_ref, wo_ref, a_scr, tf):
    h = _rms(x, g_ref[...]).astype(BF16)
    for c in range(D_FF // tf):
        gate = _dot(h, wi_ref[:, c * tf:(c + 1) * tf])
        up = _dot(h, wi_ref[:, D_FF + c * tf:D_FF + (c + 1) * tf])
        a_scr[:, c * tf:(c + 1) * tf] = (_silu(gate) * up).astype(BF16)
    return x + FFN_RESIDUAL * _dot(a_scr[...], wo_ref[...])


def _ffn_body(x_ref, g_ref, wi_ref, wo_ref, o_ref, a_scr, *, tf):
    o_ref[...] = _swiglu_half(x_ref[...], g_ref, wi_ref, wo_ref, a_scr, tf)


def _ffn(x, norm_g, w_in_b, w_out_b, *, tm, tf):
    n = x.shape[0]
    row = lambda w: pl.BlockSpec((tm, w), lambda i: (i, 0))
    return pl.pallas_call(
        functools.partial(_ffn_body, tf=tf),
        grid=(n // tm,),
        in_specs=[row(D_MODEL), _resident(norm_g.shape), _resident(w_in_b.shape), _resident(w_out_b.shape)],
        out_specs=row(D_MODEL),
        out_shape=jax.ShapeDtypeStruct((n, D_MODEL), F32),
        scratch_shapes=[pltpu.VMEM((tm, D_FF), BF16)],
        compiler_params=_cparams(("arbitrary",)),
        name="ffn1",
    )(x, norm_g, w_in_b, w_out_b)


def _ffn2_body(x_ref, att_ref, gn_ref, ag_ref, wm_ref, g_ref, wi_ref, wo_ref, fg_ref, o_ref, a_scr, *, tf):
    an = _rms(att_ref[...], ag_ref[...]).astype(BF16)
    x2 = x_ref[...] + _dot(an, wm_ref[:D_ATT, :]) + _dot(gn_ref[...], wm_ref[D_ATT:, :])
    o_ref[...] = _rms(_swiglu_half(x2, g_ref, wi_ref, wo_ref, a_scr, tf), fg_ref[...])


def _ffn2(x1, att, gn, att_g, w_mix_b, norm_g, w_in_b, w_out_b, final_g, *, tm, tf):
    n = x1.shape[0]
    row = lambda w: pl.BlockSpec((tm, w), lambda i: (i, 0))
    return pl.pallas_call(
        functools.partial(_ffn2_body, tf=tf),
        grid=(n // tm,),
        in_specs=[row(D_MODEL), row(D_ATT), row(D_SSM), _resident(att_g.shape), _resident(w_mix_b.shape),
                  _resident(norm_g.shape), _resident(w_in_b.shape), _resident(w_out_b.shape),
                  _resident(final_g.shape)],
        out_specs=row(D_MODEL),
        out_shape=jax.ShapeDtypeStruct((n, D_MODEL), F32),
        scratch_shapes=[pltpu.VMEM((tm, D_FF), BF16)],
        compiler_params=_cparams(("arbitrary",)),
        name="ffn2",
    )(x1, att, gn, att_g, w_mix_b, norm_g, w_in_b, w_out_b, final_g)


def _small_slab(raw, bias):
    v = raw + bias
    t = jnp.log1p(jnp.exp(-jnp.abs(v)))
    lane = lax.broadcasted_iota(jnp.int32, v.shape, 1)
    logf = -(jnp.maximum(-v, 0.0) + t)
    dt = jnp.maximum(v, 0.0) + t
    return jnp.where(lane < DT_LANE0, logf, jnp.where(lane < 2 * DT_LANE0, dt, 0.0))


def _inproj_aug_body(x_ref, g_ref, wq_ref, wk_ref, wv_ref, wz_ref, wx_ref, ws_ref, bs_ref,
                     eq_ref, ek_ref, cw_ref, cb_ref, wa_ref, wb_ref,
                     k_ref, v_ref, q16_ref, k16_ref, v16_ref, aq_ref, ak_ref, z_ref, xc_ref, tail_ref, slab_ref,
                     lft_ref, wa16_ref, wb16_ref, carry_scr, conv_scr, *, tiles_per_seq, tm):
    i = pl.program_id(0)
    wa16_ref[...] = wa_ref[...].astype(BF16)
    wb16_ref[...] = wb_ref[...].astype(BF16)

    @pl.when(i % tiles_per_seq == 0)
    def _():
        carry_scr[...] = jnp.zeros_like(carry_scr)
        conv_scr[0:8, :] = jnp.zeros((8, CONV_DIM), F32)

    h = _rms(x_ref[...], g_ref[...]).astype(BF16)
    q = _dot(h, wq_ref[...])
    k = _dot(h, wk_ref[...])
    v = _dot(h, wv_ref[...])
    k_ref[0] = k.T
    v_ref[0] = v.T
    q16_ref[...] = (q * (ATT_SCALE * LOG2E)).astype(BF16)
    k16_ref[...] = k.astype(BF16)
    v16_ref[...] = v.astype(BF16)
    z_ref[...] = _dot(h, wz_ref[...])
    slab = _small_slab(_dot(h, ws_ref[...]), bs_ref[...])
    slab_ref[...] = slab
    lft_ref[0] = slab.T[0:DT_LANE0, :]

    u = _dot(h, wx_ref[...])
    conv_scr[8:8 + tm, :] = u
    conv = (cb_ref[...] + cw_ref[0:1, :] * conv_scr[5:5 + tm, :] + cw_ref[1:2, :] * conv_scr[6:6 + tm, :]
            + cw_ref[2:3, :] * conv_scr[7:7 + tm, :] + cw_ref[3:4, :] * u)
    conv_scr[0:8, :] = u[tm - 8:tm, :]
    tail_ref[0] = u[tm - 8:tm, :]
    xc_ref[...] = _silu(conv)

    lane = lax.broadcasted_iota(jnp.int32, slab.shape, 1)
    logf = jnp.where(lane < DT_LANE0, slab, 0.0)
    c = _dot3r(_tril(tm), logf) + carry_scr[0:1, :]
    carry_scr[0:1, :] = c[tm - 1:tm, :]
    c2 = c * LOG2E
    c_hi = c2.astype(BF16).astype(F32)
    r1 = c2 - c_hi
    c_mid = r1.astype(BF16).astype(F32)
    c_lo = (r1 - c_mid).astype(BF16).astype(F32)
    aug = (c_hi + pltpu.roll(c_mid, DT_LANE0, 1) + pltpu.roll(c_lo, 2 * DT_LANE0, 1)
           + jnp.where(lane == 3 * DT_LANE0, 1.0, 0.0)).astype(BF16)
    aq_ref[...] = _dot(aug, eq_ref[...]).astype(BF16)
    ak_ref[...] = _dot(aug, ek_ref[...]).astype(BF16)


def _aug_scatter_mats():
    n_pairs = N_ATT_HEADS // 2
    eq = np.zeros((LANES, n_pairs * LANES), np.float32)
    ek = np.zeros((LANES, n_pairs * LANES), np.float32)
    one_lane = 3 * DT_LANE0
    for h in range(N_ATT_HEADS):
        base = (h // 2) * LANES + (0 if h % 2 else HEAD_DIM)
        for p in range(3):
            eq[p * DT_LANE0 + h, base + p] = 1.0
            eq[one_lane, base + 3 + p] = 1.0
            ek[one_lane, base + p] = 1.0
            ek[p * DT_LANE0 + h, base + 3 + p] = -1.0
    return jnp.asarray(eq, BF16), jnp.asarray(ek, BF16)


def _slab_rows(rows, n_steps):
    rb = -(-rows // n_steps)
    rb += -rb % 16
    while rows % rb:
        rb += 16
    return rb


def _inproj_aug(x1, norm_g, ws, bias_s, conv_w, conv_b, w_a, w_b, *, tm, seq):
    n = x1.shape[0]
    n_steps = n // tm

    def cast_spec(w):
        rb = _slab_rows(w.shape[0], n_steps)
        last = w.shape[0] // rb - 1
        return pl.BlockSpec((rb, w.shape[1]), lambda i: (jnp.minimum(i, last), 0))

    eq, ek = _aug_scatter_mats()
    row = lambda w: pl.BlockSpec((tm, w), lambda i: (i, 0))
    full = lambda a: pl.BlockSpec(a.shape, lambda i: (0, 0))
    wq, wk, wv, wz, wx, wsm = ws
    tps = seq // tm
    nseq = n // seq
    kv_t = pl.BlockSpec((1, D_ATT, tm), lambda i: (i // tps, 0, i % tps))
    tail = pl.BlockSpec((1, 8, CONV_DIM), lambda i: (i // tps, 0, 0))
    sds = jax.ShapeDtypeStruct
    return pl.pallas_call(
        functools.partial(_inproj_aug_body, tiles_per_seq=tps, tm=tm),
        grid=(n // tm,),
        in_specs=[row(D_MODEL), full(norm_g), full(wq), full(wk), full(wv), full(wz), full(wx),
                  full(wsm), full(bias_s), full(eq), full(ek), full(conv_w), full(conv_b),
                  cast_spec(w_a), cast_spec(w_b)],
        out_specs=[kv_t, kv_t, row(D_ATT), row(D_ATT), row(D_ATT), row(D_ATT), row(D_ATT),
                   row(D_SSM), row(CONV_DIM), tail, row(LANES),
                   pl.BlockSpec((1, DT_LANE0, tm), lambda i: (i // tps, 0, i % tps)),
                   cast_spec(w_a), cast_spec(w_b)],
        out_shape=[
            sds((nseq, D_ATT, seq), F32), sds((nseq, D_ATT, seq), F32),
            sds((n, D_ATT), BF16), sds((n, D_ATT), BF16), sds((n, D_ATT), BF16),
            sds((n, D_ATT), BF16), sds((n, D_ATT), BF16),
            sds((n, D_SSM), F32), sds((n, CONV_DIM), F32), sds((nseq, 8, CONV_DIM), F32), sds((n, LANES), F32),
            sds((nseq, DT_LANE0, seq), F32),
            sds(w_a.shape, BF16), sds(w_b.shape, BF16)],
        scratch_shapes=[pltpu.VMEM((8, LANES), F32), pltpu.VMEM((8 + tm, CONV_DIM), F32)],
        compiler_params=_cparams(("arbitrary",)),
        name="inproj_prompt",
    )(x1, norm_g, wq, wk, wv, wz, wx, wsm, bias_s, eq, ek, conv_w, conv_b, w_a, w_b)


def _inproj_plain_body(x_ref, g_ref, wq_ref, wk_ref, wv_ref, wz_ref, wx_ref, ws_ref, bs_ref,
                       q_ref, k_ref, v_ref, z_ref, xbc_ref, slab_ref):
    h = _rms(x_ref[...], g_ref[...]).astype(BF16)
    q_ref[...] = _dot(h, wq_ref[...])
    k_ref[...] = _dot(h, wk_ref[...])
    v_ref[...] = _dot(h, wv_ref[...])
    z_ref[...] = _dot(h, wz_ref[...])
    xbc_ref[...] = _dot(h, wx_ref[...])
    slab_ref[...] = _small_slab(_dot(h, ws_ref[...]), bs_ref[...])


def _inproj_plain(x1, norm_g, ws, bias_s):
    n = x1.shape[0]
    wq, wk, wv, wz, wx, wsm = ws
    full = lambda a: pl.BlockSpec(a.shape, lambda i: (0, 0))
    out = lambda w: pl.BlockSpec((n, w), lambda i: (0, 0))
    return pl.pallas_call(
        _inproj_plain_body,
        grid=(1,),
        in_specs=[full(x1), full(norm_g), full(wq), full(wk), full(wv), full(wz), full(wx),
                  full(wsm), full(bias_s)],
        out_specs=[out(D_ATT), out(D_ATT), out(D_ATT), out(D_SSM), out(CONV_DIM), out(LANES)],
        out_shape=[jax.ShapeDtypeStruct((n, w), F32)
                   for w in (D_ATT, D_ATT, D_ATT, D_SSM, CONV_DIM, LANES)],
        compiler_params=_cparams(("arbitrary",)),
        name="inproj_sample",
    )(x1, norm_g, wq, wk, wv, wz, wx, wsm, bias_s)


def _head_expand_mat():
    e = np.zeros((LANES, D_SSM), np.float32)
    for h in range(N_SSM_HEADS):
        e[DT_LANE0 + h, h * SSM_HEAD_DIM:(h + 1) * SSM_HEAD_DIM] = 1.0
    return jnp.asarray(e, BF16)


def _gate_groupnorm(y, z, gain):
    g = y * _silu(z)
    half = D_SSM // N_SSM_GROUPS
    parts = []
    for gi in range(N_SSM_GROUPS):
        gg = g[:, gi * half:(gi + 1) * half]
        parts.append(gg * lax.rsqrt(jnp.mean(gg * gg, axis=-1, keepdims=True) + EPS))
    return jnp.concatenate(parts, axis=1) * gain


def _ssd_chunk(xc, slab, z, ht_old, arow, dsk, sg, e8):
    Q = SSD_CHUNK
    xs = xc[:, :D_SSM]
    lane = lax.broadcasted_iota(jnp.int32, slab.shape, 1)
    dtm = jnp.where((lane >= DT_LANE0) & (lane < 2 * DT_LANE0), slab, 0.0)
    a = dtm * arow
    a_c = _dot3r(_tril(Q), a)
    a_ct = a_c.T
    a_last = a_c[Q - 1:Q, :]
    ea_x = _dot3(jnp.exp(a_c), e8)
    dec_x = _dot3(jnp.exp(a_last - a_c), e8)
    xdt = xs * _dot3(dtm, e8)
    xdt_b = xdt.astype(BF16)
    xd_b = (xdt * dec_x).astype(BF16)
    cd_x = ea_x[Q - 1:Q, :]

    r_i = lax.broadcasted_iota(jnp.int32, (Q, Q), 0)
    c_i = lax.broadcasted_iota(jnp.int32, (Q, Q), 1)
    tri = c_i <= r_i
    lane_q = lax.broadcasted_iota(jnp.int32, (Q, LANES), 1)
    gw = D_SSM // N_SSM_GROUPS
    hpg = N_SSM_HEADS // N_SSM_GROUPS
    y_parts, ht_parts = [], []
    for g in range(N_SSM_GROUPS):
        bm = xc[:, D_SSM + g * D_STATE:D_SSM + (g + 1) * D_STATE]
        cm = xc[:, D_SSM + (N_SSM_GROUPS + g) * D_STATE:D_SSM + (N_SSM_GROUPS + g + 1) * D_STATE]
        cb16 = cm.astype(BF16)
        cbm = _dot_nt(cb16, bm.astype(BF16))
        y_off = _dot(cb16, ht_old[:, g * gw:(g + 1) * gw].astype(BF16))
        y_diag = []
        for pr in range(hpg // 2):
            halves = []
            for hh in range(2):
                h = g * hpg + pr * 2 + hh
                col = a_c[:, DT_LANE0 + h:DT_LANE0 + h + 1]
                row = a_ct[DT_LANE0 + h:DT_LANE0 + h + 1, :]
                lm = jnp.where(tri, jnp.exp(col - row), 0.0)
                sc = (cbm * lm).astype(BF16)
                lo = g * gw + pr * LANES
                halves.append(_dot(sc, xdt_b[:, lo:lo + LANES]))
            y_diag.append(jnp.where(lane_q < SSM_HEAD_DIM, halves[0], halves[1]))
        y_parts.append(jnp.concatenate(y_diag, axis=1) + y_off * ea_x[:, g * gw:(g + 1) * gw])
        new = _dot(bm.T.astype(BF16), xd_b[:, g * gw:(g + 1) * gw])
        ht_parts.append(ht_old[:, g * gw:(g + 1) * gw] * cd_x[:, g * gw:(g + 1) * gw] + new)
    y = jnp.concatenate(y_parts, axis=1) + dsk * xs
    return _gate_groupnorm(y, z, sg).astype(BF16), jnp.concatenate(ht_parts, axis=1)


def _ssd_body(xc_ref, slab_ref, z_ref, arow_ref, dsk_ref, sg_ref, e8_ref, gn_ref, st_ref, ht_scr,
              *, n_steps, cps):
    c = pl.program_id(1)
    Q = SSD_CHUNK

    @pl.when(c == 0)
    def _():
        ht_scr[...] = jnp.zeros_like(ht_scr)

    ht = ht_scr[...]
    for ci in range(cps):
        rows = slice(ci * Q, (ci + 1) * Q)
        gn, ht = _ssd_chunk(xc_ref[0, rows, :], slab_ref[0, rows, :], z_ref[0, rows, :], ht,
                            arow_ref[...], dsk_ref[...], sg_ref[...], e8_ref[...])
        gn_ref[0, rows, :] = gn
    ht_scr[...] = ht

    @pl.when(c == n_steps - 1)
    def _():
        st_ref[0] = ht.T


def _ssd_prompt(xc, slab, z, a_row, dsk_x, ssm_g, *, cps):
    b, L, _ = xc.shape
    cps = min(cps, L // SSD_CHUNK)
    rows = cps * SSD_CHUNK
    assert L % rows == 0
    ns = L // rows
    e8 = _head_expand_mat()
    blk = lambda w: pl.BlockSpec((1, rows, w), lambda bi, ci: (bi, ci, 0))
    full = lambda a: pl.BlockSpec(a.shape, lambda bi, ci: (0, 0))
    return pl.pallas_call(
        functools.partial(_ssd_body, n_steps=ns, cps=cps),
        grid=(b, ns),
        in_specs=[blk(CONV_DIM), blk(LANES), blk(D_SSM), full(a_row), full(dsk_x), full(ssm_g), full(e8)],
        out_specs=[blk(D_SSM), pl.BlockSpec((1, D_SSM, D_STATE), lambda bi, ci: (bi, 0, 0))],
        out_shape=[jax.ShapeDtypeStruct((b, L, D_SSM), BF16),
                   jax.ShapeDtypeStruct((b, D_SSM, D_STATE), F32)],
        scratch_shapes=[pltpu.VMEM((D_STATE, D_SSM), F32)],
        compiler_params=_cparams(("arbitrary", "arbitrary")),
        name="ssd_prompt",
    )(xc, slab, z, a_row, dsk_x, ssm_g, e8)


def _ssd_step_body(xbc_ref, sc_ref, slab_ref, z_ref, st_ref, cw_ref, cb_ref, arow_ref, dsk_ref,
                   sg_ref, e8_ref, gn_ref, so_ref, xc_scr, dtx_scr, decx_scr, *, spb):
    step = pl.program_id(0)

    @pl.when(step == 0)
    def _():
        conv = (cb_ref[...] + cw_ref[0:1, :] * sc_ref[0] + cw_ref[1:2, :] * sc_ref[1]
                + cw_ref[2:3, :] * sc_ref[2] + cw_ref[3:4, :] * xbc_ref[...])
        xc_scr[...] = _silu(conv)
        slab = slab_ref[...]
        lane = lax.broadcasted_iota(jnp.int32, slab.shape, 1)
        dtm = jnp.where((lane >= DT_LANE0) & (lane < 2 * DT_LANE0), slab, 0.0)
        e8 = e8_ref[...]
        dtx_scr[...] = _dot3(dtm, e8)
        decx_scr[...] = _dot3(jnp.exp(dtm * arow_ref[...]), e8)

    gw = D_SSM // N_SSM_GROUPS
    for i in range(spb):
        b = step * spb + i
        xrow = xc_scr[pl.ds(b, 1), :]
        xs = xrow[:, :D_SSM]
        xdt = xs * dtx_scr[pl.ds(b, 1), :]
        dec = decx_scr[pl.ds(b, 1), :]
        y_rows = []
        for j in range(D_SSM // LANES):
            g = (j * LANES) // gw
            bm = xrow[:, D_SSM + g * D_STATE:D_SSM + (g + 1) * D_STATE]
            cm = xrow[:, D_SSM + (N_SSM_GROUPS + g) * D_STATE:D_SSM + (N_SSM_GROUPS + g + 1) * D_STATE]
            xcol = _row_to_col(xdt[:, j * LANES:(j + 1) * LANES])
            dcol = _row_to_col(dec[:, j * LANES:(j + 1) * LANES])
            hs = dcol * st_ref[i, j * LANES:(j + 1) * LANES, :] + xcol * bm
            so_ref[i, j * LANES:(j + 1) * LANES, :] = hs
            y_rows.append(_col_to_row(jnp.sum(hs * cm, axis=1, keepdims=True)))
        y = jnp.concatenate(y_rows, axis=1) + dsk_ref[...] * xs
        gn_ref[i] = _gate_groupnorm(y, z_ref[pl.ds(b, 1), :], sg_ref[...]).astype(BF16)


def _ssd_step(xbc, sconv_t, slab, z, state, conv_w, conv_b, a_row, dsk_x, ssm_g, *, spb):
    nb = xbc.shape[0]
    assert nb % spb == 0
    e8 = _head_expand_mat()
    full2 = lambda a: pl.BlockSpec(a.shape, lambda bi: (0,) * a.ndim)
    gn, st = pl.pallas_call(
        functools.partial(_ssd_step_body, spb=spb),
        grid=(nb // spb,),
        in_specs=[full2(xbc), full2(sconv_t), full2(slab), full2(z),
                  pl.BlockSpec((spb, D_SSM, D_STATE), lambda bi: (bi, 0, 0)),
                  full2(conv_w), full2(conv_b), full2(a_row), full2(dsk_x), full2(ssm_g), full2(e8)],
        out_specs=[pl.BlockSpec((spb, 1, D_SSM), lambda bi: (bi, 0, 0)),
                   pl.BlockSpec((spb, D_SSM, D_STATE), lambda bi: (bi, 0, 0))],
        out_shape=[jax.ShapeDtypeStruct((nb, 1, D_SSM), BF16),
                   jax.ShapeDtypeStruct((nb, D_SSM, D_STATE), F32)],
        scratch_shapes=[pltpu.VMEM((nb, CONV_DIM), F32), pltpu.VMEM((nb, D_SSM), F32),
                        pltpu.VMEM((nb, D_SSM), F32)],
        compiler_params=_cparams(("arbitrary",)),
        name="ssd_step",
    )(xbc, sconv_t, slab, z, state, conv_w, conv_b, a_row, dsk_x, ssm_g, e8)
    return gn.reshape(nb, D_SSM), st


DECODE_GROUP = 16
DECODE_SLOTS = 3


def _fox_body(pt_ref, q_ref, k_ref, v_ref, aq_ref, ak_ref, qs_ref, ks_ref, vs_ref, lfs_ref,
              ck_hbm, cv_hbm, clf_hbm, o_ref, os_ref,
              s_scr, m_scr, acc_scr, qp_scr, kbuf, vbuf, lbuf, sems, qblk_scr, dm_scr, dl_scr, dacc_scr,
              carry_scr, *, tq, nt, gps, gpseq, n_groups):
    G, NS = DECODE_GROUP, DECODE_SLOTS
    H, HD, P = N_ATT_HEADS, HEAD_DIM, PAGE_SIZE
    t = pl.program_id(2)
    sid = (pl.program_id(0) * pl.num_programs(1) + pl.program_id(1)) * pl.num_programs(2) + t
    nl = tq // LANES

    def group_copies(gg):
        slot = gg % NS
        src = jnp.minimum(gg, n_groups - 1)
        seq = src // gpseq
        base = (gpseq - 1 - src % gpseq) * G
        copies = []
        for i in range(G):
            page = pt_ref[seq, base + i]
            copies.append(pltpu.make_async_copy(ck_hbm.at[page], kbuf.at[slot, i], sems.at[slot, 0]))
            copies.append(pltpu.make_async_copy(cv_hbm.at[page], vbuf.at[slot, i], sems.at[slot, 1]))
            copies.append(pltpu.make_async_copy(clf_hbm.at[page], lbuf.at[slot, i], sems.at[slot, 2]))
        return copies

    def start_group(gg):
        for n, cp in enumerate(group_copies(gg)):
            cp.start(priority=1 if n % 3 == 1 else 0)

    def wait_group(gg):
        for cp in group_copies(gg):
            cp.wait()

    r8 = lax.broadcasted_iota(jnp.int32, (H, D_ATT), 0)
    c8 = lax.broadcasted_iota(jnp.int32, (H, D_ATT), 1)
    own_head = c8 // HD == r8

    def decode_init():
        seq = (sid * gps) // gpseq
        qs = (qs_ref[seq] * ATT_SCALE).astype(BF16).astype(F32)
        qblk = jnp.where(own_head, qs, 0.0)
        qblk_scr[...] = qblk.astype(BF16)
        k2 = ks_ref[seq].astype(BF16).astype(F32)
        dm_scr[...] = jnp.broadcast_to(jnp.sum(qblk * k2, axis=1, keepdims=True), dm_scr.shape)
        dl_scr[...] = jnp.ones_like(dl_scr)
        dacc_scr[...] = jnp.broadcast_to(vs_ref[seq].astype(BF16).astype(F32), dacc_scr.shape)
        carry_scr[...] = jnp.broadcast_to(_row_to_col(lfs_ref[seq])[0:H, :], carry_scr.shape)

    def decode_group(slot):
        kcat = jnp.concatenate([kbuf[slot, i].astype(BF16) for i in range(G)], axis=1)
        vcat = jnp.concatenate([vbuf[slot, i].astype(BF16) for i in range(G)], axis=1)
        s = _dot(qblk_scr[...], kcat)
        x = jnp.concatenate([lbuf[slot, i] for i in range(G)], axis=0)
        rr = lax.broadcasted_iota(jnp.int32, (P, P), 0)
        cc = lax.broadcasted_iota(jnp.int32, (P, P), 1)
        rev_local = _dot3(x, (rr > cc).astype(BF16))
        tot = jnp.sum(x, axis=1, keepdims=True)
        carry = carry_scr[:, 0:1]
        s_pages = [None] * G
        for i in reversed(range(G)):
            s_pages[i] = s[:, i * P:(i + 1) * P] + rev_local[i * H:(i + 1) * H, :] + carry
            carry = carry + tot[i * H:(i + 1) * H, :]
        carry_scr[...] = jnp.broadcast_to(carry, carry_scr.shape)
        st = jnp.concatenate(s_pages, axis=1)
        m_old = dm_scr[:, 0:1]
        m_new = jnp.maximum(m_old, jnp.max(st, axis=1, keepdims=True))
        alpha = jnp.exp(m_old - m_new)
        p = jnp.exp(st - m_new)
        l_new = alpha * dl_scr[:, 0:1] + jnp.sum(p, axis=1, keepdims=True)
        dacc_scr[...] = alpha * dacc_scr[...] + _dot_nt(p.astype(BF16), vcat)
        dm_scr[...] = jnp.broadcast_to(m_new, dm_scr.shape)
        dl_scr[...] = jnp.broadcast_to(l_new, dl_scr.shape)

    def decode_finish():
        o = dacc_scr[...] / dl_scr[:, 0:1]
        os_ref[0] = jnp.sum(jnp.where(own_head, o, 0.0), axis=0, keepdims=True)

    r = lax.broadcasted_iota(jnp.int32, (tq, tq), 0)
    c = lax.broadcasted_iota(jnp.int32, (tq, tq), 1)
    causal = c <= r

    def chunk_of(ci):
        first = ci <= t
        sel = jnp.where(first, 0, 1)
        q0 = pl.multiple_of(jnp.where(first, t, nt - 1 - t) * tq, tq)
        k0 = pl.multiple_of(jnp.where(first, ci, ci - t - 1) * tq, tq)
        return sel, q0, k0

    lane_q = lax.broadcasted_iota(jnp.int32, (tq, LANES), 1)
    even_dims = lane_q < HD

    def per_head(x2, a2):
        return jnp.where(even_dims, x2, a2), jnp.where(even_dims, a2, x2)

    def build_queries():
        for x, tile in enumerate((t, nt - 1 - t)):
            rows = pl.ds(pl.multiple_of(tile * tq, tq), tq)
            q_e, q_o = per_head(q_ref[0, rows, :], aq_ref[0, rows, :])
            qp_scr[x, 0] = q_e
            qp_scr[x, 1] = q_o

    def scores(ci):
        sel, q0, k0 = chunk_of(ci)
        ks = per_head(k_ref[0, pl.ds(k0, tq), :], ak_ref[0, pl.ds(k0, tq), :])
        for j in range(2):
            s = _dot_nt(qp_scr[sel, j], ks[j])
            if ci == nt:
                s = jnp.where(causal, s, NEG_BIG)
            elif ci < nt // 2:
                s = jnp.where(jnp.logical_or(causal, ci != t), s, NEG_BIG)
            s_scr[j, ci] = s
            m = m_scr[sel, j]
            for u in range(nl):
                m = jnp.maximum(m, s[:, u * LANES:(u + 1) * LANES])
            m_scr[sel, j] = m

    def row_max():
        for x in range(2):
            for j in range(2):
                m_scr[x, j] = jnp.broadcast_to(jnp.max(m_scr[x, j], axis=1, keepdims=True), (tq, LANES))

    one_at = lambda ln: jnp.where(lane_q == ln, 1.0, 0.0).astype(BF16)
    ones_e, ones_o = one_at(HD), one_at(0)

    def weighted(ci):
        sel, q0, k0 = chunk_of(ci)
        v2 = v_ref[0, pl.ds(k0, tq), :]
        vs = (jnp.where(even_dims, v2, ones_e), jnp.where(even_dims, ones_o, v2))
        for j in range(2):
            s = s_scr[j, ci]
            mrep = m_scr[sel, j]
            p = jnp.concatenate([jnp.exp2(s[:, u * LANES:(u + 1) * LANES] - mrep) for u in range(nl)], axis=1)
            acc_scr[sel, j] += _dot(p.astype(BF16), vs[j])

    def write_out():
        lane = lax.broadcasted_iota(jnp.int32, (tq, LANES), 1)
        for x, tile in enumerate((t, nt - 1 - t)):
            acc_e, acc_o = acc_scr[x, 0], acc_scr[x, 1]
            out = jnp.where(lane < HEAD_DIM, acc_e / acc_e[:, HEAD_DIM:HEAD_DIM + 1], acc_o / acc_o[:, 0:1])
            o_ref[0, pl.ds(pl.multiple_of(tile * tq, tq), tq), :] = out

    items = ([functools.partial(scores, ci) for ci in range(nt + 1)] + [row_max]
             + [functools.partial(weighted, ci) for ci in range(nt + 1)] + [write_out])
    per_part = -(-len(items) // gps)
    parts = [items[i * per_part:(i + 1) * per_part] for i in range(gps)]

    @pl.when(sid == 0)
    def _():
        for g0 in range(NS - 1):
            start_group(g0)

    pl.when((sid * gps) % gpseq == 0)(decode_init)
    m_scr[...] = jnp.full(m_scr.shape, NEG_BIG, F32)
    acc_scr[...] = jnp.zeros_like(acc_scr)
    build_queries()
    for gi in range(gps):
        gg = sid * gps + gi
        wait_group(gg)
        start_group(gg + (NS - 1))
        for item in parts[gi]:
            item()
        decode_group(gg % NS)
    pl.when((sid * gps + gps - 1) % gpseq == gpseq - 1)(decode_finish)

    @pl.when(sid == pl.num_programs(0) * pl.num_programs(1) * pl.num_programs(2) - 1)
    def _():
        for extra in range(NS - 1):
            wait_group(n_groups + extra)


def _fox_attention(q16, k16, v16, aq, ak, q_s, k_s, v_s, slab_s, cache_k, cache_v, cache_logf, page_table,
                   *, tq):
    b, L, _ = q16.shape
    nt = L // tq
    nb, n_pages = page_table.shape
    n_steps = b * (N_ATT_HEADS // 2) * (nt // 2)
    n_groups = nb * n_pages // DECODE_GROUP
    gps = n_groups // n_steps
    gpseq = n_pages // DECODE_GROUP
    assert nt % 2 == 0 and n_pages % DECODE_GROUP == 0
    assert gps * n_steps == n_groups and gpseq % gps == 0 and n_groups >= DECODE_SLOTS
    spq = gpseq // gps
    hp_n, t_n = N_ATT_HEADS // 2, nt // 2
    seq_of = lambda bi, hp, t: ((bi * hp_n + hp) * t_n + t) // spq
    pair = pl.BlockSpec((1, L, LANES), lambda bi, hp, t, pt: (bi, 0, hp), pipeline_mode=pl.Buffered(1))
    srow = lambda w: pl.BlockSpec((nb, 1, w), lambda bi, hp, t, pt: (0, 0, 0), pipeline_mode=pl.Buffered(1))
    anyspec = pl.BlockSpec(memory_space=pl.ANY)
    row3 = lambda a: a.reshape(nb, 1, a.shape[-1])
    G, NS = DECODE_GROUP, DECODE_SLOTS
    att, att_s = pl.pallas_call(
        functools.partial(_fox_body, tq=tq, nt=nt, gps=gps, gpseq=gpseq, n_groups=n_groups),
        grid_spec=pltpu.PrefetchScalarGridSpec(
            num_scalar_prefetch=1,
            grid=(b, hp_n, t_n),
            in_specs=[pair, pair, pair, pair, pair, srow(D_ATT), srow(D_ATT), srow(D_ATT), srow(LANES),
                      anyspec, anyspec, anyspec],
            out_specs=[pl.BlockSpec((1, L, LANES), lambda bi, hp, t, pt: (bi, 0, hp)),
                       pl.BlockSpec((1, 1, D_ATT), lambda bi, hp, t, pt: (seq_of(bi, hp, t), 0, 0))],
            scratch_shapes=[
                pltpu.VMEM((2, nt + 1, tq, tq), F32), pltpu.VMEM((2, 2, tq, LANES), F32),
                pltpu.VMEM((2, 2, tq, LANES), F32), pltpu.VMEM((2, 2, tq, LANES), BF16),
                pltpu.VMEM((NS, G, D_ATT, PAGE_SIZE), F32), pltpu.VMEM((NS, G, D_ATT, PAGE_SIZE), F32),
                pltpu.VMEM((NS, G, N_ATT_HEADS, PAGE_SIZE), F32), pltpu.SemaphoreType.DMA((NS, 3)),
                pltpu.VMEM((N_ATT_HEADS, D_ATT), BF16), pltpu.VMEM((8, LANES), F32),
                pltpu.VMEM((8, LANES), F32), pltpu.VMEM((8, D_ATT), F32), pltpu.VMEM((8, LANES), F32)]),
        out_shape=[jax.ShapeDtypeStruct((b, L, D_ATT), F32), jax.ShapeDtypeStruct((nb, 1, D_ATT), F32)],
        compiler_params=_cparams(("arbitrary", "arbitrary", "arbitrary")),
        name="fox_attention",
    )(page_table, q16, k16, v16, aq, ak, row3(q_s), row3(k_s), row3(v_s), row3(slab_s),
      cache_k, cache_v, cache_logf)
    return att, att_s.reshape(nb, D_ATT)


def _lane_pad(vec, lane0):
    return jnp.pad(vec.astype(F32)[None, :], ((0, 0), (lane0, LANES - lane0 - vec.shape[0])))


def kernel(x_prompt, x_sample, cache_k, cache_v, cache_logf, page_table, state_conv, state_ssm,
           ffn1_norm, w_ffn1_in, w_ffn1_out, mix_norm, w_in, b_f, conv_w, conv_b, dt_bias, a_log,
           d_skip, att_out_norm, ssm_out_norm, w_out, ffn2_norm, w_ffn2_in, w_ffn2_out, final_norm):
    depth = w_in.shape[0]
    assert depth == 1
    B, L, _ = x_prompt.shape
    nb, T, _ = x_sample.shape
    assert T == 1
    H, HD = N_ATT_HEADS, HEAD_DIM
    assert H == DT_LANE0

    l0 = 0
    w1i, w1o = w_ffn1_in[l0].astype(BF16), w_ffn1_out[l0].astype(BF16)
    wi = w_in[l0]
    o_f = 3 * D_ATT
    o_z = o_f + H
    o_x = o_z + D_SSM
    o_dt = o_x + CONV_DIM
    w_small = jnp.concatenate(
        [wi[:, o_f:o_f + H], wi[:, o_dt:o_dt + N_SSM_HEADS],
         jnp.zeros((D_MODEL, LANES - DT_LANE0 - N_SSM_HEADS), wi.dtype)], axis=1)
    ws = tuple(w.astype(BF16) for w in (wi[:, 0:D_ATT], wi[:, D_ATT:2 * D_ATT], wi[:, 2 * D_ATT:3 * D_ATT],
                                        wi[:, o_z:o_z + D_SSM], wi[:, o_x:o_x + CONV_DIM], w_small))
    bias_s = _lane_pad(b_f[l0], 0) + _lane_pad(dt_bias[l0], DT_LANE0)
    a_row = _lane_pad(-jnp.exp(a_log[l0].astype(F32)), DT_LANE0)
    dsk_x = jnp.repeat(d_skip[l0].astype(F32), SSM_HEAD_DIM)[None, :]
    row = lambda v: v.astype(F32)[None, :]
    g1, gm, g2, gf = row(ffn1_norm[l0]), row(mix_norm[l0]), row(ffn2_norm[l0]), row(final_norm)
    ga, gs = row(att_out_norm[l0]), row(ssm_out_norm[l0])
    cw, cb = conv_w[l0].astype(F32), row(conv_b[l0])
    wm = w_out[l0].astype(BF16)

    TM, TF = FFN_ROWS, FFN_CHUNK
    xp = x_prompt.reshape(B * L, D_MODEL)
    x1 = _ffn(xp, g1, w1i, w1o, tm=TM, tf=TF)
    (k_p, v_p, q16, k16, v16, aq, ak, z_p, xc_p, tail_p, slab_p, lf_p, w2i, w2o) = _inproj_aug(
        x1, gm, ws, bias_s, cw, cb, w_ffn2_in[l0].astype(F32), w_ffn2_out[l0].astype(F32),
        tm=INPROJ_ROWS, seq=L)
    xs = x_sample.reshape(nb, D_MODEL)
    x1s = _ffn(xs, g1, w1i, w1o, tm=nb, tf=TF)
    q_s, k_s, v_s, z_s, xbc_s, slab_s = _inproj_plain(x1s, gm, ws, bias_s)

    n_pool = cache_k.shape[1]
    ck = jnp.transpose(cache_k[l0], (0, 2, 3, 1)).reshape(n_pool, D_ATT, PAGE_SIZE)
    cv = jnp.transpose(cache_v[l0], (0, 2, 3, 1)).reshape(n_pool, D_ATT, PAGE_SIZE)
    clf = jnp.transpose(cache_logf[l0], (0, 2, 1))
    seq3 = lambda a: a.reshape(B, L, a.shape[-1])
    att, att_s = _fox_attention(seq3(q16), seq3(k16), seq3(v16), seq3(aq), seq3(ak),
                                q_s, k_s, v_s, slab_s, ck, cv, clf, page_table, tq=ATT_ROWS)

    gn_p, st_p = _ssd_prompt(seq3(xc_p), seq3(slab_p), seq3(z_p), a_row, dsk_x, gs, cps=SSD_CHUNKS_PER_STEP)
    y_p = _ffn2(x1, att.reshape(B * L, D_ATT), gn_p.reshape(B * L, D_SSM), ga, wm, g2, w2i, w2o, gf,
                tm=TM, tf=TF)

    sconv_t = jnp.swapaxes(state_conv[l0], 0, 1)
    gn_s, st_s = _ssd_step(xbc_s, sconv_t, slab_s, z_s,
                           state_ssm[l0].reshape(nb, D_SSM, D_STATE), cw, cb, a_row, dsk_x, gs,
                           spb=SSD_STEP_SEQS if nb % SSD_STEP_SEQS == 0 else 1)
    y_s = _ffn2(x1s, att_s, gn_s, ga, wm, g2, w2i, w2o, gf, tm=nb, tf=TF)

    return (
        y_p.reshape(B, L, D_MODEL),
        y_s.reshape(nb, 1, D_MODEL),
        jnp.transpose(k_p.reshape(1, B, H, HD, L), (0, 1, 4, 2, 3)),
        jnp.transpose(v_p.reshape(1, B, H, HD, L), (0, 1, 4, 2, 3)),
        jnp.transpose(lf_p.reshape(1, B, H, L), (0, 1, 3, 2)),
        tail_p[:, 8 - (CONV_W - 1):, :][None],
        st_p.reshape(1, B, N_SSM_HEADS, SSM_HEAD_DIM, D_STATE),
        k_s.reshape(1, nb, 1, H, HD),
        v_s.reshape(1, nb, 1, H, HD),
        slab_s[:, :H].reshape(1, nb, 1, H),
        jnp.concatenate([state_conv[l0][:, 1:, :], xbc_s[:, None, :]], axis=1)[None],
        st_s.reshape(1, nb, N_SSM_HEADS, SSM_HEAD_DIM, D_STATE),
    )
```
